```python
import math
import jax, jax.numpy as jnp
from jax import lax
import numpy as np

D_MODEL = 1024
BATCH = 8
SEQ = 4096
DEPTH = 4

CHUNK = 64
N_MIXERS = 3
EPS = 1e-6

POOL_WINDOWS = (2, 4, 8, 16)
N_POOL_GROUPS = len(POOL_WINDOWS)
POOL_GROUP = D_MODEL // N_POOL_GROUPS

SB_HEAD_DIM = 64
SB_HEADS = D_MODEL // SB_HEAD_DIM
Q_BLOCK = 128

S5_GROUP = 16
S5_GROUPS = D_MODEL // S5_GROUP
S5_STATE = 64
S5_DT_MIN = 1e-3
S5_DT_MAX = 1e-1

MEM_LEN = 256
XA_HEADS = 4
XA_HEAD_DIM = D_MODEL // XA_HEADS

D_FF = ((8 * D_MODEL // 3 + 127) // 128) * 128
CONV_WIDTH = 3

N_POOL_LAYERS = (DEPTH + 2) // 3
N_SB_LAYERS = (DEPTH + 1) // 3
N_S5_LAYERS = DEPTH // 3

kernel_name = "hybrid_pool_stickbreak_s5_streaming_trunk"


def rms_norm(x, g):
    xf = x.astype(jnp.float32)
    y = xf * lax.rsqrt(jnp.mean(xf * xf, axis=-1, keepdims=True) + EPS)
    return (y * g.astype(jnp.float32)).astype(x.dtype)


def causal_shift(x, k):
    return jnp.pad(x, ((0, 0), (k, 0), (0, 0)))[:, : x.shape[1]]


def pool_mixer(h, w, scale):
    B, S, D = h.shape
    hf = h.astype(jnp.float32)
    cs = jnp.cumsum(hf, axis=1)
    counts = jnp.arange(1, S + 1, dtype=jnp.float32)
    groups = []
    for gi, win in enumerate(POOL_WINDOWS):
        sl = slice(gi * POOL_GROUP, (gi + 1) * POOL_GROUP)
        c = cs[..., sl]
        win_sum = c - causal_shift(c, win)
        cnt = jnp.minimum(counts, float(win))[None, :, None]
        groups.append(win_sum / cnt - hf[..., sl])
    p = jnp.stack(groups, axis=2).astype(h.dtype)
    y = jnp.einsum('bsgc,gcd->bsgd', p, w).reshape(B, S, D)
    return y * scale


def stick_breaking_attention(h, w_qkv, w_o):
    B, S, D = h.shape
    qkv = (h @ w_qkv).reshape(B, S, 3, SB_HEADS, SB_HEAD_DIM)
    q, k, v = qkv[:, :, 0], qkv[:, :, 1], qkv[:, :, 2]
    scale = SB_HEAD_DIM ** -0.5
    outs = []
    for i in range(S // Q_BLOCK):
        q0 = i * Q_BLOCK
        kend = q0 + Q_BLOCK
        z = jnp.einsum('bqhd,bkhd->bhqk', q[:, q0:kend], k[:, :kend]).astype(jnp.float32) * scale
        qpos = q0 + jnp.arange(Q_BLOCK)
        kpos = jnp.arange(kend)
        mask = kpos[None, :] < qpos[:, None]
        log_one_minus = jnp.where(mask, jax.nn.log_sigmoid(-z), 0.0)
        between = lax.cumsum(log_one_minus, axis=3, reverse=True) - log_one_minus
        a = jnp.where(mask, jnp.exp(jax.nn.log_sigmoid(z) + between), 0.0)
        outs.append(jnp.einsum('bhqk,bkhd->bqhd', a.astype(v.dtype), v[:, :kend]))
    o = jnp.concatenate(outs, axis=1).reshape(B, S, D)
    return o @ w_o


def _lin_rec_combine(left, right):
    a1, b1 = left
    a2, b2 = right
    return a1 * a2, a2 * b1 + b2


def s5_mixer(h, a_re, a_im, log_dt, b_re, b_im, c_re, c_im, d, w_glu):
    B, S, D = h.shape
    f32 = jnp.float32
    u = h.astype(f32)
    lam = lax.complex(a_re.astype(f32), a_im.astype(f32))
    dt = jnp.exp(log_dt.astype(f32))[:, None]
    dt_lam = lam * dt
    a_bar = jnp.exp(dt_lam)
    b_bar = ((a_bar - 1.0) / lam)[..., None] * lax.complex(b_re.astype(f32), b_im.astype(f32))
    c_mat = lax.complex(c_re.astype(f32), c_im.astype(f32))
    steps = jnp.arange(1, CHUNK + 1, dtype=f32)
    a_pow = jnp.exp(steps[:, None, None] * dt_lam[None])
    n_chunks = S // CHUNK
    u_chunks = u.reshape(B, n_chunks, CHUNK, S5_GROUPS, S5_GROUP).transpose(1, 0, 2, 3, 4)

    def chunk_step(state, u_blk):
        bu = jnp.einsum('bcgi,gpi->bcgp', u_blk.astype(jnp.complex64), b_bar)
        a_seq = jnp.broadcast_to(a_bar, bu.shape)
        _, xs = lax.associative_scan(_lin_rec_combine, (a_seq, bu), axis=1)
        xs = xs + a_pow[None] * state[:, None]
        y = jnp.real(jnp.einsum('bcgp,gip->bcgi', xs, c_mat))
        return xs[:, -1], y

    state0 = jnp.zeros((B, S5_GROUPS, S5_STATE), jnp.complex64)
    _, ys = lax.scan(chunk_step, state0, u_chunks)
    y = ys.transpose(1, 0, 2, 3, 4).reshape(B, S, D) + d.astype(f32) * u
    y = jax.nn.gelu(y).astype(h.dtype)
    val, gate = jnp.split(y @ w_glu, 2, axis=-1)
    return val * jax.nn.sigmoid(gate)


def memory_cross_attention(h, mem_n, wq, wkv, wo):
    B, S, D = h.shape
    M = mem_n.shape[1]
    q = (h @ wq).reshape(B, S, XA_HEADS, XA_HEAD_DIM)
    kv = (mem_n @ wkv).reshape(B, M, 2, XA_HEADS, XA_HEAD_DIM)
    k, v = kv[:, :, 0], kv[:, :, 1]
    s = jnp.einsum('bshd,bmhd->bhsm', q, k).astype(jnp.float32) * (XA_HEAD_DIM ** -0.5)
    p = jax.nn.softmax(s, axis=-1).astype(v.dtype)
    o = jnp.einsum('bhsm,bmhd->bshd', p, v).reshape(B, S, D)
    return o @ wo


def conv_glu_ffn(h, w_up, conv_w, conv_b, w_down):
    u = h @ w_up
    u = sum(conv_w[CONV_WIDTH - 1 - k] * causal_shift(u, k) for k in range(CONV_WIDTH)) + conv_b
    val, gate = jnp.split(u, 2, axis=-1)
    return (jax.nn.silu(gate) * val) @ w_down


def _fwd_setup_inputs(seed: int = 0) -> dict:
    key = jax.random.key(seed)
    keys = iter(jax.random.split(key, 40))
    f32 = jnp.float32

    def nrm(shape, scale):
        return jax.random.normal(next(keys), shape, f32) * scale

    def gain(shape):
        return 1.0 + nrm(shape, 0.02)

    D, F = D_MODEL, D_FF
    G, P, Cg = S5_GROUPS, S5_STATE, S5_GROUP
    a_im_init = jnp.pi * jnp.arange(P, dtype=f32)
    return {
        "x": nrm((BATCH, SEQ, D), 1.0),
        "mem": nrm((BATCH, MEM_LEN, D), 1.0),
        "mix_norm_g": gain((DEPTH, D)),
        "pool_w": nrm((N_POOL_LAYERS, N_POOL_GROUPS, POOL_GROUP, POOL_GROUP), POOL_GROUP ** -0.5),
        "pool_scale": gain((N_POOL_LAYERS, D)),
        "sb_w_qkv": nrm((N_SB_LAYERS, D, 3 * D), D ** -0.5),
        "sb_w_o": nrm((N_SB_LAYERS, D, D), D ** -0.5),
        "s5_a_re": -0.5 + nrm((N_S5_LAYERS, G, P), 0.01),
        "s5_a_im": a_im_init + nrm((N_S5_LAYERS, G, P), 0.01),
        "s5_log_dt": jax.random.uniform(next(keys), (N_S5_LAYERS, G), f32,
                                        minval=math.log(S5_DT_MIN), maxval=math.log(S5_DT_MAX)),
        "s5_b_re": nrm((N_S5_LAYERS, G, P, Cg), (2 * Cg) ** -0.5),
        "s5_b_im": nrm((N_S5_LAYERS, G, P, Cg), (2 * Cg) ** -0.5),
        "s5_c_re": nrm((N_S5_LAYERS, G, Cg, P), (2 * P) ** -0.5),
        "s5_c_im": nrm((N_S5_LAYERS, G, Cg, P), (2 * P) ** -0.5),
        "s5_d": nrm((N_S5_LAYERS, D), 1.0),
        "s5_w_glu": nrm((N_S5_LAYERS, D, 2 * D), D ** -0.5),
        "xa_norm_g": gain((DEPTH, D)),
        "mem_norm_g": gain((DEPTH, D)),
        "xa_wq": nrm((DEPTH, D, D), D ** -0.5),
        "xa_wkv": nrm((DEPTH, D, 2 * D), D ** -0.5),
        "xa_wo": nrm((DEPTH, D, D), D ** -0.5),
        "ffn_norm_g": gain((DEPTH, D)),
        "ffn_w_up": nrm((DEPTH, D, 2 * F), D ** -0.5),
        "ffn_conv_w": nrm((DEPTH, CONV_WIDTH, 2 * F), CONV_WIDTH ** -0.5),
        "ffn_conv_b": nrm((DEPTH, 2 * F), 0.01),
        "ffn_w_down": nrm((DEPTH, F, D), F ** -0.5),
        "final_norm_g": gain((D,)),
    }


def _fwd_reference(x, mem, mix_norm_g, pool_w, pool_scale, sb_w_qkv, sb_w_o,
              s5_a_re, s5_a_im, s5_log_dt, s5_b_re, s5_b_im, s5_c_re, s5_c_im, s5_d, s5_w_glu,
              xa_norm_g, mem_norm_g, xa_wq, xa_wkv, xa_wo,
              ffn_norm_g, ffn_w_up, ffn_conv_w, ffn_conv_b, ffn_w_down, final_norm_g):
    h = x
    for i in range(DEPTH):
        kind = i % N_MIXERS
        j = i // N_MIXERS
        hn = rms_norm(h, mix_norm_g[i])
        if kind == 0:
            t = pool_mixer(hn, pool_w[j], pool_scale[j])
        elif kind == 1:
            t = stick_breaking_attention(hn, sb_w_qkv[j], sb_w_o[j])
        else:
            t = s5_mixer(hn, s5_a_re[j], s5_a_im[j], s5_log_dt[j], s5_b_re[j], s5_b_im[j],
                         s5_c_re[j], s5_c_im[j], s5_d[j], s5_w_glu[j])
        h = h + t.astype(h.dtype)
        m = memory_cross_attention(rms_norm(h, xa_norm_g[i]), rms_norm(mem, mem_norm_g[i]),
                                   xa_wq[i], xa_wkv[i], xa_wo[i])
        h = h + m.astype(h.dtype)
        f = conv_glu_ffn(rms_norm(h, ffn_norm_g[i]), ffn_w_up[i], ffn_conv_w[i], ffn_conv_b[i], ffn_w_down[i])
        h = h + f.astype(h.dtype)
    return rms_norm(h, final_norm_g)


import jax as _jax
import jax.numpy as _jnp

TWIN_FORMAT = 'train_step'
FWD_PARAMS = ['x', 'mem', 'mix_norm_g', 'pool_w', 'pool_scale', 'sb_w_qkv', 'sb_w_o', 's5_a_re', 's5_a_im', 's5_log_dt', 's5_b_re', 's5_b_im', 's5_c_re', 's5_c_im', 's5_d', 's5_w_glu', 'xa_norm_g', 'mem_norm_g', 'xa_wq', 'xa_wkv', 'xa_wo', 'ffn_norm_g', 'ffn_w_up', 'ffn_conv_w', 'ffn_conv_b', 'ffn_w_down', 'final_norm_g']
TWIN_WEIGHTS = ['mix_norm_g', 'pool_w', 'pool_scale', 'sb_w_qkv', 'sb_w_o', 's5_a_re', 's5_a_im', 's5_log_dt', 's5_b_re', 's5_b_im', 's5_c_re', 's5_c_im', 's5_d', 's5_w_glu', 'xa_norm_g', 'mem_norm_g', 'xa_wq', 'xa_wkv', 'xa_wo', 'ffn_norm_g', 'ffn_w_up', 'ffn_conv_w', 'ffn_conv_b', 'ffn_w_down', 'final_norm_g']
TWIN_DIFF_INPUT = 'x'
TWIN_INPUTS = ['x', 'mem', 'mix_norm_g', 'pool_w', 'pool_scale', 'sb_w_qkv', 'sb_w_o', 's5_a_re', 's5_a_im', 's5_log_dt', 's5_b_re', 's5_b_im', 's5_c_re', 's5_c_im', 's5_d', 's5_w_glu', 'xa_norm_g', 'mem_norm_g', 'xa_wq', 'xa_wkv', 'xa_wo', 'ffn_norm_g', 'ffn_w_up', 'ffn_conv_w', 'ffn_conv_b', 'ffn_w_down', 'final_norm_g', 'loss_target', 'm_mix_norm_g', 'm_pool_w', 'm_pool_scale', 'm_sb_w_qkv', 'm_sb_w_o', 'm_s5_a_re', 'm_s5_a_im', 'm_s5_log_dt', 'm_s5_b_re', 'm_s5_b_im', 'm_s5_c_re', 'm_s5_c_im', 'm_s5_d', 'm_s5_w_glu', 'm_xa_norm_g', 'm_mem_norm_g', 'm_xa_wq', 'm_xa_wkv', 'm_xa_wo', 'm_ffn_norm_g', 'm_ffn_w_up', 'm_ffn_conv_w', 'm_ffn_conv_b', 'm_ffn_w_down', 'm_final_norm_g', 'v_mix_norm_g', 'v_pool_w', 'v_pool_scale', 'v_sb_w_qkv', 'v_sb_w_o', 'v_s5_a_re', 'v_s5_a_im', 'v_s5_log_dt', 'v_s5_b_re', 'v_s5_b_im', 'v_s5_c_re', 'v_s5_c_im', 'v_s5_d', 'v_s5_w_glu', 'v_xa_norm_g', 'v_mem_norm_g', 'v_xa_wq', 'v_xa_wkv', 'v_xa_wo', 'v_ffn_norm_g', 'v_ffn_w_up', 'v_ffn_conv_w', 'v_ffn_conv_b', 'v_ffn_w_down', 'v_final_norm_g']
TWIN_OUTPUTS = ['loss', 'grad_x', 'grad_mix_norm_g', 'grad_pool_w', 'grad_pool_scale', 'grad_sb_w_qkv', 'grad_sb_w_o', 'grad_s5_a_re', 'grad_s5_a_im', 'grad_s5_log_dt', 'grad_s5_b_re', 'grad_s5_b_im', 'grad_s5_c_re', 'grad_s5_c_im', 'grad_s5_d', 'grad_s5_w_glu', 'grad_xa_norm_g', 'grad_mem_norm_g', 'grad_xa_wq', 'grad_xa_wkv', 'grad_xa_wo', 'grad_ffn_norm_g', 'grad_ffn_w_up', 'grad_ffn_conv_w', 'grad_ffn_conv_b', 'grad_ffn_w_down', 'grad_final_norm_g', 'delta_mix_norm_g', 'delta_pool_w', 'delta_pool_scale', 'delta_sb_w_qkv', 'delta_sb_w_o', 'delta_s5_a_re', 'delta_s5_a_im', 'delta_s5_log_dt', 'delta_s5_b_re', 'delta_s5_b_im', 'delta_s5_c_re', 'delta_s5_c_im', 'delta_s5_d', 'delta_s5_w_glu', 'delta_xa_norm_g', 'delta_mem_norm_g', 'delta_xa_wq', 'delta_xa_wkv', 'delta_xa_wo', 'delta_ffn_norm_g', 'delta_ffn_w_up', 'delta_ffn_conv_w', 'delta_ffn_conv_b', 'delta_ffn_w_down', 'delta_final_norm_g', 'new_m_mix_norm_g', 'new_m_pool_w', 'new_m_pool_scale', 'new_m_sb_w_qkv', 'new_m_sb_w_o', 'new_m_s5_a_re', 'new_m_s5_a_im', 'new_m_s5_log_dt', 'new_m_s5_b_re', 'new_m_s5_b_im', 'new_m_s5_c_re', 'new_m_s5_c_im', 'new_m_s5_d', 'new_m_s5_w_glu', 'new_m_xa_norm_g', 'new_m_mem_norm_g', 'new_m_xa_wq', 'new_m_xa_wkv', 'new_m_xa_wo', 'new_m_ffn_norm_g', 'new_m_ffn_w_up', 'new_m_ffn_conv_w', 'new_m_ffn_conv_b', 'new_m_ffn_w_down', 'new_m_final_norm_g', 'new_v_mix_norm_g', 'new_v_pool_w', 'new_v_pool_scale', 'new_v_sb_w_qkv', 'new_v_sb_w_o', 'new_v_s5_a_re', 'new_v_s5_a_im', 'new_v_s5_log_dt', 'new_v_s5_b_re', 'new_v_s5_b_im', 'new_v_s5_c_re', 'new_v_s5_c_im', 'new_v_s5_d', 'new_v_s5_w_glu', 'new_v_xa_norm_g', 'new_v_mem_norm_g', 'new_v_xa_wq', 'new_v_xa_wkv', 'new_v_xa_wo', 'new_v_ffn_norm_g', 'new_v_ffn_w_up', 'new_v_ffn_conv_w', 'new_v_ffn_conv_b', 'new_v_ffn_w_down', 'new_v_final_norm_g']
TWIN_LEAF_KINDS = {'loss': 'loss', 'grad_x': 'grad_x', 'grad_mix_norm_g': 'grad_w', 'grad_pool_w': 'grad_w', 'grad_pool_scale': 'grad_w', 'grad_sb_w_qkv': 'grad_w', 'grad_sb_w_o': 'grad_w', 'grad_s5_a_re': 'grad_w', 'grad_s5_a_im': 'grad_w', 'grad_s5_log_dt': 'grad_w', 'grad_s5_b_re': 'grad_w', 'grad_s5_b_im': 'grad_w', 'grad_s5_c_re': 'grad_w', 'grad_s5_c_im': 'grad_w', 'grad_s5_d': 'grad_w', 'grad_s5_w_glu': 'grad_w', 'grad_xa_norm_g': 'grad_w', 'grad_mem_norm_g': 'grad_w', 'grad_xa_wq': 'grad_w', 'grad_xa_wkv': 'grad_w', 'grad_xa_wo': 'grad_w', 'grad_ffn_norm_g': 'grad_w', 'grad_ffn_w_up': 'grad_w', 'grad_ffn_conv_w': 'grad_w', 'grad_ffn_conv_b': 'grad_w', 'grad_ffn_w_down': 'grad_w', 'grad_final_norm_g': 'grad_w', 'delta_mix_norm_g': 'delta_w', 'delta_pool_w': 'delta_w', 'delta_pool_scale': 'delta_w', 'delta_sb_w_qkv': 'delta_w', 'delta_sb_w_o': 'delta_w', 'delta_s5_a_re': 'delta_w', 'delta_s5_a_im': 'delta_w', 'delta_s5_log_dt': 'delta_w', 'delta_s5_b_re': 'delta_w', 'delta_s5_b_im': 'delta_w', 'delta_s5_c_re': 'delta_w', 'delta_s5_c_im': 'delta_w', 'delta_s5_d': 'delta_w', 'delta_s5_w_glu': 'delta_w', 'delta_xa_norm_g': 'delta_w', 'delta_mem_norm_g': 'delta_w', 'delta_xa_wq': 'delta_w', 'delta_xa_wkv': 'delta_w', 'delta_xa_wo': 'delta_w', 'delta_ffn_norm_g': 'delta_w', 'delta_ffn_w_up': 'delta_w', 'delta_ffn_conv_w': 'delta_w', 'delta_ffn_conv_b': 'delta_w', 'delta_ffn_w_down': 'delta_w', 'delta_final_norm_g': 'delta_w', 'new_m_mix_norm_g': 'new_m', 'new_m_pool_w': 'new_m', 'new_m_pool_scale': 'new_m', 'new_m_sb_w_qkv': 'new_m', 'new_m_sb_w_o': 'new_m', 'new_m_s5_a_re': 'new_m', 'new_m_s5_a_im': 'new_m', 'new_m_s5_log_dt': 'new_m', 'new_m_s5_b_re': 'new_m', 'new_m_s5_b_im': 'new_m', 'new_m_s5_c_re': 'new_m', 'new_m_s5_c_im': 'new_m', 'new_m_s5_d': 'new_m', 'new_m_s5_w_glu': 'new_m', 'new_m_xa_norm_g': 'new_m', 'new_m_mem_norm_g': 'new_m', 'new_m_xa_wq': 'new_m', 'new_m_xa_wkv': 'new_m', 'new_m_xa_wo': 'new_m', 'new_m_ffn_norm_g': 'new_m', 'new_m_ffn_w_up': 'new_m', 'new_m_ffn_conv_w': 'new_m', 'new_m_ffn_conv_b': 'new_m', 'new_m_ffn_w_down': 'new_m', 'new_m_final_norm_g': 'new_m', 'new_v_mix_norm_g': 'new_v', 'new_v_pool_w': 'new_v', 'new_v_pool_scale': 'new_v', 'new_v_sb_w_qkv': 'new_v', 'new_v_sb_w_o': 'new_v', 'new_v_s5_a_re': 'new_v', 'new_v_s5_a_im': 'new_v', 'new_v_s5_log_dt': 'new_v', 'new_v_s5_b_re': 'new_v', 'new_v_s5_b_im': 'new_v', 'new_v_s5_c_re': 'new_v', 'new_v_s5_c_im': 'new_v', 'new_v_s5_d': 'new_v', 'new_v_s5_w_glu': 'new_v', 'new_v_xa_norm_g': 'new_v', 'new_v_mem_norm_g': 'new_v', 'new_v_xa_wq': 'new_v', 'new_v_xa_wkv': 'new_v', 'new_v_xa_wo': 'new_v', 'new_v_ffn_norm_g': 'new_v', 'new_v_ffn_w_up': 'new_v', 'new_v_ffn_conv_w': 'new_v', 'new_v_ffn_conv_b': 'new_v', 'new_v_ffn_w_down': 'new_v', 'new_v_final_norm_g': 'new_v'}


def _forward(args):
    return _fwd_reference(*[args[k] for k in FWD_PARAMS])


def _output_shape():
    def fwd():
        inp = _fwd_setup_inputs(0)
        return _fwd_reference(*[inp[k] for k in FWD_PARAMS])
    out = _jax.eval_shape(fwd)
    return out.shape, out.dtype

N_MICROBATCH = 1
ADAM_LR = 0.001
ADAM_B1 = 0.9
ADAM_B2 = 0.999
ADAM_EPS = 1e-08
ADAM_WD = 0.01
ADAM_STEP = 10
PER_EXAMPLE_BATCH_AXIS = {'x': 0, 'mem': 0, 'loss_target': 0}
SHARED_INPUTS = []
_WEIGHT_DTYPES = {'mix_norm_g': _jnp.float32, 'pool_w': _jnp.float32, 'pool_scale': _jnp.float32, 'sb_w_qkv': _jnp.float32, 'sb_w_o': _jnp.float32, 's5_a_re': _jnp.float32, 's5_a_im': _jnp.float32, 's5_log_dt': _jnp.float32, 's5_b_re': _jnp.float32, 's5_b_im': _jnp.float32, 's5_c_re': _jnp.float32, 's5_c_im': _jnp.float32, 's5_d': _jnp.float32, 's5_w_glu': _jnp.float32, 'xa_norm_g': _jnp.float32, 'mem_norm_g': _jnp.float32, 'xa_wq': _jnp.float32, 'xa_wkv': _jnp.float32, 'xa_wo': _jnp.float32, 'ffn_norm_g': _jnp.float32, 'ffn_w_up': _jnp.float32, 'ffn_conv_w': _jnp.float32, 'ffn_conv_b': _jnp.float32, 'ffn_w_down': _jnp.float32, 'final_norm_g': _jnp.float32}
MOMENT_SCALE = {'mix_norm_g': 1.041783e-01, 'pool_w': 1.249712e-01, 'pool_scale': 2.107837e-01, 'sb_w_qkv': 5.734856e-02, 'sb_w_o': 8.215616e-02, 's5_a_re': 2.074931e-03, 's5_a_im': 2.025999e-03, 's5_log_dt': 1.252016e+00, 's5_b_re': 1.357880e-03, 's5_b_im': 1.348365e-03, 's5_c_re': 2.613151e-03, 's5_c_im': 2.614845e-03, 's5_d': 4.136638e-02, 's5_w_glu': 2.896047e-02, 'xa_norm_g': 1.367651e-02, 'mem_norm_g': 2.108472e-02, 'xa_wq': 1.368818e-02, 'xa_wkv': 1.379375e-02, 'xa_wo': 1.392681e-02, 'ffn_norm_g': 1.019417e-01, 'ffn_w_up': 4.273928e-02, 'ffn_conv_w': 4.270635e-02, 'ffn_conv_b': 4.143747e-02, 'ffn_w_down': 6.984250e-02, 'final_norm_g': 3.211818e+01}


def _to_microbatches(a, axis):
    t = _jnp.moveaxis(a, axis, 0)
    t = t.reshape((N_MICROBATCH, t.shape[0] // N_MICROBATCH) + t.shape[1:])
    return _jnp.moveaxis(t, 1, axis + 1)


def setup_inputs(seed: int = 0) -> dict:
    inp = _fwd_setup_inputs(seed)
    key = _jax.random.fold_in(_jax.random.key(seed), 7919)
    shape, _ = _output_shape()
    out = dict(inp)
    out["loss_target"] = _jax.random.normal(_jax.random.fold_in(key, 0), shape, _jnp.float32)
    for i, name in enumerate(TWIN_WEIGHTS):
        w = inp[name].astype(_jnp.float32)
        if MOMENT_SCALE is None:
            s = _jnp.sqrt(_jnp.mean(_jnp.square(w)) + 1e-30)
        else:
            s = MOMENT_SCALE[name]
        km, kv = _jax.random.split(_jax.random.fold_in(key, i + 1))
        out[name] = w
        out["m_" + name] = s * _jax.random.normal(km, w.shape, _jnp.float32)
        out["v_" + name] = (s * s) * _jax.random.uniform(kv, w.shape, _jnp.float32, 0.5, 1.5)
    if N_MICROBATCH > 1:
        for name, axis in PER_EXAMPLE_BATCH_AXIS.items():
            out[name] = _to_microbatches(out[name], axis)
    return {'x': out['x'], 'mem': out['mem'], 'mix_norm_g': out['mix_norm_g'], 'pool_w': out['pool_w'], 'pool_scale': out['pool_scale'], 'sb_w_qkv': out['sb_w_qkv'], 'sb_w_o': out['sb_w_o'], 's5_a_re': out['s5_a_re'], 's5_a_im': out['s5_a_im'], 's5_log_dt': out['s5_log_dt'], 's5_b_re': out['s5_b_re'], 's5_b_im': out['s5_b_im'], 's5_c_re': out['s5_c_re'], 's5_c_im': out['s5_c_im'], 's5_d': out['s5_d'], 's5_w_glu': out['s5_w_glu'], 'xa_norm_g': out['xa_norm_g'], 'mem_norm_g': out['mem_norm_g'], 'xa_wq': out['xa_wq'], 'xa_wkv': out['xa_wkv'], 'xa_wo': out['xa_wo'], 'ffn_norm_g': out['ffn_norm_g'], 'ffn_w_up': out['ffn_w_up'], 'ffn_conv_w': out['ffn_conv_w'], 'ffn_conv_b': out['ffn_conv_b'], 'ffn_w_down': out['ffn_w_down'], 'final_norm_g': out['final_norm_g'], 'loss_target': out['loss_target'], 'm_mix_norm_g': out['m_mix_norm_g'], 'm_pool_w': out['m_pool_w'], 'm_pool_scale': out['m_pool_scale'], 'm_sb_w_qkv': out['m_sb_w_qkv'], 'm_sb_w_o': out['m_sb_w_o'], 'm_s5_a_re': out['m_s5_a_re'], 'm_s5_a_im': out['m_s5_a_im'], 'm_s5_log_dt': out['m_s5_log_dt'], 'm_s5_b_re': out['m_s5_b_re'], 'm_s5_b_im': out['m_s5_b_im'], 'm_s5_c_re': out['m_s5_c_re'], 'm_s5_c_im': out['m_s5_c_im'], 'm_s5_d': out['m_s5_d'], 'm_s5_w_glu': out['m_s5_w_glu'], 'm_xa_norm_g': out['m_xa_norm_g'], 'm_mem_norm_g': out['m_mem_norm_g'], 'm_xa_wq': out['m_xa_wq'], 'm_xa_wkv': out['m_xa_wkv'], 'm_xa_wo': out['m_xa_wo'], 'm_ffn_norm_g': out['m_ffn_norm_g'], 'm_ffn_w_up': out['m_ffn_w_up'], 'm_ffn_conv_w': out['m_ffn_conv_w'], 'm_ffn_conv_b': out['m_ffn_conv_b'], 'm_ffn_w_down': out['m_ffn_w_down'], 'm_final_norm_g': out['m_final_norm_g'], 'v_mix_norm_g': out['v_mix_norm_g'], 'v_pool_w': out['v_pool_w'], 'v_pool_scale': out['v_pool_scale'], 'v_sb_w_qkv': out['v_sb_w_qkv'], 'v_sb_w_o': out['v_sb_w_o'], 'v_s5_a_re': out['v_s5_a_re'], 'v_s5_a_im': out['v_s5_a_im'], 'v_s5_log_dt': out['v_s5_log_dt'], 'v_s5_b_re': out['v_s5_b_re'], 'v_s5_b_im': out['v_s5_b_im'], 'v_s5_c_re': out['v_s5_c_re'], 'v_s5_c_im': out['v_s5_c_im'], 'v_s5_d': out['v_s5_d'], 'v_s5_w_glu': out['v_s5_w_glu'], 'v_xa_norm_g': out['v_xa_norm_g'], 'v_mem_norm_g': out['v_mem_norm_g'], 'v_xa_wq': out['v_xa_wq'], 'v_xa_wkv': out['v_xa_wkv'], 'v_xa_wo': out['v_xa_wo'], 'v_ffn_norm_g': out['v_ffn_norm_g'], 'v_ffn_w_up': out['v_ffn_w_up'], 'v_ffn_conv_w': out['v_ffn_conv_w'], 'v_ffn_conv_b': out['v_ffn_conv_b'], 'v_ffn_w_down': out['v_ffn_w_down'], 'v_final_norm_g': out['v_final_norm_g']}


def _loss(weights, diff, rest, loss_target):
    with _jax.named_scope("forward"):
        args = {**rest, TWIN_DIFF_INPUT: diff, **{k: w.astype(_WEIGHT_DTYPES[k]) for k, w in weights.items()}}
        y = _forward(args)
    with _jax.named_scope("loss_head"):
        err = _jnp.square(y.astype(_jnp.float32) - loss_target)
        return 0.5 * _jnp.sum(_jnp.mean(err, axis=-1)) if err.ndim else 0.5 * err


def _adamw(w, g, m, v):
    m = ADAM_B1 * m + (1.0 - ADAM_B1) * g
    v = ADAM_B2 * v + (1.0 - ADAM_B2) * _jnp.square(g)
    m_hat = m / (1.0 - ADAM_B1 ** ADAM_STEP)
    v_hat = v / (1.0 - ADAM_B2 ** ADAM_STEP)
    delta = -ADAM_LR * (m_hat / (_jnp.sqrt(v_hat) + ADAM_EPS) + ADAM_WD * w)
    return delta, m, v


def reference(x, mem, mix_norm_g, pool_w, pool_scale, sb_w_qkv, sb_w_o, s5_a_re, s5_a_im, s5_log_dt, s5_b_re, s5_b_im, s5_c_re, s5_c_im, s5_d, s5_w_glu, xa_norm_g, mem_norm_g, xa_wq, xa_wkv, xa_wo, ffn_norm_g, ffn_w_up, ffn_conv_w, ffn_conv_b, ffn_w_down, final_norm_g, loss_target, m_mix_norm_g, m_pool_w, m_pool_scale, m_sb_w_qkv, m_sb_w_o, m_s5_a_re, m_s5_a_im, m_s5_log_dt, m_s5_b_re, m_s5_b_im, m_s5_c_re, m_s5_c_im, m_s5_d, m_s5_w_glu, m_xa_norm_g, m_mem_norm_g, m_xa_wq, m_xa_wkv, m_xa_wo, m_ffn_norm_g, m_ffn_w_up, m_ffn_conv_w, m_ffn_conv_b, m_ffn_w_down, m_final_norm_g, v_mix_norm_g, v_pool_w, v_pool_scale, v_sb_w_qkv, v_sb_w_o, v_s5_a_re, v_s5_a_im, v_s5_log_dt, v_s5_b_re, v_s5_b_im, v_s5_c_re, v_s5_c_im, v_s5_d, v_s5_w_glu, v_xa_norm_g, v_mem_norm_g, v_xa_wq, v_xa_wkv, v_xa_wo, v_ffn_norm_g, v_ffn_w_up, v_ffn_conv_w, v_ffn_conv_b, v_ffn_w_down, v_final_norm_g):
    given = dict(x=x, mem=mem, mix_norm_g=mix_norm_g, pool_w=pool_w, pool_scale=pool_scale, sb_w_qkv=sb_w_qkv, sb_w_o=sb_w_o, s5_a_re=s5_a_re, s5_a_im=s5_a_im, s5_log_dt=s5_log_dt, s5_b_re=s5_b_re, s5_b_im=s5_b_im, s5_c_re=s5_c_re, s5_c_im=s5_c_im, s5_d=s5_d, s5_w_glu=s5_w_glu, xa_norm_g=xa_norm_g, mem_norm_g=mem_norm_g, xa_wq=xa_wq, xa_wkv=xa_wkv, xa_wo=xa_wo, ffn_norm_g=ffn_norm_g, ffn_w_up=ffn_w_up, ffn_conv_w=ffn_conv_w, ffn_conv_b=ffn_conv_b, ffn_w_down=ffn_w_down, final_norm_g=final_norm_g, loss_target=loss_target, m_mix_norm_g=m_mix_norm_g, m_pool_w=m_pool_w, m_pool_scale=m_pool_scale, m_sb_w_qkv=m_sb_w_qkv, m_sb_w_o=m_sb_w_o, m_s5_a_re=m_s5_a_re, m_s5_a_im=m_s5_a_im, m_s5_log_dt=m_s5_log_dt, m_s5_b_re=m_s5_b_re, m_s5_b_im=m_s5_b_im, m_s5_c_re=m_s5_c_re, m_s5_c_im=m_s5_c_im, m_s5_d=m_s5_d, m_s5_w_glu=m_s5_w_glu, m_xa_norm_g=m_xa_norm_g, m_mem_norm_g=m_mem_norm_g, m_xa_wq=m_xa_wq, m_xa_wkv=m_xa_wkv, m_xa_wo=m_xa_wo, m_ffn_norm_g=m_ffn_norm_g, m_ffn_w_up=m_ffn_w_up, m_ffn_conv_w=m_ffn_conv_w, m_ffn_conv_b=m_ffn_conv_b, m_ffn_w_down=m_ffn_w_down, m_final_norm_g=m_final_norm_g, v_mix_norm_g=v_mix_norm_g, v_pool_w=v_pool_w, v_pool_scale=v_pool_scale, v_sb_w_qkv=v_sb_w_qkv, v_sb_w_o=v_sb_w_o, v_s5_a_re=v_s5_a_re, v_s5_a_im=v_s5_a_im, v_s5_log_dt=v_s5_log_dt, v_s5_b_re=v_s5_b_re, v_s5_b_im=v_s5_b_im, v_s5_c_re=v_s5_c_re, v_s5_c_im=v_s5_c_im, v_s5_d=v_s5_d, v_s5_w_glu=v_s5_w_glu, v_xa_norm_g=v_xa_norm_g, v_mem_norm_g=v_mem_norm_g, v_xa_wq=v_xa_wq, v_xa_wkv=v_xa_wkv, v_xa_wo=v_xa_wo, v_ffn_norm_g=v_ffn_norm_g, v_ffn_w_up=v_ffn_w_up, v_ffn_conv_w=v_ffn_conv_w, v_ffn_conv_b=v_ffn_conv_b, v_ffn_w_down=v_ffn_w_down, v_final_norm_g=v_final_norm_g)
    weights = {n: given[n] for n in TWIN_WEIGHTS}
    shared = {n: given[n] for n in SHARED_INPUTS}
    per_example = {n: given[n] for n in ['x', 'mem']}
    grad_fn = _jax.value_and_grad(_loss, argnums=(0, 1))

    def one_microbatch(ex, loss_target):
        ex = dict(ex)
        diff = ex.pop(TWIN_DIFF_INPUT)
        return grad_fn(weights, diff, {**shared, **ex}, loss_target)

    if N_MICROBATCH == 1:
        loss, (grad_w, grad_x) = one_microbatch(per_example, given["loss_target"])
    else:
        def body(carry, xs):
            loss_sum, grad_sum = carry
            l_k, (gw_k, gx_k) = one_microbatch(xs[0], xs[1])
            with _jax.named_scope("update"):
                return (loss_sum + l_k, _jax.tree.map(_jnp.add, grad_sum, gw_k)), gx_k

        init = (_jnp.zeros((), _jnp.float32), _jax.tree.map(_jnp.zeros_like, weights))
        (loss, grad_w), grad_x = _jax.lax.scan(body, init, (per_example, given["loss_target"]))
    with _jax.named_scope("update"):
        delta_w, new_m, new_v = {}, {}, {}
        for n in TWIN_WEIGHTS:
            delta_w[n], new_m[n], new_v[n] = _adamw(weights[n], grad_w[n], given["m_" + n], given["v_" + n])
    return (loss, grad_x, *[grad_w[n] for n in TWIN_WEIGHTS], *[delta_w[n] for n in TWIN_WEIGHTS],
            *[new_m[n] for n in TWIN_WEIGHTS], *[new_v[n] for n in TWIN_WEIGHTS])
```

```python
import functools
import math

import jax
import jax.numpy as jnp
from jax import lax
from jax.experimental import pallas as pl
from jax.experimental.pallas import tpu as pltpu

F32 = jnp.float32
BF16 = jnp.bfloat16
MESH = pl.DeviceIdType.MESH

EPS = 1e-6
POOL_WINDOWS = (2, 4, 8, 16)
POOL_HALO = 128
POOL_TILE = 256
SB_HEAD_DIM = 64
SB_BLOCK = 128
S5_GROUP = 16
S5_STATE = 64
S5_GROUPS_PER_BLOCK = 8
XA_HEADS = 4
CONV_WIDTH = 3
ADAM_LR, ADAM_B1, ADAM_B2, ADAM_EPS, ADAM_WD, ADAM_STEP = 0.001, 0.9, 0.999, 1e-08, 0.01, 10

V7X_VMEM_BYTES = 64 * 1024 * 1024
LANE = 128
SUBLANE = 8


def _vmem_limit(block_bytes):
    want = 2 * block_bytes + 16 * 1024 * 1024
    return int(min(V7X_VMEM_BYTES - 6 * 1024 * 1024, max(32 * 1024 * 1024, want)))


def _params(sem, block_bytes):
    return pltpu.CompilerParams(dimension_semantics=sem, vmem_limit_bytes=_vmem_limit(block_bytes))


def _tile(n, cap, align=LANE):
    if n <= cap:
        return n
    t = (cap // align) * align
    while t >= align:
        if n % t == 0:
            return t
        t -= align
    return n


def _nbytes(shape, dtype):
    return math.prod(shape) * jnp.dtype(dtype).itemsize


def mm(a, b, *, ta=False, tb=False, out_dtype=F32, res=None, b_col0=None, name, tm=1024, tn=1024, tk=512):
    if ta:
        K, M = a.shape
    else:
        M, K = a.shape
    if tb:
        N, Kb = b.shape
    else:
        Kb, N = b.shape
    if b_col0 is None:
        assert K == Kb, (a.shape, b.shape, ta, tb)
    tm, tn, tk = _tile(M, tm), _tile(N, tn), _tile(K, tk)
    nk = K // tk
    koff = 0
    if b_col0 is not None:
        assert tb and b_col0 % tk == 0 and b_col0 + K <= Kb
        koff = b_col0 // tk
    dims = (((0,) if ta else (1,), (1,) if tb else (0,)), ((), ()))

    def body(*refs):
        if res is None:
            a_ref, b_ref, o_ref, acc = refs
            r_ref = None
        else:
            a_ref, b_ref, r_ref, o_ref, acc = refs
        k = pl.program_id(2)

        @pl.when(k == 0)
        def _():
            acc[...] = jnp.zeros_like(acc)

        acc[...] += lax.dot_general(a_ref[...].astype(BF16), b_ref[...].astype(BF16), dims,
                                    preferred_element_type=F32)

        @pl.when(k == nk - 1)
        def _():
            r = acc[...]
            if r_ref is not None:
                r = r + r_ref[...].astype(F32)
            o_ref[...] = r.astype(out_dtype)

    a_spec = pl.BlockSpec((tk, tm), lambda i, j, k: (k, i)) if ta else pl.BlockSpec((tm, tk), lambda i, j, k: (i, k))
    b_spec = pl.BlockSpec((tn, tk), lambda i, j, k: (j, k + koff)) if tb else pl.BlockSpec((tk, tn), lambda i, j, k: (k, j))
    in_specs = [a_spec, b_spec]
    args = [a, b]
    blk = _nbytes((tm, tk), a.dtype) + _nbytes((tk, tn), b.dtype) + _nbytes((tm, tn), out_dtype)
    if res is not None:
        in_specs.append(pl.BlockSpec((tm, tn), lambda i, j, k: (i, j)))
        args.append(res)
        blk += _nbytes((tm, tn), res.dtype)
    return pl.pallas_call(
        body, grid=(M // tm, N // tn, nk), in_specs=in_specs,
        out_specs=pl.BlockSpec((tm, tn), lambda i, j, k: (i, j)),
        out_shape=jax.ShapeDtypeStruct((M, N), out_dtype),
        scratch_shapes=[pltpu.VMEM((tm, tn), F32)], name=name,
        compiler_params=_params(("parallel", "parallel", "arbitrary"), blk + _nbytes((tm, tn), F32)),
    )(*args)


def bdmm(a, w, *, out_dtype, name, tm=512):
    M = a.shape[0]
    nb, ka, kn = w.shape
    tm = _tile(M, tm)

    def body(a_ref, w_ref, o_ref):
        o_ref[...] = jnp.dot(a_ref[...].astype(BF16), w_ref[...].astype(BF16),
                             preferred_element_type=F32).astype(out_dtype)

    blk = _nbytes((tm, ka), a.dtype) + _nbytes((ka, kn), w.dtype) + _nbytes((tm, kn), out_dtype)
    return pl.pallas_call(
        body, grid=(M // tm, nb),
        in_specs=[pl.BlockSpec((tm, ka), lambda i, b: (i, b)), pl.BlockSpec((None, ka, kn), lambda i, b: (b, 0, 0))],
        out_specs=pl.BlockSpec((tm, kn), lambda i, b: (i, b)),
        out_shape=jax.ShapeDtypeStruct((M, nb * kn), out_dtype), name=name,
        compiler_params=_params(("parallel", "parallel"), blk),
    )(a, w)


def bdmm_tn(a, d, *, ka, kd, name, tm=512):
    M = a.shape[0]
    nb = a.shape[1] // ka
    assert d.shape[1] == nb * kd
    tm = _tile(M, tm)

    def body(a_ref, d_ref, o_ref):
        i = pl.program_id(1)
        v = lax.dot_general(a_ref[...].astype(BF16), d_ref[...].astype(BF16), (((0,), (0,)), ((), ())),
                            preferred_element_type=F32)

        @pl.when(i == 0)
        def _():
            o_ref[...] = v

        @pl.when(i > 0)
        def _():
            o_ref[...] += v

    blk = _nbytes((tm, ka), a.dtype) + _nbytes((tm, kd), d.dtype) + _nbytes((ka, kd), F32)
    return pl.pallas_call(
        body, grid=(nb, M // tm),
        in_specs=[pl.BlockSpec((tm, ka), lambda b, i: (i, b)), pl.BlockSpec((tm, kd), lambda b, i: (i, b))],
        out_specs=pl.BlockSpec((None, ka, kd), lambda b, i: (b, 0, 0)),
        out_shape=jax.ShapeDtypeStruct((nb, ka, kd), F32), name=name,
        compiler_params=_params(("parallel", "arbitrary"), blk),
    )(a, d)


def ew(fn, ins, outs, *, rows, tr, name):
    n = rows // tr
    assert n * tr == rows
    in_specs, args, blk = [], [], 0
    for a, kind in ins:
        if kind == "tile":
            assert a.shape[0] == rows, (name, a.shape, rows)
            in_specs.append(pl.BlockSpec((tr, a.shape[1]), lambda i: (i, 0)))
            blk += _nbytes((tr, a.shape[1]), a.dtype)
        else:
            in_specs.append(pl.BlockSpec(a.shape, lambda i, nd=a.ndim: (0,) * nd))
            blk += _nbytes(a.shape, a.dtype)
        args.append(a)
    out_shape, out_specs = [], []
    for c, dt, kind in outs:
        if kind == "tile":
            out_shape.append(jax.ShapeDtypeStruct((rows, c), dt))
            out_specs.append(pl.BlockSpec((tr, c), lambda i: (i, 0)))
            blk += _nbytes((tr, c), dt)
        else:
            out_shape.append(jax.ShapeDtypeStruct((1, c), dt))
            out_specs.append(pl.BlockSpec((1, c), lambda i: (0, 0)))
    nin = len(ins)

    def body(*refs):
        i = pl.program_id(0)
        vals = fn(i, *[r[...] for r in refs[:nin]])
        for (c, dt, kind), o, v in zip(outs, refs[nin:], vals):
            if kind == "tile":
                o[...] = v.astype(dt)
            else:
                @pl.when(i == 0)
                def _():
                    o[...] = v.astype(dt)

                @pl.when(i > 0)
                def _():
                    o[...] += v.astype(dt)

    has_acc = any(k == "acc" for _, _, k in outs)
    return pl.pallas_call(
        body, grid=(n,), in_specs=in_specs, out_specs=out_specs, out_shape=out_shape, name=name,
        compiler_params=_params(("arbitrary" if has_acc else "parallel",), 3 * blk),
    )(*args)


def _colsum(x):
    return jnp.sum(x, axis=0, keepdims=True)


def rms_fwd(x, g, *, out_dtype, name, tr=256):
    def fn(i, xv, gv):
        r = lax.rsqrt(jnp.mean(xv * xv, axis=-1, keepdims=True) + EPS)
        return [xv * r * gv]

    return ew(fn, [(x, "tile"), (g, "full")], [(x.shape[1], out_dtype, "tile")], rows=x.shape[0], tr=tr, name=name)[0]


def rms_bwd(x, g, dy, dres, *, name, tr=256):
    def fn(i, xv, gv, dyv, drv):
        dyv = dyv.astype(F32)
        r = lax.rsqrt(jnp.mean(xv * xv, axis=-1, keepdims=True) + EPS)
        xh = xv * r
        gy = dyv * gv
        dx = r * (gy - xh * jnp.mean(gy * xh, axis=-1, keepdims=True))
        return [dx + drv, _colsum(dyv * xh)]

    D = x.shape[1]
    return ew(fn, [(x, "tile"), (g, "full"), (dy, "tile"), (dres, "tile")], [(D, F32, "tile"), (D, F32, "acc")],
              rows=x.shape[0], tr=tr, name=name)


def rms_bwd_g(x, dy, *, name, tr=256):
    def fn(i, xv, dyv):
        r = lax.rsqrt(jnp.mean(xv * xv, axis=-1, keepdims=True) + EPS)
        return [_colsum(dyv.astype(F32) * xv * r)]

    return ew(fn, [(x, "tile"), (dy, "tile")], [(x.shape[1], F32, "acc")], rows=x.shape[0],
              tr=_tile(x.shape[0], tr, SUBLANE), name=name)[0]


def loss_head(h, g, target, *, name, tr=256):
    D = h.shape[1]

    def fn(i, xv, gv, tv):
        r = lax.rsqrt(jnp.mean(xv * xv, axis=-1, keepdims=True) + EPS)
        xh = xv * r
        err = xh * gv - tv
        dy = err * (1.0 / D)
        gy = dy * gv
        dx = r * (gy - xh * jnp.mean(gy * xh, axis=-1, keepdims=True))
        part = _colsum(err * err) * (0.5 / D)
        return [dx, _colsum(dy * xh), jnp.sum(part, axis=1, keepdims=True)]

    return ew(fn, [(h, "tile"), (g, "full"), (target, "tile")], [(D, F32, "tile"), (D, F32, "acc"), (1, F32, "acc")],
              rows=h.shape[0], tr=tr, name=name)


def _as2d(a):
    if a.ndim >= 2 and a.shape[-1] >= LANE:
        return a.reshape(-1, a.shape[-1])
    if a.size % (8 * LANE) == 0:
        return a.reshape(-1, 8 * LANE)
    return a.reshape(1, -1)


def adamw(w, g, m, v, *, name):
    shape = w.shape
    w2, g2, m2, v2 = (_as2d(t) for t in (w, g.astype(F32).reshape(shape), m, v))
    R, C = w2.shape
    tr = R
    if R * C * 4 > (1 << 20):
        tr = _tile(R, max(SUBLANE, (1 << 20) // (C * 4) // SUBLANE * SUBLANE), SUBLANE)
    c1 = 1.0 / (1.0 - ADAM_B1 ** ADAM_STEP)
    c2 = 1.0 / (1.0 - ADAM_B2 ** ADAM_STEP)

    def fn(i, wv, gv, mv, vv):
        mn = ADAM_B1 * mv + (1.0 - ADAM_B1) * gv
        vn = ADAM_B2 * vv + (1.0 - ADAM_B2) * (gv * gv)
        delta = -ADAM_LR * ((mn * c1) / (jnp.sqrt(vn * c2) + ADAM_EPS) + ADAM_WD * wv)
        return [delta, mn, vn]

    d, mn, vn = ew(fn, [(w2, "tile"), (g2, "tile"), (m2, "tile"), (v2, "tile")], [(C, F32, "tile")] * 3,
                   rows=R, tr=tr, name=name)
    return d.reshape(shape), mn.reshape(shape), vn.reshape(shape)


def _split_bf16(x):
    hi = x.astype(BF16)
    return hi, (x - hi.astype(F32)).astype(BF16)


def _dot2(band, x):
    hi, lo = _split_bf16(x)
    return jnp.dot(band, hi, preferred_element_type=F32) + jnp.dot(band, lo, preferred_element_type=F32)


def _pool_fwd_window(xm, xh, r0, win):
    T = xm.shape[0]
    t = r0 + lax.broadcasted_iota(jnp.int32, (T, 1), 0)
    s_main = r0 + lax.broadcasted_iota(jnp.int32, (1, T), 1)
    s_halo = r0 - POOL_HALO + lax.broadcasted_iota(jnp.int32, (1, POOL_HALO), 1)
    band_m = ((s_main <= t) & (s_main > t - win)).astype(BF16)
    band_h = ((s_halo > t - win) & (s_halo >= 0)).astype(BF16)
    ws = _dot2(band_m, xm) + _dot2(band_h, xh)
    cnt = jnp.minimum(t + 1, win).astype(F32)
    return ws / cnt - xm


def _pool_bwd_window(dm, dh, r0, win, S):
    T = dm.shape[0]
    s = r0 + lax.broadcasted_iota(jnp.int32, (T, 1), 0)
    t_main = r0 + lax.broadcasted_iota(jnp.int32, (1, T), 1)
    t_halo = r0 + T + lax.broadcasted_iota(jnp.int32, (1, POOL_HALO), 1)
    band_m = ((t_main >= s) & (t_main < s + win)).astype(BF16)
    band_h = ((t_halo < s + win) & (t_halo < S)).astype(BF16)
    tm_col = r0 + lax.broadcasted_iota(jnp.int32, (T, 1), 0)
    th_col = r0 + T + lax.broadcasted_iota(jnp.int32, (POOL_HALO, 1), 0)
    dmc = dm / jnp.minimum(tm_col + 1, win).astype(F32)
    dhc = dh / jnp.minimum(th_col + 1, win).astype(F32)
    return _dot2(band_m, dmc) + _dot2(band_h, dhc) - dm


def _pool_specs(T, Cg, order):
    per = T // POOL_HALO
    if order == "ig":
        return (pl.BlockSpec((T, Cg), lambda i, g: (i, g)),
                pl.BlockSpec((POOL_HALO, Cg), lambda i, g: (jnp.maximum(i * per - 1, 0), g)))
    return (pl.BlockSpec((T, Cg), lambda g, i: (i, g)),
            pl.BlockSpec((POOL_HALO, Cg), lambda g, i: (jnp.maximum(i * per - 1, 0), g)))


def pool_fwd(hn, h, w, scale, *, name):
    S, D = hn.shape
    G, Cg, _ = w.shape
    T = _tile(S, POOL_TILE)

    def body(xm_ref, xh_ref, h_ref, w_ref, sc_ref, o_ref):
        i, g = pl.program_id(0), pl.program_id(1)
        win = jnp.left_shift(2, g)
        p = _pool_fwd_window(xm_ref[...], xh_ref[...], i * T, win)
        y = jnp.dot(p.astype(BF16), w_ref[...], preferred_element_type=F32)
        o_ref[...] = h_ref[...] + y * sc_ref[...]

    main, halo = _pool_specs(T, Cg, "ig")
    return pl.pallas_call(
        body, grid=(S // T, G),
        in_specs=[main, halo, main, pl.BlockSpec((None, Cg, Cg), lambda i, g: (g, 0, 0)),
                  pl.BlockSpec((1, Cg), lambda i, g: (0, g))],
        out_specs=main, out_shape=jax.ShapeDtypeStruct((S, D), F32), name=name,
        compiler_params=_params(("parallel", "parallel"), 4 * T * Cg * 4),
    )(hn, hn, h, w, scale)


def pool_bwd_w(hn, dt, w, scale, *, name):
    S, D = hn.shape
    G, Cg, _ = w.shape
    T = _tile(S, POOL_TILE)

    def body(xm_ref, xh_ref, dt_ref, w_ref, sc_ref, dp_ref, dw_ref, ds_ref):
        g, i = pl.program_id(0), pl.program_id(1)
        win = jnp.left_shift(2, g)
        p = _pool_fwd_window(xm_ref[...], xh_ref[...], i * T, win).astype(BF16)
        dtv = dt_ref[...]
        ypre = jnp.dot(p, w_ref[...], preferred_element_type=F32)
        dy = (dtv * sc_ref[...]).astype(BF16)
        dp_ref[...] = lax.dot_general(dy, w_ref[...], (((1,), (1,)), ((), ())), preferred_element_type=F32)
        dwv = lax.dot_general(p, dy, (((0,), (0,)), ((), ())), preferred_element_type=F32)
        dsv = _colsum(dtv * ypre)

        @pl.when(i == 0)
        def _():
            dw_ref[...] = dwv
            ds_ref[...] = dsv

        @pl.when(i > 0)
        def _():
            dw_ref[...] += dwv
            ds_ref[...] += dsv

    main, halo = _pool_specs(T, Cg, "gi")
    return pl.pallas_call(
        body, grid=(G, S // T),
        in_specs=[main, halo, main, pl.BlockSpec((None, Cg, Cg), lambda g, i: (g, 0, 0)),
                  pl.BlockSpec((1, Cg), lambda g, i: (0, g))],
        out_specs=[main, pl.BlockSpec((None, Cg, Cg), lambda g, i: (g, 0, 0)), pl.BlockSpec((1, Cg), lambda g, i: (0, g))],
        out_shape=[jax.ShapeDtypeStruct((S, D), F32), jax.ShapeDtypeStruct((G, Cg, Cg), F32),
                   jax.ShapeDtypeStruct((1, D), F32)], name=name,
        compiler_params=_params(("parallel", "arbitrary"), 4 * T * Cg * 4),
    )(hn, hn, dt, w, scale)


def pool_bwd_x(dp, G, *, name):
    S, D = dp.shape
    Cg = D // G
    T = _tile(S, POOL_TILE)
    per = T // POOL_HALO
    last = S // POOL_HALO - 1

    def body(dm_ref, dh_ref, o_ref):
        i, g = pl.program_id(0), pl.program_id(1)
        o_ref[...] = _pool_bwd_window(dm_ref[...], dh_ref[...], i * T, jnp.left_shift(2, g), S)

    main = pl.BlockSpec((T, Cg), lambda i, g: (i, g))
    return pl.pallas_call(
        body, grid=(S // T, G),
        in_specs=[main, pl.BlockSpec((POOL_HALO, Cg), lambda i, g: (jnp.minimum((i + 1) * per, last), g))],
        out_specs=main, out_shape=jax.ShapeDtypeStruct((S, D), F32), name=name,
        compiler_params=_params(("parallel", "parallel"), 3 * T * Cg * 4),
    )(dp, dp)


def _sb_tile(z, mask, upper, carry):
    e = jnp.exp(-jnp.abs(z))
    sp = jnp.log(1.0 + e)
    ls = jnp.minimum(z, 0.0) - sp
    lsn = jnp.where(mask, jnp.minimum(-z, 0.0) - sp, 0.0)
    between = _dot2r(lsn, upper) + carry
    a = jnp.where(mask, jnp.exp(ls + between), 0.0)
    return a, lsn, e


def _dot2r(x, band):
    hi, lo = _split_bf16(x)
    return jnp.dot(hi, band, preferred_element_type=F32) + jnp.dot(lo, band, preferred_element_type=F32)


def _dot3r(x, band):
    hi = x.astype(BF16)
    r1 = x - hi.astype(F32)
    mid = r1.astype(BF16)
    lo = (r1 - mid.astype(F32)).astype(BF16)
    return (jnp.dot(hi, band, preferred_element_type=F32) + jnp.dot(mid, band, preferred_element_type=F32)
            + jnp.dot(lo, band, preferred_element_type=F32))


def _head_masks(n_lanes):
    lane = lax.broadcasted_iota(jnp.int32, (1, n_lanes), 1)
    return [((lane >= h * SB_HEAD_DIM) & (lane < (h + 1) * SB_HEAD_DIM)) for h in range(n_lanes // SB_HEAD_DIM)]


_NT = (((1,), (1,)), ((), ()))
_TN = (((0,), (0,)), ((), ()))


def sb_fwd(qkv, *, name):
    S, D3 = qkv.shape
    D = D3 // 3
    B = SB_BLOCK
    npair = D // LANE
    scale = SB_HEAD_DIM ** -0.5

    def body(q_ref, k_ref, v_ref, o_ref):
        i = pl.program_id(1)
        masks = _head_masks(LANE)
        q = q_ref[...]
        qh = [jnp.where(m, q, jnp.zeros_like(q)) for m in masks]
        row = lax.broadcasted_iota(jnp.int32, (B, B), 0)
        col = lax.broadcasted_iota(jnp.int32, (B, B), 1)
        upper = (row > col).astype(BF16)

        def step(n, carry):
            o, rs = carry
            j = i - n
            off = pl.multiple_of(j * B, B)
            kb = k_ref[pl.ds(off, B), :]
            vb = v_ref[pl.ds(off, B), :]
            mask = (j * B + col) < (i * B + row)
            new_rs = []
            for hd, m in enumerate(masks):
                z = lax.dot_general(qh[hd], kb, _NT, preferred_element_type=F32) * scale
                a, lsn, _ = _sb_tile(z, mask, upper, rs[hd])
                o = o + jnp.dot(a.astype(BF16), jnp.where(m, vb, jnp.zeros_like(vb)), preferred_element_type=F32)
                new_rs.append(rs[hd] + jnp.sum(lsn, axis=1, keepdims=True))
            return o, tuple(new_rs)

        zero = jnp.zeros((B, 1), F32)
        o, _ = lax.fori_loop(0, i + 1, step, (jnp.zeros((B, LANE), F32), tuple(zero for _ in masks)))
        o_ref[...] = o

    return pl.pallas_call(
        body, grid=(npair, S // B),
        in_specs=[pl.BlockSpec((B, LANE), lambda p, i: (i, p)),
                  pl.BlockSpec((S, LANE), lambda p, i: (0, npair + p)),
                  pl.BlockSpec((S, LANE), lambda p, i: (0, 2 * npair + p))],
        out_specs=pl.BlockSpec((B, LANE), lambda p, i: (i, p)),
        out_shape=jax.ShapeDtypeStruct((S, D), F32), name=name,
        compiler_params=_params(("parallel", "arbitrary"), 2 * S * LANE * 2),
    )(qkv, qkv, qkv)


def sb_bwd(qkv, o, do, *, name):
    S, D3 = qkv.shape
    D = D3 // 3
    B = SB_BLOCK
    npair = D // LANE
    scale = SB_HEAD_DIM ** -0.5

    def body(q_ref, k_ref, v_ref, o_ref, do_ref, dq_ref, dk_ref, dv_ref):
        i = pl.program_id(1)

        @pl.when(i == 0)
        def _():
            dk_ref[...] = jnp.zeros_like(dk_ref)
            dv_ref[...] = jnp.zeros_like(dv_ref)

        masks = _head_masks(LANE)
        q = q_ref[...]
        dov = do_ref[...]
        ov = o_ref[...]
        qh = [jnp.where(m, q, jnp.zeros_like(q)) for m in masks]
        doh = [jnp.where(m, dov, 0.0).astype(BF16) for m in masks]
        gsum = [jnp.sum(dh_.astype(F32) * ov, axis=1, keepdims=True) for dh_ in doh]
        row = lax.broadcasted_iota(jnp.int32, (B, B), 0)
        col = lax.broadcasted_iota(jnp.int32, (B, B), 1)
        upper = (row > col).astype(BF16)
        upper_incl = (row >= col).astype(BF16)

        def step(n, carry):
            dq, rs, gs = carry
            j = i - n
            off = pl.multiple_of(j * B, B)
            kb = k_ref[pl.ds(off, B), :]
            vb = v_ref[pl.ds(off, B), :]
            mask = (j * B + col) < (i * B + row)
            dkb = jnp.zeros((B, LANE), F32)
            dvb = jnp.zeros((B, LANE), F32)
            new_rs, new_gs = [], []
            for hd, m in enumerate(masks):
                z = lax.dot_general(qh[hd], kb, _NT, preferred_element_type=F32) * scale
                a, lsn, e = _sb_tile(z, mask, upper, rs[hd])
                sig = jnp.where(z >= 0.0, 1.0, e) / (1.0 + e)
                da = lax.dot_general(doh[hd], vb, _NT, preferred_element_type=F32)
                ab = a.astype(BF16)
                g = ab.astype(F32) * da
                sg = _dot3r(g, upper_incl) + gs[hd]
                dz = jnp.where(mask, g * (1.0 - sig) - (gsum[hd] - sg) * sig, 0.0) * scale
                dzb = dz.astype(BF16)
                dq = dq + jnp.dot(dzb, jnp.where(m, kb, jnp.zeros_like(kb)), preferred_element_type=F32)
                dkb = dkb + lax.dot_general(dzb, qh[hd], _TN, preferred_element_type=F32)
                dvb = dvb + lax.dot_general(ab, doh[hd], _TN, preferred_element_type=F32)
                new_rs.append(rs[hd] + jnp.sum(lsn, axis=1, keepdims=True))
                new_gs.append(gs[hd] + jnp.sum(g, axis=1, keepdims=True))
            dk_ref[pl.ds(off, B), :] += dkb
            dv_ref[pl.ds(off, B), :] += dvb
            return dq, tuple(new_rs), tuple(new_gs)

        zero = jnp.zeros((B, 1), F32)
        zs = tuple(zero for _ in masks)
        dq, _, _ = lax.fori_loop(0, i + 1, step, (jnp.zeros((B, LANE), F32), zs, zs))
        dq_ref[...] = dq

    tile = pl.BlockSpec((B, LANE), lambda p, i: (i, p))
    strip = pl.BlockSpec((S, LANE), lambda p, i: (0, p))
    return pl.pallas_call(
        body, grid=(npair, S // B),
        in_specs=[tile, pl.BlockSpec((S, LANE), lambda p, i: (0, npair + p)),
                  pl.BlockSpec((S, LANE), lambda p, i: (0, 2 * npair + p)), tile, tile],
        out_specs=[tile, strip, strip],
        out_shape=[jax.ShapeDtypeStruct((S, D), F32)] * 3, name=name,
        compiler_params=_params(("parallel", "arbitrary"), 2 * S * LANE * 2 + 2 * S * LANE * 4),
    )(qkv, qkv, qkv, o, do)


S5_HALF = S5_GROUPS_PER_BLOCK * S5_STATE
S5_BLOCK = 2 * S5_HALF


def s5_scan(bu, pw, *, reverse, name, tr=512):
    S, W = bu.shape
    nb = W // S5_BLOCK
    tr = _tile(S, tr, SUBLANE)
    nsub = tr // SUBLANE
    nt = S // tr
    H = S5_HALF

    def body(bu_ref, pw_ref, x_ref, st_re, st_im):
        i = pl.program_id(1)

        @pl.when(i == 0)
        def _():
            st_re[...] = jnp.zeros_like(st_re)
            st_im[...] = jnp.zeros_like(st_im)

        row = lax.broadcasted_iota(jnp.int32, (SUBLANE, H), 0)
        steps = []
        for k, sh in enumerate((1, 2, 4)):
            valid = (row < SUBLANE - sh) if reverse else (row >= sh)
            steps.append((sh, valid, pw_ref[SUBLANE + k:SUBLANE + k + 1, 0:H], pw_ref[SUBLANE + k:SUBLANE + k + 1, H:2 * H]))
        ap_re = pw_ref[0:SUBLANE, 0:H]
        ap_im = pw_ref[0:SUBLANE, H:2 * H]
        edge = (row == 0) if reverse else (row == SUBLANE - 1)

        def sub(n, carry):
            s_re, s_im = carry
            j = (nsub - 1 - n) if reverse else n
            off = pl.multiple_of(j * SUBLANE, SUBLANE)
            r = bu_ref[pl.ds(off, SUBLANE), 0:H]
            m = bu_ref[pl.ds(off, SUBLANE), H:2 * H]
            for sh, valid, a_re, a_im in steps:
                amt = (SUBLANE - sh) if reverse else sh
                rs = jnp.where(valid, pltpu.roll(r, amt, 0), 0.0)
                ms = jnp.where(valid, pltpu.roll(m, amt, 0), 0.0)
                r, m = r + a_re * rs - a_im * ms, m + a_re * ms + a_im * rs
            r, m = r + ap_re * s_re - ap_im * s_im, m + ap_re * s_im + ap_im * s_re
            x_ref[pl.ds(off, SUBLANE), 0:H] = r
            x_ref[pl.ds(off, SUBLANE), H:2 * H] = m
            return (jnp.sum(jnp.where(edge, r, 0.0), axis=0, keepdims=True),
                    jnp.sum(jnp.where(edge, m, 0.0), axis=0, keepdims=True))

        s_re, s_im = lax.fori_loop(0, nsub, sub, (st_re[...], st_im[...]))
        st_re[...] = s_re
        st_im[...] = s_im

    if reverse:
        tile = pl.BlockSpec((tr, S5_BLOCK), lambda b, i: (nt - 1 - i, b))
    else:
        tile = pl.BlockSpec((tr, S5_BLOCK), lambda b, i: (i, b))
    return pl.pallas_call(
        body, grid=(nb, nt),
        in_specs=[tile, pl.BlockSpec((2 * SUBLANE, S5_BLOCK), lambda b, i: (0, b))],
        out_specs=tile, out_shape=jax.ShapeDtypeStruct((S, W), F32),
        scratch_shapes=[pltpu.VMEM((1, H), F32), pltpu.VMEM((1, H), F32)], name=name,
        compiler_params=_params(("parallel", "arbitrary"), 2 * tr * S5_BLOCK * 4),
    )(bu, pw)


def s5_da(lam, x, *, name, tr=512):
    S, W = lam.shape
    nb = W // S5_BLOCK
    tr = _tile(S, tr, SUBLANE)
    nsub = tr // SUBLANE
    nt = S // tr
    H = S5_HALF

    def body(l_ref, x_ref, xh_ref, o_ref, acc_re, acc_im):
        i = pl.program_id(1)

        @pl.when(i == 0)
        def _():
            acc_re[...] = jnp.zeros_like(acc_re)
            acc_im[...] = jnp.zeros_like(acc_im)

        row = lax.broadcasted_iota(jnp.int32, (SUBLANE, H), 0)
        first = row == 0

        def sub(n, carry):
            a_re, a_im = carry
            off = pl.multiple_of(n * SUBLANE, SUBLANE)
            poff = pl.multiple_of(jnp.maximum(n - 1, 0) * SUBLANE, SUBLANE)
            inside = n > 0
            start = jnp.logical_and(i == 0, n == 0)
            out = []
            for lo in (0, H):
                cur = x_ref[pl.ds(off, SUBLANE), lo:lo + H]
                prv = jnp.where(inside, x_ref[pl.ds(poff, SUBLANE), lo:lo + H], xh_ref[:, lo:lo + H])
                xs = jnp.where(first, pltpu.roll(prv, 1, 0), pltpu.roll(cur, 1, 0))
                out.append(jnp.where(jnp.logical_and(start, first), 0.0, xs))
            xs_re, xs_im = out
            l_re = l_ref[pl.ds(off, SUBLANE), 0:H]
            l_im = l_ref[pl.ds(off, SUBLANE), H:2 * H]
            return a_re + l_re * xs_re + l_im * xs_im, a_im + l_im * xs_re - l_re * xs_im

        a_re, a_im = lax.fori_loop(0, nsub, sub, (acc_re[...], acc_im[...]))
        acc_re[...] = a_re
        acc_im[...] = a_im

        @pl.when(i == nt - 1)
        def _():
            o_ref[:, 0:H] = jnp.sum(a_re, axis=0, keepdims=True)
            o_ref[:, H:2 * H] = jnp.sum(a_im, axis=0, keepdims=True)

    per = tr // SUBLANE
    tile = pl.BlockSpec((tr, S5_BLOCK), lambda b, i: (i, b))
    return pl.pallas_call(
        body, grid=(nb, nt),
        in_specs=[tile, tile, pl.BlockSpec((SUBLANE, S5_BLOCK), lambda b, i: (jnp.maximum(i * per - 1, 0), b))],
        out_specs=pl.BlockSpec((1, S5_BLOCK), lambda b, i: (0, b)),
        out_shape=jax.ShapeDtypeStruct((1, W), F32),
        scratch_shapes=[pltpu.VMEM((SUBLANE, H), F32), pltpu.VMEM((SUBLANE, H), F32)], name=name,
        compiler_params=_params(("parallel", "arbitrary"), 2 * tr * S5_BLOCK * 4),
    )(lam, x, x)


def _gelu(y):
    c = math.sqrt(2.0 / math.pi)
    return 0.5 * y * (1.0 + jnp.tanh(c * (y + 0.044715 * y * y * y)))


def _gelu_grad(y):
    c = math.sqrt(2.0 / math.pi)
    th = jnp.tanh(c * (y + 0.044715 * y * y * y))
    return 0.5 * (1.0 + th) + 0.5 * y * (1.0 - th * th) * c * (1.0 + 3.0 * 0.044715 * y * y)


def _sigmoid(x):
    e = jnp.exp(-jnp.abs(x))
    return jnp.where(x >= 0.0, 1.0, e) / (1.0 + e)


def _xa_probs(qh, kh, scale):
    s = lax.dot_general(qh, kh, _NT, preferred_element_type=F32) * scale
    p = jnp.exp(s - jnp.max(s, axis=-1, keepdims=True))
    return p / jnp.sum(p, axis=-1, keepdims=True)


def xa_fwd(q, kv, *, name, tm=512):
    S, D = q.shape
    M = kv.shape[0]
    dh = D // XA_HEADS
    scale = dh ** -0.5
    tm = _tile(S, tm)

    def body(q_ref, kv_ref, o_ref):
        for h in range(XA_HEADS):
            p = _xa_probs(q_ref[:, h * dh:(h + 1) * dh], kv_ref[:, h * dh:(h + 1) * dh], scale)
            o_ref[:, h * dh:(h + 1) * dh] = jnp.dot(p.astype(BF16), kv_ref[:, D + h * dh:D + (h + 1) * dh],
                                                   preferred_element_type=F32).astype(BF16)

    return pl.pallas_call(
        body, grid=(S // tm,),
        in_specs=[pl.BlockSpec((tm, D), lambda i: (i, 0)), pl.BlockSpec((M, 2 * D), lambda i: (0, 0))],
        out_specs=pl.BlockSpec((tm, D), lambda i: (i, 0)),
        out_shape=jax.ShapeDtypeStruct((S, D), BF16), name=name,
        compiler_params=_params(("parallel",), 2 * tm * D * 2 + M * 2 * D * 2),
    )(q, kv)


def xa_bwd(q, kv, do, *, name, tm=512):
    S, D = q.shape
    M = kv.shape[0]
    dh = D // XA_HEADS
    scale = dh ** -0.5
    tm = _tile(S, tm)

    def body(q_ref, kv_ref, do_ref, dq_ref, dkv_ref):
        i = pl.program_id(0)

        @pl.when(i == 0)
        def _():
            dkv_ref[...] = jnp.zeros_like(dkv_ref)

        for h in range(XA_HEADS):
            sl = slice(h * dh, (h + 1) * dh)
            vsl = slice(D + h * dh, D + (h + 1) * dh)
            qh, kh, vh = q_ref[:, sl], kv_ref[:, sl], kv_ref[:, vsl]
            doh = do_ref[:, sl].astype(BF16)
            p = _xa_probs(qh, kh, scale)
            dp = lax.dot_general(doh, vh, _NT, preferred_element_type=F32)
            ds = (p * (dp - jnp.sum(dp * p, axis=-1, keepdims=True)) * scale).astype(BF16)
            dq_ref[:, sl] = jnp.dot(ds, kh, preferred_element_type=F32).astype(BF16)
            dkv_ref[:, sl] += lax.dot_general(ds, qh, _TN, preferred_element_type=F32)
            dkv_ref[:, vsl] += lax.dot_general(p.astype(BF16), doh, _TN, preferred_element_type=F32)

    return pl.pallas_call(
        body, grid=(S // tm,),
        in_specs=[pl.BlockSpec((tm, D), lambda i: (i, 0)), pl.BlockSpec((M, 2 * D), lambda i: (0, 0)),
                  pl.BlockSpec((tm, D), lambda i: (i, 0))],
        out_specs=[pl.BlockSpec((tm, D), lambda i: (i, 0)), pl.BlockSpec((M, 2 * D), lambda i: (0, 0))],
        out_shape=[jax.ShapeDtypeStruct((S, D), BF16), jax.ShapeDtypeStruct((M, 2 * D), F32)], name=name,
        compiler_params=_params(("arbitrary",), 3 * tm * D * 2 + M * 2 * D * 6),
    )(q, kv, do)


FFN_STRIP = 256
FFN_ROWS = 512


def _shift_down(x, k):
    row = lax.broadcasted_iota(jnp.int32, x.shape, 0)
    return jnp.where(row >= k, pltpu.roll(x, k, 0), 0.0)


def _shift_up(x, k):
    n = x.shape[0]
    row = lax.broadcasted_iota(jnp.int32, x.shape, 0)
    return jnp.where(row < n - k, pltpu.roll(x, n - k, 0), 0.0)


FFN_HALO = 2 * SUBLANE


def _rows_with_prev(u_ref, r0, R):
    cur = u_ref[pl.ds(r0, R), :].astype(F32)
    p0 = pl.multiple_of(jnp.maximum(r0 - FFN_HALO, 0), FFN_HALO)
    prev = jnp.where(r0 > 0, u_ref[pl.ds(p0, FFN_HALO), :].astype(F32), 0.0)
    return jnp.concatenate([prev, cur], axis=0), cur


def _conv_rows(u_ref, r0, R, w_ref, b_ref):
    ext, _ = _rows_with_prev(u_ref, r0, R)
    out = w_ref[2:3, :] * ext + w_ref[1:2, :] * _shift_down(ext, 1) + w_ref[0:1, :] * _shift_down(ext, 2) + b_ref[...]
    return out[FFN_HALO:, :]


def ffn_act_fwd(u, conv_w, conv_b, *, name):
    S, F2 = u.shape
    F = F2 // 2
    tc = _tile(F, FFN_STRIP)
    nc = F // tc
    R = _tile(S, FFN_ROWS, 16)

    def body(uv_ref, ug_ref, wv_ref, wg_ref, bv_ref, bg_ref, o_ref):
        def rows(n, _):
            r0 = pl.multiple_of(n * R, R)
            val = _conv_rows(uv_ref, r0, R, wv_ref, bv_ref)
            gate = _conv_rows(ug_ref, r0, R, wg_ref, bg_ref)
            o_ref[pl.ds(r0, R), :] = (gate * _sigmoid(gate) * val).astype(BF16)
            return 0

        lax.fori_loop(0, S // R, rows, 0)

    return pl.pallas_call(
        body, grid=(nc,),
        in_specs=[pl.BlockSpec((S, tc), lambda c: (0, c)), pl.BlockSpec((S, tc), lambda c: (0, nc + c)),
                  pl.BlockSpec((CONV_WIDTH, tc), lambda c: (0, c)), pl.BlockSpec((CONV_WIDTH, tc), lambda c: (0, nc + c)),
                  pl.BlockSpec((1, tc), lambda c: (0, c)), pl.BlockSpec((1, tc), lambda c: (0, nc + c))],
        out_specs=pl.BlockSpec((S, tc), lambda c: (0, c)),
        out_shape=jax.ShapeDtypeStruct((S, F), BF16), name=name,
        compiler_params=_params(("parallel",), 3 * S * tc * 2),
    )(u, u, conv_w, conv_w, conv_b, conv_b)


def ffn_act_bwd(u, dact, conv_w, conv_b, *, name):
    S, F2 = u.shape
    F = F2 // 2
    tc = _tile(F, FFN_STRIP)
    nc = F // tc
    R = _tile(S, FFN_ROWS, 16)
    nr = S // R
    HALO = FFN_HALO

    def body(uv_ref, ug_ref, da_ref, wv_ref, wg_ref, bv_ref, bg_ref,
             duv_ref, dug_ref, dwv_ref, dwg_ref, dbv_ref, dbg_ref, dcv, dcg):
        def p1(n, _):
            r0 = pl.multiple_of(n * R, R)
            val = _conv_rows(uv_ref, r0, R, wv_ref, bv_ref)
            gate = _conv_rows(ug_ref, r0, R, wg_ref, bg_ref)
            d = da_ref[pl.ds(r0, R), :].astype(F32)
            sg = _sigmoid(gate)
            dcv[pl.ds(r0, R), :] = d * gate * sg
            dcg[pl.ds(r0, R), :] = d * val * (sg + gate * sg * (1.0 - sg))
            return 0

        lax.fori_loop(0, nr, p1, 0)

        def p2(n, carry):
            r0 = pl.multiple_of(n * R, R)
            nxt = pl.multiple_of(jnp.minimum(r0 + R, S - HALO), HALO)
            new = []
            for u_ref, dc, w_ref, du_ref, acc in ((uv_ref, dcv, wv_ref, duv_ref, carry[0]),
                                                  (ug_ref, dcg, wg_ref, dug_ref, carry[1])):
                d = dc[pl.ds(r0, R), :]
                after = jnp.where(r0 + R < S, dc[pl.ds(nxt, HALO), :], 0.0)
                ext = jnp.concatenate([d, after], axis=0)
                du = w_ref[2:3, :] * ext + w_ref[1:2, :] * _shift_up(ext, 1) + w_ref[0:1, :] * _shift_up(ext, 2)
                du_ref[pl.ds(r0, R), :] = du[:R, :].astype(BF16)
                uext, cur = _rows_with_prev(u_ref, r0, R)
                u1 = _shift_down(uext, 1)[HALO:, :]
                u2 = _shift_down(uext, 2)[HALO:, :]
                dw2, dw1, dw0, db = acc
                new.append((dw2 + _colsum(d * cur), dw1 + _colsum(d * u1), dw0 + _colsum(d * u2), db + _colsum(d)))
            return tuple(new)

        z = jnp.zeros((1, tc), F32)
        accs = lax.fori_loop(0, nr, p2, ((z, z, z, z), (z, z, z, z)))
        for (dw2, dw1, dw0, db), dw_ref, db_ref in ((accs[0], dwv_ref, dbv_ref), (accs[1], dwg_ref, dbg_ref)):
            dw_ref[0:1, :] = dw0
            dw_ref[1:2, :] = dw1
            dw_ref[2:3, :] = dw2
            db_ref[...] = db

    strip_v = pl.BlockSpec((S, tc), lambda c: (0, c))
    strip_g = pl.BlockSpec((S, tc), lambda c: (0, nc + c))
    w_v = pl.BlockSpec((CONV_WIDTH, tc), lambda c: (0, c))
    w_g = pl.BlockSpec((CONV_WIDTH, tc), lambda c: (0, nc + c))
    b_v = pl.BlockSpec((1, tc), lambda c: (0, c))
    b_g = pl.BlockSpec((1, tc), lambda c: (0, nc + c))
    outs = pl.pallas_call(
        body, grid=(nc,),
        in_specs=[strip_v, strip_g, strip_v, w_v, w_g, b_v, b_g],
        out_specs=[strip_v, strip_v, w_v, w_v, b_v, b_v],
        out_shape=[jax.ShapeDtypeStruct((S, F), BF16), jax.ShapeDtypeStruct((S, F), BF16),
                   jax.ShapeDtypeStruct((CONV_WIDTH, F), F32), jax.ShapeDtypeStruct((CONV_WIDTH, F), F32),
                   jax.ShapeDtypeStruct((1, F), F32), jax.ShapeDtypeStruct((1, F), F32)],
        scratch_shapes=[pltpu.VMEM((S, tc), F32), pltpu.VMEM((S, tc), F32)], name=name,
        compiler_params=_params(("parallel",), 5 * S * tc * 2 + S * tc * 4),
    )(u, u, dact, conv_w, conv_w, conv_b, conv_b)
    duv, dug, dwv, dwg, dbv, dbg = outs
    return duv, dug, jnp.concatenate([dwv, dwg], axis=1), jnp.concatenate([dbv, dbg], axis=1)


def _s5_discretize(a_re, a_im, log_dt, b_re, b_im):
    lam = lax.complex(a_re, a_im)
    dt_lam = lam * jnp.exp(log_dt)[:, None]
    a_bar = jnp.exp(dt_lam)
    b_bar = ((a_bar - 1.0) / lam)[..., None] * lax.complex(b_re, b_im)
    return jnp.real(a_bar), jnp.imag(a_bar), jnp.real(b_bar), jnp.imag(b_bar)


def _s5_cols(z_re, z_im, nb):
    lead = z_re.shape[:-2]
    re = z_re.reshape(*lead, nb, S5_HALF)
    im = z_im.reshape(*lead, nb, S5_HALF)
    return jnp.concatenate([re, im], axis=-1).reshape(*lead, nb * S5_BLOCK)


def _s5_powers(a_re, a_im, log_dt, nb, *, reverse):
    dt_lam = lax.complex(a_re, a_im) * jnp.exp(log_dt)[:, None]
    if reverse:
        dt_lam = jnp.conj(dt_lam)
        carry = jnp.arange(SUBLANE, 0, -1, dtype=F32)
    else:
        carry = jnp.arange(1, SUBLANE + 1, dtype=F32)
    ks = jnp.concatenate([carry, jnp.array([1.0, 2.0, 4.0], F32), jnp.zeros((SUBLANE - 3,), F32)])
    pw = jnp.exp(ks[:, None, None] * dt_lam[None])
    return _s5_cols(jnp.real(pw), jnp.imag(pw), nb)


def _s5_in_weights(bb_re, bb_im, nb):
    eye = jnp.eye(S5_GROUPS_PER_BLOCK, dtype=F32)
    G, P, Cg = bb_re.shape

    def one(bb):
        t = jnp.einsum("bgpi,gh->bgihp", bb.reshape(nb, S5_GROUPS_PER_BLOCK, P, Cg), eye)
        return t.reshape(nb, S5_GROUPS_PER_BLOCK * Cg, S5_HALF)

    return jnp.concatenate([one(bb_re), one(bb_im)], axis=2)


def _s5_out_weights(c_re, c_im, nb):
    eye = jnp.eye(S5_GROUPS_PER_BLOCK, dtype=F32)
    G, Cg, P = c_re.shape

    def one(c):
        t = jnp.einsum("bgip,gh->bgphi", c.reshape(nb, S5_GROUPS_PER_BLOCK, Cg, P), eye)
        return t.reshape(nb, S5_HALF, S5_GROUPS_PER_BLOCK * Cg)

    return jnp.concatenate([one(c_re), -one(c_im)], axis=1)


def _s5_in_weight_grads(dwb, Cg):
    nb = dwb.shape[0]
    eye = jnp.eye(S5_GROUPS_PER_BLOCK, dtype=F32)
    t = dwb.reshape(nb, S5_GROUPS_PER_BLOCK, Cg, 2, S5_GROUPS_PER_BLOCK, S5_STATE)
    out = jnp.einsum("bgirhp,gh->rbgpi", t, eye)
    return out[0].reshape(-1, S5_STATE, Cg), out[1].reshape(-1, S5_STATE, Cg)


def _s5_out_weight_grads(dwc, Cg):
    nb = dwc.shape[0]
    eye = jnp.eye(S5_GROUPS_PER_BLOCK, dtype=F32)
    t = dwc.reshape(nb, 2, S5_GROUPS_PER_BLOCK, S5_STATE, S5_GROUPS_PER_BLOCK, Cg)
    out = jnp.einsum("brgphi,gh->rbgip", t, eye)
    return out[0].reshape(-1, Cg, S5_STATE), -out[1].reshape(-1, Cg, S5_STATE)


ANY = pl.BlockSpec(memory_space=pl.ANY)
N_CHIPS = 4
N_DEV = 8


def _place():
    x, y, c = lax.axis_index("x"), lax.axis_index("y"), lax.axis_index("c")
    chips = [(1 - x, y), (x, 1 - y), (1 - x, 1 - y)]
    return x, y, c, chips


def gather_chips(w, *, name):
    R, C = w.shape
    Hh = R // 2
    assert 2 * Hh == R

    def body(w_ref, out_ref, send_sems, recv_sems, local_sem):
        x, y, c, chips = _place()
        me = 2 * x + y
        sibling = (x, y, 1 - c)

        def half(chip, hc):
            return out_ref.at[chip, pl.ds(hc * Hh, Hh), :]

        def copy(k, src, dst, to):
            return pltpu.make_async_remote_copy(src_ref=src, dst_ref=dst, send_sem=send_sems.at[k],
                                                recv_sem=recv_sems.at[k], device_id=to, device_id_type=MESH)

        mine = pltpu.make_async_copy(w_ref, out_ref.at[me], local_sem)
        mine.start()
        first = [copy(j, w_ref.at[pl.ds(c * Hh, Hh), :], half(me, c), (px, py, c)) for j, (px, py) in enumerate(chips)]
        for cp in first:
            cp.start()
        passed = []
        for j, (px, py) in enumerate(chips):
            landed = half(2 * px + py, c)
            copy(j, landed, landed, (px, py, c)).wait_recv()
            fwd = copy(3 + j, landed, landed, sibling)
            fwd.start()
            passed.append(fwd)
        for j, (px, py) in enumerate(chips):
            theirs = half(2 * px + py, 1 - c)
            copy(3 + j, theirs, theirs, sibling).wait_recv()
        for cp in first + passed:
            cp.wait_send()
        mine.wait()

    return pl.pallas_call(
        body, in_specs=[ANY], out_specs=ANY, out_shape=jax.ShapeDtypeStruct((N_CHIPS, R, C), w.dtype),
        scratch_shapes=[pltpu.SemaphoreType.DMA((6,)), pltpu.SemaphoreType.DMA((6,)), pltpu.SemaphoreType.DMA], name=name,
    )(w)


def swap_halves(g4, *, name):
    n, R, C = g4.shape
    Hh = R // 2

    def body(g_ref, out_ref, send_sem, recv_sem):
        x, y, c, _ = _place()
        cp = pltpu.make_async_remote_copy(
            src_ref=g_ref.at[pl.ds(0, n), pl.ds((1 - c) * Hh, Hh), :], dst_ref=out_ref, send_sem=send_sem,
            recv_sem=recv_sem, device_id=(x, y, 1 - c), device_id_type=MESH)
        cp.start()
        cp.wait()

    return pl.pallas_call(
        body, in_specs=[ANY], out_specs=ANY, out_shape=jax.ShapeDtypeStruct((n, Hh, C), g4.dtype),
        scratch_shapes=[pltpu.SemaphoreType.DMA, pltpu.SemaphoreType.DMA], name=name,
    )(g4)


def add_half(g4, other, *, name, tr=160):
    n, R, C = g4.shape
    Hh = R // 2
    tr = _tile(Hh, tr, 16)
    nblk = Hh // tr
    cidx = lax.axis_index("c").astype(jnp.int32).reshape(1)

    def body(c_ref, g_ref, o_ref, out_ref):
        out_ref[...] = (g_ref[...].astype(F32) + o_ref[...].astype(F32)).astype(out_ref.dtype)

    return pl.pallas_call(
        body,
        grid_spec=pltpu.PrefetchScalarGridSpec(
            num_scalar_prefetch=1, grid=(nblk,),
            in_specs=[pl.BlockSpec((n, tr, C), lambda i, c_ref: (0, c_ref[0] * nblk + i, 0)),
                      pl.BlockSpec((n, tr, C), lambda i, c_ref: (0, i, 0))],
            out_specs=pl.BlockSpec((n, tr, C), lambda i, c_ref: (0, i, 0))),
        out_shape=jax.ShapeDtypeStruct((n, Hh, C), g4.dtype), name=name,
        compiler_params=_params(("parallel",), 3 * n * tr * C * 2),
    )(cidx, g4, other)


def scatter_chips(p4, *, name):
    n, Hh, C = p4.shape

    def body(p_ref, out_ref, send_sems, recv_sems, local_sem):
        x, y, c, chips = _place()
        me = 2 * x + y
        mine = pltpu.make_async_copy(p_ref.at[me], out_ref.at[me], local_sem)
        mine.start()
        sends = []
        for j, (px, py) in enumerate(chips):
            cp = pltpu.make_async_remote_copy(src_ref=p_ref.at[2 * px + py], dst_ref=out_ref.at[me], send_sem=send_sems.at[j],
                                              recv_sem=recv_sems.at[j], device_id=(px, py, c), device_id_type=MESH)
            cp.start()
            sends.append(cp)
        for j, (px, py) in enumerate(chips):
            slot = out_ref.at[2 * px + py]
            pltpu.make_async_remote_copy(src_ref=slot, dst_ref=slot, send_sem=send_sems.at[j], recv_sem=recv_sems.at[j],
                                         device_id=(px, py, c), device_id_type=MESH).wait_recv()
        for cp in sends:
            cp.wait_send()
        mine.wait()

    return pl.pallas_call(
        body, in_specs=[ANY], out_specs=ANY, out_shape=jax.ShapeDtypeStruct((n, Hh, C), p4.dtype),
        scratch_shapes=[pltpu.SemaphoreType.DMA((3,)), pltpu.SemaphoreType.DMA((3,)), pltpu.SemaphoreType.DMA], name=name,
    )(p4)


def sum_leading(x3, *, name, tr=160, align=16):
    n, R, C = x3.shape
    tr = _tile(R, tr, align)

    def body(x_ref, o_ref):
        acc = x_ref[0].astype(F32)
        for k in range(1, n):
            acc = acc + x_ref[k].astype(F32)
        o_ref[...] = acc

    return pl.pallas_call(
        body, grid=(R // tr,), in_specs=[pl.BlockSpec((n, tr, C), lambda i: (0, i, 0))],
        out_specs=pl.BlockSpec((tr, C), lambda i: (i, 0)), out_shape=jax.ShapeDtypeStruct((R, C), F32), name=name,
        compiler_params=_params(("parallel",), n * tr * C * 4 + tr * C * 4),
    )(x3)


def join_halves(r, *, name):
    Hh, C = r.shape

    def body(r_ref, out_ref, send_sem, recv_sem, local_sem):
        x, y, c, _ = _place()
        mine = pltpu.make_async_copy(r_ref, out_ref.at[pl.ds(c * Hh, Hh), :], local_sem)
        mine.start()
        cp = pltpu.make_async_remote_copy(src_ref=r_ref, dst_ref=out_ref.at[pl.ds(c * Hh, Hh), :], send_sem=send_sem,
                                          recv_sem=recv_sem, device_id=(x, y, 1 - c), device_id_type=MESH)
        cp.start()
        theirs = out_ref.at[pl.ds((1 - c) * Hh, Hh), :]
        pltpu.make_async_remote_copy(src_ref=theirs, dst_ref=theirs, send_sem=send_sem, recv_sem=recv_sem,
                                     device_id=(x, y, 1 - c), device_id_type=MESH).wait_recv()
        cp.wait_send()
        mine.wait()

    return pl.pallas_call(
        body, in_specs=[ANY], out_specs=ANY, out_shape=jax.ShapeDtypeStruct((2 * Hh, C), r.dtype),
        scratch_shapes=[pltpu.SemaphoreType.DMA, pltpu.SemaphoreType.DMA, pltpu.SemaphoreType.DMA], name=name,
    )(r)


def gather_devices(v, *, name):
    m_per, n = v.shape

    def body(x_ref, out_ref, send_sems, recv_sems, local_sem):
        x, y, c, chips = _place()
        me, sibling = (x, y, c), (x, y, 1 - c)

        def rows(px, py, pc):
            return out_ref.at[pl.ds((4 * px + 2 * py + pc) * m_per, m_per), :]

        def copy(k, block, to, src=None):
            return pltpu.make_async_remote_copy(src_ref=rows(*block) if src is None else src, dst_ref=rows(*block),
                                                send_sem=send_sems.at[k], recv_sem=recv_sems.at[k], device_id=to,
                                                device_id_type=MESH)

        mine = pltpu.make_async_copy(x_ref, rows(*me), local_sem)
        mine.start()
        first = [copy(0, me, sibling, src=x_ref)]
        first += [copy(1 + j, me, (*chip, c), src=x_ref) for j, chip in enumerate(chips)]
        for cp in first:
            cp.start()
        passed = [copy(4 + j, (*chip, c), sibling) for j, chip in enumerate(chips)]
        for j, chip in enumerate(chips):
            copy(1 + j, (*chip, c), me).wait_recv()
            passed[j].start()
        copy(0, sibling, me).wait_recv()
        for j, chip in enumerate(chips):
            copy(4 + j, (*chip, 1 - c), me).wait_recv()
        for cp in first + passed:
            cp.wait_send()
        mine.wait()

    return pl.pallas_call(
        body, out_shape=jax.ShapeDtypeStruct((N_DEV * m_per, n), v.dtype),
        in_specs=[pl.BlockSpec(memory_space=pltpu.VMEM)], out_specs=pl.BlockSpec(memory_space=pltpu.VMEM),
        scratch_shapes=[pltpu.SemaphoreType.DMA((7,)), pltpu.SemaphoreType.DMA((7,)), pltpu.SemaphoreType.DMA], name=name,
        compiler_params=pltpu.CompilerParams(vmem_limit_bytes=_vmem_limit(9 * m_per * n * 4)),
    )(v)


def reduce_weight_grads(g4):
    other = swap_halves(g4, name="rs_swap_halves")
    part = add_half(g4, other, name="rs_add_half")
    landed = scatter_chips(part, name="rs_scatter_chips")
    mine = sum_leading(landed, name="rs_sum_chips")
    return join_halves(mine, name="rs_join_halves")


PACK_COLS = 1024
BIG = (("pool_w", 2), ("sb_w_qkv", 2), ("sb_w_o", 1), ("s5_w_glu", 2), ("xa_wq", 1), ("xa_wkv", 2), ("xa_wo", 1),
       ("ffn_w_up", 2), ("ffn_w_down", 1))
SMALL_SHARDED = (("pool_scale", 1), ("s5_d", 1), ("ffn_conv_w", 2))
REPLICATED = ("mix_norm_g", "s5_a_re", "s5_a_im", "s5_log_dt", "s5_b_re", "s5_b_im", "s5_c_re", "s5_c_im",
              "xa_norm_g", "mem_norm_g", "ffn_norm_g", "ffn_conv_b", "final_norm_g")
WEIGHTS = ("mix_norm_g", "pool_w", "pool_scale", "sb_w_qkv", "sb_w_o", "s5_a_re", "s5_a_im", "s5_log_dt", "s5_b_re",
           "s5_b_im", "s5_c_re", "s5_c_im", "s5_d", "s5_w_glu", "xa_norm_g", "mem_norm_g", "xa_wq", "xa_wkv", "xa_wo",
           "ffn_norm_g", "ffn_w_up", "ffn_conv_w", "ffn_conv_b", "ffn_w_down", "final_norm_g")


def _pack_rows(parts, row_align):
    flat = jnp.concatenate(parts, axis=-1)
    n = flat.shape[-1]
    per = PACK_COLS * row_align
    padded = -(-n // per) * per
    if padded != n:
        flat = jnp.pad(flat, [(0, 0)] * (flat.ndim - 1) + [(0, padded - n)])
    return flat.reshape(*flat.shape[:-1], padded // PACK_COLS, PACK_COLS)


def _to_natural(g, ax):
    g = jnp.moveaxis(g, 0, ax)
    sh = g.shape
    return g.reshape(*sh[:ax], sh[ax] * sh[ax + 1], *sh[ax + 2:])


def _to_chunks(a, ax):
    sh = a.shape
    a = a.reshape(*sh[:ax], N_CHIPS, sh[ax] // N_CHIPS, *sh[ax + 1:])
    return jnp.moveaxis(a, ax, 0).reshape(N_CHIPS, -1)


def _unpack(flat, shapes):
    out, off = [], 0
    for sh in shapes:
        n = math.prod(sh)
        out.append(flat[..., off:off + n].reshape(*flat.shape[:-1], *sh))
        off += n
    return out


def _mixer_kind(i):
    return i % 3, i // 3


def _s5_forward(hn, h, s5, tag):
    bu = bdmm(hn, s5["w_in"], out_dtype=F32, name=f"{tag}_s5_bu")
    xs = s5_scan(bu, s5["pw_fwd"], reverse=False, name=f"{tag}_s5_scan")
    ycx = bdmm(xs, s5["w_out"], out_dtype=F32, name=f"{tag}_s5_cx")
    D = hn.shape[1]

    def post(i, yv, uv, dv):
        return [_gelu(yv + dv * uv)]

    yg = ew(post, [(ycx, "tile"), (hn, "tile"), (s5["d"], "full")], [(D, BF16, "tile")], rows=hn.shape[0], tr=256,
            name=f"{tag}_s5_gelu")[0]
    vg = mm(yg, s5["w_glu"], out_dtype=F32, name=f"{tag}_s5_glu")

    def glu(i, vgv, hv):
        return [hv + vgv[:, :D] * _sigmoid(vgv[:, D:])]

    h1 = ew(glu, [(vg, "tile"), (h, "tile")], [(D, F32, "tile")], rows=hn.shape[0], tr=256, name=f"{tag}_s5_gate")[0]
    return h1, dict(xs=xs, ycx=ycx, yg=yg, vg=vg)


def _s5_backward(hn, dout, s5, sv, tag):
    S, D = hn.shape

    def dglu(i, vgv, dv):
        sg = _sigmoid(vgv[:, D:])
        return [jnp.concatenate([dv * sg, dv * vgv[:, :D] * sg * (1.0 - sg)], axis=1)]

    dvg = ew(dglu, [(sv["vg"], "tile"), (dout, "tile")], [(2 * D, BF16, "tile")], rows=S, tr=256, name=f"{tag}_s5_dgate")[0]
    dyg = mm(dvg, s5["w_glu"], tb=True, out_dtype=F32, name=f"{tag}_s5_dglu_x")
    dw_glu = mm(sv["yg"], dvg, ta=True, out_dtype=BF16, name=f"{tag}_s5_dglu_w")

    def dgelu(i, dygv, yv, uv, dv):
        dyp = dygv * _gelu_grad(yv + dv * uv)
        return [dyp, _colsum(dyp * uv)]

    dyp, dd = ew(dgelu, [(dyg, "tile"), (sv["ycx"], "tile"), (hn, "tile"), (s5["d"], "full")],
                 [(D, F32, "tile"), (D, F32, "acc")], rows=S, tr=256, name=f"{tag}_s5_dgelu")
    gx = bdmm(dyp, s5["w_out_t"], out_dtype=F32, name=f"{tag}_s5_dcx")
    dw_out = bdmm_tn(sv["xs"], dyp, ka=S5_BLOCK, kd=S5_GROUPS_PER_BLOCK * S5_GROUP, name=f"{tag}_s5_dwout")
    lam = s5_scan(gx, s5["pw_bwd"], reverse=True, name=f"{tag}_s5_scan_bwd")
    da = s5_da(lam, sv["xs"], name=f"{tag}_s5_da")
    dw_in = bdmm_tn(hn, lam, ka=S5_GROUPS_PER_BLOCK * S5_GROUP, kd=S5_BLOCK, name=f"{tag}_s5_dwin")
    du = bdmm(lam, s5["w_in_t"], out_dtype=F32, name=f"{tag}_s5_du")

    def dsum(i, duv, dypv, dv):
        return [duv + dypv * dv]

    dhn = ew(dsum, [(du, "tile"), (dyp, "tile"), (s5["d"], "full")], [(D, F32, "tile")], rows=S, tr=256,
             name=f"{tag}_s5_dhn")[0]
    return dhn, dict(dw_glu=dw_glu, dd=dd, dw_out=dw_out, dw_in=dw_in, da=da)


def _step(x, mem, target, w, m, v):
    S, D = x.shape
    depth = w["mix_norm_g"].shape[0]
    F = w["ffn_w_down"].shape[1] * N_CHIPS
    chip = 2 * lax.axis_index("x") + lax.axis_index("y")

    big_shapes = [w[n].shape for n, _ in BIG]
    packed = _pack_rows([w[n].astype(BF16).reshape(-1) for n, _ in BIG], 32)
    gathered = gather_chips(packed, name="ag_weights").reshape(N_CHIPS, -1)
    full = {n: _to_natural(p, ax) for (n, ax), p in zip(BIG, _unpack(gathered, big_shapes))}
    small_shapes = [w[n].shape for n, _ in SMALL_SHARDED]
    spacked = _pack_rows([w[n].reshape(-1) for n, _ in SMALL_SHARDED], SUBLANE)
    sgathered = gather_devices(spacked, name="ag_small").reshape(N_CHIPS, 2, -1)[:, 0]
    full.update({n: _to_natural(p, ax) for (n, ax), p in zip(SMALL_SHARDED, _unpack(sgathered, small_shapes))})

    n_s5 = w["s5_a_re"].shape[0]
    s5 = []
    for j in range(n_s5):
        G = w["s5_a_re"].shape[1]
        nb = G // S5_GROUPS_PER_BLOCK
        prm = (w["s5_a_re"][j], w["s5_a_im"][j], w["s5_log_dt"][j], w["s5_b_re"][j], w["s5_b_im"][j])
        (ab_re, ab_im, bb_re, bb_im), disc_vjp = jax.vjp(_s5_discretize, *prm)
        w_in = _s5_in_weights(bb_re, bb_im, nb)
        w_out = _s5_out_weights(w["s5_c_re"][j], w["s5_c_im"][j], nb)
        s5.append(dict(
            w_in=w_in, w_in_t=jnp.transpose(w_in, (0, 2, 1)), w_out=w_out, w_out_t=jnp.transpose(w_out, (0, 2, 1)),
            pw_fwd=_s5_powers(prm[0], prm[1], prm[2], nb, reverse=False),
            pw_bwd=_s5_powers(prm[0], prm[1], prm[2], nb, reverse=True),
            d=full["s5_d"][j][None], w_glu=full["s5_w_glu"][j], vjp=disc_vjp, nb=nb))

    h = x
    saved = []
    for i in range(depth):
        kind, j = _mixer_kind(i)
        tag = f"L{i}"
        sv = dict(h=h)
        g_mix = w["mix_norm_g"][i][None]
        if kind == 0:
            hn = rms_fwd(h, g_mix, out_dtype=F32, name=f"{tag}_mix_norm")
            h1 = pool_fwd(hn, h, full["pool_w"][j], full["pool_scale"][j][None], name=f"{tag}_pool")
        elif kind == 1:
            hn = rms_fwd(h, g_mix, out_dtype=BF16, name=f"{tag}_mix_norm")
            qkv = mm(hn, full["sb_w_qkv"][j], out_dtype=BF16, name=f"{tag}_sb_qkv")
            o = sb_fwd(qkv, name=f"{tag}_sb_attn")
            h1 = mm(o, full["sb_w_o"][j], res=h, name=f"{tag}_sb_out")
            sv.update(qkv=qkv, o=o)
        else:
            hn = rms_fwd(h, g_mix, out_dtype=F32, name=f"{tag}_mix_norm")
            h1, s5sv = _s5_forward(hn, h, s5[j], tag)
            sv.update(s5sv)
        sv.update(hn=hn, h1=h1)
        hq = rms_fwd(h1, w["xa_norm_g"][i][None], out_dtype=BF16, name=f"{tag}_xa_norm")
        memn = rms_fwd(mem, w["mem_norm_g"][i][None], out_dtype=BF16, name=f"{tag}_mem_norm", tr=mem.shape[0])
        q = mm(hq, full["xa_wq"][i], out_dtype=BF16, name=f"{tag}_xa_q")
        kv = mm(memn, full["xa_wkv"][i], out_dtype=BF16, name=f"{tag}_xa_kv")
        oa = xa_fwd(q, kv, name=f"{tag}_xa_attn")
        h2 = mm(oa, full["xa_wo"][i], res=h1, name=f"{tag}_xa_out")
        hf = rms_fwd(h2, w["ffn_norm_g"][i][None], out_dtype=BF16, name=f"{tag}_ffn_norm")
        uu = mm(hf, full["ffn_w_up"][i], out_dtype=BF16, tn=1408, name=f"{tag}_ffn_up")
        conv_w, conv_b = full["ffn_conv_w"][i], w["ffn_conv_b"][i][None]
        act = ffn_act_fwd(uu, conv_w, conv_b, name=f"{tag}_ffn_act")
        h3 = mm(act, full["ffn_w_down"][i], res=h2, name=f"{tag}_ffn_down")
        sv.update(hq=hq, memn=memn, q=q, kv=kv, oa=oa, h2=h2, hf=hf, uu=uu, act=act)
        saved.append(sv)
        h = h3

    dh, g_final, loss = loss_head(h, w["final_norm_g"][None], target, name="loss_head")

    gw = {n: [None] * w[n].shape[0] for n in WEIGHTS if n != "final_norm_g"}
    for i in reversed(range(depth)):
        kind, j = _mixer_kind(i)
        tag = f"L{i}b"
        sv = saved[i]
        conv_w, conv_b = full["ffn_conv_w"][i], w["ffn_conv_b"][i][None]
        dact = mm(dh, full["ffn_w_down"][i], tb=True, out_dtype=BF16, tn=1408, name=f"{tag}_ffn_down_x")
        gw["ffn_w_down"][i] = mm(sv["act"], dh, ta=True, out_dtype=BF16, tm=1408, name=f"{tag}_ffn_down_w")
        duv, dug, dcw, dcb = ffn_act_bwd(sv["uu"], dact, conv_w, conv_b, name=f"{tag}_ffn_act")
        gw["ffn_conv_w"][i], gw["ffn_conv_b"][i] = dcw, dcb[0]
        dhf = mm(duv, full["ffn_w_up"][i], tb=True, b_col0=0, name=f"{tag}_ffn_up_xv")
        dhf = mm(dug, full["ffn_w_up"][i], tb=True, b_col0=F, res=dhf, name=f"{tag}_ffn_up_xg")
        gw["ffn_w_up"][i] = jnp.concatenate(
            [mm(sv["hf"], duv, ta=True, out_dtype=BF16, tn=1408, name=f"{tag}_ffn_up_wv"),
             mm(sv["hf"], dug, ta=True, out_dtype=BF16, tn=1408, name=f"{tag}_ffn_up_wg")], axis=1)
        dh2, dg = rms_bwd(sv["h2"], w["ffn_norm_g"][i][None], dhf, dh, name=f"{tag}_ffn_norm")
        gw["ffn_norm_g"][i] = dg[0]

        doa = mm(dh2, full["xa_wo"][i], tb=True, out_dtype=BF16, name=f"{tag}_xa_out_x")
        gw["xa_wo"][i] = mm(sv["oa"], dh2, ta=True, out_dtype=BF16, name=f"{tag}_xa_out_w")
        dq, dkv = xa_bwd(sv["q"], sv["kv"], doa, name=f"{tag}_xa_attn")
        dhq = mm(dq, full["xa_wq"][i], tb=True, name=f"{tag}_xa_q_x")
        gw["xa_wq"][i] = mm(sv["hq"], dq, ta=True, out_dtype=BF16, name=f"{tag}_xa_q_w")
        dmemn = mm(dkv, full["xa_wkv"][i], tb=True, name=f"{tag}_xa_kv_x")
        gw["xa_wkv"][i] = mm(sv["memn"], dkv, ta=True, out_dtype=BF16, name=f"{tag}_xa_kv_w")
        gw["mem_norm_g"][i] = rms_bwd_g(mem, dmemn, name=f"{tag}_mem_norm")[0]
        dh1, dg = rms_bwd(sv["h1"], w["xa_norm_g"][i][None], dhq, dh2, name=f"{tag}_xa_norm")
        gw["xa_norm_g"][i] = dg[0]

        g_mix = w["mix_norm_g"][i][None]
        if kind == 0:
            dp, dpw, dps = pool_bwd_w(sv["hn"], dh1, full["pool_w"][j], full["pool_scale"][j][None], name=f"{tag}_pool_w")
            gw["pool_w"][j], gw["pool_scale"][j] = dpw, dps[0]
            dhn = pool_bwd_x(dp, len(POOL_WINDOWS), name=f"{tag}_pool_x")
        elif kind == 1:
            do = mm(dh1, full["sb_w_o"][j], tb=True, name=f"{tag}_sb_out_x")
            gw["sb_w_o"][j] = mm(sv["o"], dh1, ta=True, out_dtype=BF16, name=f"{tag}_sb_out_w")
            dq3 = sb_bwd(sv["qkv"], sv["o"], do, name=f"{tag}_sb_attn")
            dqkv = jnp.concatenate([t.astype(BF16) for t in dq3], axis=1)
            dhn = mm(dqkv, full["sb_w_qkv"][j], tb=True, name=f"{tag}_sb_qkv_x")
            gw["sb_w_qkv"][j] = mm(sv["hn"], dqkv, ta=True, out_dtype=BF16, name=f"{tag}_sb_qkv_w")
        else:
            dhn, sg = _s5_backward(sv["hn"], dh1, s5[j], sv, tag)
            Cg = w["s5_b_re"].shape[-1]
            nb = s5[j]["nb"]
            gw["s5_w_glu"][j], gw["s5_d"][j] = sg["dw_glu"], sg["dd"][0]
            da = sg["da"].reshape(nb, 2, -1)
            gw["s5_a_re"][j], gw["s5_a_im"][j] = da[:, 0].reshape(-1, S5_STATE), da[:, 1].reshape(-1, S5_STATE)
            gw["s5_b_re"][j], gw["s5_b_im"][j] = _s5_in_weight_grads(sg["dw_in"], Cg)
            gw["s5_c_re"][j], gw["s5_c_im"][j] = _s5_out_weight_grads(sg["dw_out"], Cg)
        dh, dg = rms_bwd(sv["h"], g_mix, dhn, dh1, name=f"{tag}_mix_norm")
        gw["mix_norm_g"][i] = dg[0]
    grad_x = dh

    g4 = _pack_rows([_to_chunks(jnp.stack(gw[n]).astype(BF16), ax) for n, ax in BIG], 32)
    reduced = reduce_weight_grads(g4).reshape(-1)
    grads = dict(zip([n for n, _ in BIG], _unpack(reduced, big_shapes)))

    s5_raw = ("s5_a_re", "s5_a_im", "s5_b_re", "s5_b_im")
    small_names = [n for n in REPLICATED if n not in ("final_norm_g", "s5_log_dt")] + [n for n, _ in SMALL_SHARDED]
    small_full = [jnp.stack(gw[n]).astype(F32) for n in small_names] + [g_final[0]]
    small_full_shapes = [t.shape for t in small_full]
    spk = _pack_rows([t.reshape(-1) for t in small_full], SUBLANE)
    everyone = gather_devices(spk, name="ar_small_gather").reshape(N_DEV, *spk.shape)
    ssum = sum_leading(everyone, name="ar_small_sum", align=SUBLANE).reshape(-1)
    small = dict(zip(small_names + ["final_norm_g"], _unpack(ssum, small_full_shapes)))
    per_layer = [[], [], [], [], []]
    for j in range(n_s5):
        ct = tuple(small[n][j] for n in s5_raw)
        for lst, gpart in zip(per_layer, s5[j]["vjp"]((ct[0], ct[1], ct[2], ct[3]))):
            lst.append(gpart)
    for n, lst in zip(("s5_a_re", "s5_a_im", "s5_log_dt", "s5_b_re", "s5_b_im"), per_layer):
        small[n] = jnp.stack(lst)
    for n, ax in SMALL_SHARDED:
        chunks = _to_chunks(small[n], ax)
        small[n] = lax.dynamic_index_in_dim(chunks, chip, 0, keepdims=False).reshape(w[n].shape)
    for n in REPLICATED:
        grads[n] = small[n].reshape(w[n].shape)
    for n, _ in SMALL_SHARDED:
        grads[n] = small[n]

    delta, new_m, new_v = {}, {}, {}
    for n in WEIGHTS:
        delta[n], new_m[n], new_v[n] = adamw(w[n], grads[n], m[n], v[n], name=f"adamw_{n}")
    total = lax.psum(loss[0, 0], ("x", "y", "c"))
    return (total, grad_x, *[grads[n] for n in WEIGHTS], *[delta[n] for n in WEIGHTS],
            *[new_m[n] for n in WEIGHTS], *[new_v[n] for n in WEIGHTS])


def kernel(x, mem, mix_norm_g, pool_w, pool_scale, sb_w_qkv, sb_w_o, s5_a_re, s5_a_im, s5_log_dt, s5_b_re, s5_b_im,
           s5_c_re, s5_c_im, s5_d, s5_w_glu, xa_norm_g, mem_norm_g, xa_wq, xa_wkv, xa_wo, ffn_norm_g, ffn_w_up,
           ffn_conv_w, ffn_conv_b, ffn_w_down, final_norm_g, loss_target, m_mix_norm_g, m_pool_w, m_pool_scale,
           m_sb_w_qkv, m_sb_w_o, m_s5_a_re, m_s5_a_im, m_s5_log_dt, m_s5_b_re, m_s5_b_im, m_s5_c_re, m_s5_c_im,
           m_s5_d, m_s5_w_glu, m_xa_norm_g, m_mem_norm_g, m_xa_wq, m_xa_wkv, m_xa_wo, m_ffn_norm_g, m_ffn_w_up,
           m_ffn_conv_w, m_ffn_conv_b, m_ffn_w_down, m_final_norm_g, v_mix_norm_g, v_pool_w, v_pool_scale,
           v_sb_w_qkv, v_sb_w_o, v_s5_a_re, v_s5_a_im, v_s5_log_dt, v_s5_b_re, v_s5_b_im, v_s5_c_re, v_s5_c_im,
           v_s5_d, v_s5_w_glu, v_xa_norm_g, v_mem_norm_g, v_xa_wq, v_xa_wkv, v_xa_wo, v_ffn_norm_g, v_ffn_w_up,
           v_ffn_conv_w, v_ffn_conv_b, v_ffn_w_down, v_final_norm_g):
    given = dict(locals())
    w = {n: given[n] for n in WEIGHTS}
    m = {n: given["m_" + n] for n in WEIGHTS}
    v = {n: given["v_" + n] for n in WEIGHTS}
    out = _step(x[0], mem[0], loss_target[0], w, m, v)
    return (out[0], out[1][None], *out[2:])
```

```python
import functools
import math

import jax
import jax.numpy as jnp
from jax import lax
from jax.experimental import pallas as pl
from jax.experimental.pallas import tpu as pltpu

F32 = jnp.float32
BF16 = jnp.bfloat16
MESH = pl.DeviceIdType.MESH

EPS = 1e-6
POOL_WINDOWS = (2, 4, 8, 16)
POOL_HALO = 128
POOL_TILE = 256
SB_HEAD_DIM = 64
SB_BLOCK = 128
S5_GROUP = 16
S5_STATE = 64
S5_GROUPS_PER_BLOCK = 8
XA_HEADS = 4
CONV_WIDTH = 3
ADAM_LR, ADAM_B1, ADAM_B2, ADAM_EPS, ADAM_WD, ADAM_STEP = 0.001, 0.9, 0.999, 1e-08, 0.01, 10

V7X_VMEM_BYTES = 64 * 1024 * 1024
LANE = 128
SUBLANE = 8


def _vmem_limit(block_bytes):
    want = 2 * block_bytes + 16 * 1024 * 1024
    return int(min(V7X_VMEM_BYTES - 6 * 1024 * 1024, max(32 * 1024 * 1024, want)))


def _params(sem, block_bytes):
    return pltpu.CompilerParams(dimension_semantics=sem, vmem_limit_bytes=_vmem_limit(block_bytes))


def _tile(n, cap, align=LANE):
    if n <= cap:
        return n
    t = (cap // align) * align
    while t >= align:
        if n % t == 0:
            return t
        t -= align
    return n


def _nbytes(shape, dtype):
    return math.prod(shape) * jnp.dtype(dtype).itemsize


def mm(a, b, *, ta=False, tb=False, out_dtype=F32, res=None, b_col0=None, name, tm=1024, tn=1024, tk=512):
    if ta:
        K, M = a.shape
    else:
        M, K = a.shape
    if tb:
        N, Kb = b.shape
    else:
        Kb, N = b.shape
    if b_col0 is None:
        assert K == Kb, (a.shape, b.shape, ta, tb)
    tm, tn, tk = _tile(M, tm), _tile(N, tn), _tile(K, tk)
    nk = K // tk
    koff = 0
    if b_col0 is not None:
        assert tb and b_col0 % tk == 0 and b_col0 + K <= Kb
        koff = b_col0 // tk
    dims = (((0,) if ta else (1,), (1,) if tb else (0,)), ((), ()))

    def body(*refs):
        if res is None:
            a_ref, b_ref, o_ref, acc = refs
            r_ref = None
        else:
            a_ref, b_ref, r_ref, o_ref, acc = refs
        k = pl.program_id(2)

        @pl.when(k == 0)
        def _():
            acc[...] = jnp.zeros_like(acc)

        acc[...] += lax.dot_general(a_ref[...].astype(BF16), b_ref[...].astype(BF16), dims,
                                    preferred_element_type=F32)

        @pl.when(k == nk - 1)
        def _():
            r = acc[...]
            if r_ref is not None:
                r = r + r_ref[...].astype(F32)
            o_ref[...] = r.astype(out_dtype)

    a_spec = pl.BlockSpec((tk, tm), lambda i, j, k: (k, i)) if ta else pl.BlockSpec((tm, tk), lambda i, j, k: (i, k))
    b_spec = pl.BlockSpec((tn, tk), lambda i, j, k: (j, k + koff)) if tb else pl.BlockSpec((tk, tn), lambda i, j, k: (k, j))
    in_specs = [a_spec, b_spec]
    args = [a, b]
    blk = _nbytes((tm, tk), a.dtype) + _nbytes((tk, tn), b.dtype) + _nbytes((tm, tn), out_dtype)
    if res is not None:
        in_specs.append(pl.BlockSpec((tm, tn), lambda i, j, k: (i, j)))
        args.append(res)
        blk += _nbytes((tm, tn), res.dtype)
    return pl.pallas_call(
        body, grid=(M // tm, N // tn, nk), in_specs=in_specs,
        out_specs=pl.BlockSpec((tm, tn), lambda i, j, k: (i, j)),
        out_shape=jax.ShapeDtypeStruct((M, N), out_dtype),
        scratch_shapes=[pltpu.VMEM((tm, tn), F32)], name=name,
        compiler_params=_params(("parallel", "parallel", "arbitrary"), blk + _nbytes((tm, tn), F32)),
    )(*args)


def bdmm(a, w, *, out_dtype, name, tm=512):
    M = a.shape[0]
    nb, ka, kn = w.shape
    tm = _tile(M, tm)

    def body(a_ref, w_ref, o_ref):
        o_ref[...] = jnp.dot(a_ref[...].astype(BF16), w_ref[...].astype(BF16),
                             preferred_element_type=F32).astype(out_dtype)

    blk = _nbytes((tm, ka), a.dtype) + _nbytes((ka, kn), w.dtype) + _nbytes((tm, kn), out_dtype)
    return pl.pallas_call(
        body, grid=(M // tm, nb),
        in_specs=[pl.BlockSpec((tm, ka), lambda i, b: (i, b)), pl.BlockSpec((None, ka, kn), lambda i, b: (b, 0, 0))],
        out_specs=pl.BlockSpec((tm, kn), lambda i, b: (i, b)),
        out_shape=jax.ShapeDtypeStruct((M, nb * kn), out_dtype), name=name,
        compiler_params=_params(("parallel", "parallel"), blk),
    )(a, w)


def bdmm_tn(a, d, *, ka, kd, name, tm=512):
    M = a.shape[0]
    nb = a.shape[1] // ka
    assert d.shape[1] == nb * kd
    tm = _tile(M, tm)

    def body(a_ref, d_ref, o_ref):
        i = pl.program_id(1)
        v = lax.dot_general(a_ref[...].astype(BF16), d_ref[...].astype(BF16), (((0,), (0,)), ((), ())),
                            preferred_element_type=F32)

        @pl.when(i == 0)
        def _():
            o_ref[...] = v

        @pl.when(i > 0)
        def _():
            o_ref[...] += v

    blk = _nbytes((tm, ka), a.dtype) + _nbytes((tm, kd), d.dtype) + _nbytes((ka, kd), F32)
    return pl.pallas_call(
        body, grid=(nb, M // tm),
        in_specs=[pl.BlockSpec((tm, ka), lambda b, i: (i, b)), pl.BlockSpec((tm, kd), lambda b, i: (i, b))],
        out_specs=pl.BlockSpec((None, ka, kd), lambda b, i: (b, 0, 0)),
        out_shape=jax.ShapeDtypeStruct((nb, ka, kd), F32), name=name,
        compiler_params=_params(("parallel", "arbitrary"), blk),
    )(a, d)


def ew(fn, ins, outs, *, rows, tr, name):
    n = rows // tr
    assert n * tr == rows
    in_specs, args, blk = [], [], 0
    for a, kind in ins:
        if kind == "tile":
            assert a.shape[0] == rows, (name, a.shape, rows)
            in_specs.append(pl.BlockSpec((tr, a.shape[1]), lambda i: (i, 0)))
            blk += _nbytes((tr, a.shape[1]), a.dtype)
        else:
            in_specs.append(pl.BlockSpec(a.shape, lambda i, nd=a.ndim: (0,) * nd))
            blk += _nbytes(a.shape, a.dtype)
        args.append(a)
    out_shape, out_specs = [], []
    for c, dt, kind in outs:
        if kind == "tile":
            out_shape.append(jax.ShapeDtypeStruct((rows, c), dt))
            out_specs.append(pl.BlockSpec((tr, c), lambda i: (i, 0)))
            blk += _nbytes((tr, c), dt)
        else:
            out_shape.append(jax.ShapeDtypeStruct((1, c), dt))
            out_specs.append(pl.BlockSpec((1, c), lambda i: (0, 0)))
    nin = len(ins)

    def body(*refs):
        i = pl.program_id(0)
        vals = fn(i, *[r[...] for r in refs[:nin]])
        for (c, dt, kind), o, v in zip(outs, refs[nin:], vals):
            if kind == "tile":
                o[...] = v.astype(dt)
            else:
                @pl.when(i == 0)
                def _():
                    o[...] = v.astype(dt)

                @pl.when(i > 0)
                def _():
                    o[...] += v.astype(dt)

    has_acc = any(k == "acc" for _, _, k in outs)
    return pl.pallas_call(
        body, grid=(n,), in_specs=in_specs, out_specs=out_specs, out_shape=out_shape, name=name,
        compiler_params=_params(("arbitrary" if has_acc else "parallel",), 3 * blk),
    )(*args)


def _colsum(x):
    return jnp.sum(x, axis=0, keepdims=True)


def rms_fwd(x, g, *, out_dtype, name, tr=256):
    def fn(i, xv, gv):
        r = lax.rsqrt(jnp.mean(xv * xv, axis=-1, keepdims=True) + EPS)
        return [xv * r * gv]

    return ew(fn, [(x, "tile"), (g, "full")], [(x.shape[1], out_dtype, "tile")], rows=x.shape[0], tr=tr, name=name)[0]


def rms_bwd(x, g, dy, dres, *, name, tr=256):
    def fn(i, xv, gv, dyv, drv):
        dyv = dyv.astype(F32)
        r = lax.rsqrt(jnp.mean(xv * xv, axis=-1, keepdims=True) + EPS)
        xh = xv * r
        gy = dyv * gv
        dx = r * (gy - xh * jnp.mean(gy * xh, axis=-1, keepdims=True))
        return [dx + drv, _colsum(dyv * xh)]

    D = x.shape[1]
    return ew(fn, [(x, "tile"), (g, "full"), (dy, "tile"), (dres, "tile")], [(D, F32, "tile"), (D, F32, "acc")],
              rows=x.shape[0], tr=tr, name=name)


def rms_bwd_g(x, dy, *, name, tr=256):
    def fn(i, xv, dyv):
        r = lax.rsqrt(jnp.mean(xv * xv, axis=-1, keepdims=True) + EPS)
        return [_colsum(dyv.astype(F32) * xv * r)]

    return ew(fn, [(x, "tile"), (dy, "tile")], [(x.shape[1], F32, "acc")], rows=x.shape[0],
              tr=_tile(x.shape[0], tr, SUBLANE), name=name)[0]


def loss_head(h, g, target, *, name, tr=256):
    D = h.shape[1]

    def fn(i, xv, gv, tv):
        r = lax.rsqrt(jnp.mean(xv * xv, axis=-1, keepdims=True) + EPS)
        xh = xv * r
        err = xh * gv - tv
        dy = err * (1.0 / D)
        gy = dy * gv
        dx = r * (gy - xh * jnp.mean(gy * xh, axis=-1, keepdims=True))
        part = _colsum(err * err) * (0.5 / D)
        return [dx, _colsum(dy * xh), jnp.sum(part, axis=1, keepdims=True)]

    return ew(fn, [(h, "tile"), (g, "full"), (target, "tile")], [(D, F32, "tile"), (D, F32, "acc"), (1, F32, "acc")],
              rows=h.shape[0], tr=tr, name=name)


def _as2d(a):
    if a.ndim >= 2 and a.shape[-1] >= LANE:
        return a.reshape(-1, a.shape[-1])
    if a.size % (8 * LANE) == 0:
        return a.reshape(-1, 8 * LANE)
    return a.reshape(1, -1)


def adamw(w, g, m, v, *, name):
    shape = w.shape
    w2, g2, m2, v2 = (_as2d(t) for t in (w, g.astype(F32).reshape(shape), m, v))
    R, C = w2.shape
    tr = R
    if R * C * 4 > (1 << 20):
        tr = _tile(R, max(SUBLANE, (1 << 20) // (C * 4) // SUBLANE * SUBLANE), SUBLANE)
    c1 = 1.0 / (1.0 - ADAM_B1 ** ADAM_STEP)
    c2 = 1.0 / (1.0 - ADAM_B2 ** ADAM_STEP)

    def fn(i, wv, gv, mv, vv):
        mn = ADAM_B1 * mv + (1.0 - ADAM_B1) * gv
        vn = ADAM_B2 * vv + (1.0 - ADAM_B2) * (gv * gv)
        delta = -ADAM_LR * ((mn * c1) / (jnp.sqrt(vn * c2) + ADAM_EPS) + ADAM_WD * wv)
        return [delta, mn, vn]

    d, mn, vn = ew(fn, [(w2, "tile"), (g2, "tile"), (m2, "tile"), (v2, "tile")], [(C, F32, "tile")] * 3,
                   rows=R, tr=tr, name=name)
    return d.reshape(shape), mn.reshape(shape), vn.reshape(shape)


def _split_bf16(x):
    hi = x.astype(BF16)
    return hi, (x - hi.astype(F32)).astype(BF16)


def _dot2(band, x):
    hi, lo = _split_bf16(x)
    return jnp.dot(band, hi, preferred_element_type=F32) + jnp.dot(band, lo, preferred_element_type=F32)


def _pool_fwd_window(xm, xh, r0, win):
    T = xm.shape[0]
    t = r0 + lax.broadcasted_iota(jnp.int32, (T, 1), 0)
    s_main = r0 + lax.broadcasted_iota(jnp.int32, (1, T), 1)
    s_halo = r0 - POOL_HALO + lax.broadcasted_iota(jnp.int32, (1, POOL_HALO), 1)
    band_m = ((s_main <= t) & (s_main > t - win)).astype(BF16)
    band_h = ((s_halo > t - win) & (s_halo >= 0)).astype(BF16)
    ws = _dot2(band_m, xm) + _dot2(band_h, xh)
    cnt = jnp.minimum(t + 1, win).astype(F32)
    return ws / cnt - xm


def _pool_bwd_window(dm, dh, r0, win, S):
    T = dm.shape[0]
    s = r0 + lax.broadcasted_iota(jnp.int32, (T, 1), 0)
    t_main = r0 + lax.broadcasted_iota(jnp.int32, (1, T), 1)
    t_halo = r0 + T + lax.broadcasted_iota(jnp.int32, (1, POOL_HALO), 1)
    band_m = ((t_main >= s) & (t_main < s + win)).astype(BF16)
    band_h = ((t_halo < s + win) & (t_halo < S)).astype(BF16)
    tm_col = r0 + lax.broadcasted_iota(jnp.int32, (T, 1), 0)
    th_col = r0 + T + lax.broadcasted_iota(jnp.int32, (POOL_HALO, 1), 0)
    dmc = dm / jnp.minimum(tm_col + 1, win).astype(F32)
    dhc = dh / jnp.minimum(th_col + 1, win).astype(F32)
    return _dot2(band_m, dmc) + _dot2(band_h, dhc) - dm


def _pool_specs(T, Cg, order):
    per = T // POOL_HALO
    if order == "ig":
        return (pl.BlockSpec((T, Cg), lambda i, g: (i, g)),
                pl.BlockSpec((POOL_HALO, Cg), lambda i, g: (jnp.maximum(i * per - 1, 0), g)))
    return (pl.BlockSpec((T, Cg), lambda g, i: (i, g)),
            pl.BlockSpec((POOL_HALO, Cg), lambda g, i: (jnp.maximum(i * per - 1, 0), g)))


def pool_fwd(hn, h, w, scale, *, name):
    S, D = hn.shape
    G, Cg, _ = w.shape
    T = _tile(S, POOL_TILE)

    def body(xm_ref, xh_ref, h_ref, w_ref, sc_ref, o_ref):
        i, g = pl.program_id(0), pl.program_id(1)
        win = jnp.left_shift(2, g)
        p = _pool_fwd_window(xm_ref[...], xh_ref[...], i * T, win)
        y = jnp.dot(p.astype(BF16), w_ref[...], preferred_element_type=F32)
        o_ref[...] = h_ref[...] + y * sc_ref[...]

    main, halo = _pool_specs(T, Cg, "ig")
    return pl.pallas_call(
        body, grid=(S // T, G),
        in_specs=[main, halo, main, pl.BlockSpec((None, Cg, Cg), lambda i, g: (g, 0, 0)),
                  pl.BlockSpec((1, Cg), lambda i, g: (0, g))],
        out_specs=main, out_shape=jax.ShapeDtypeStruct((S, D), F32), name=name,
        compiler_params=_params(("parallel", "parallel"), 4 * T * Cg * 4),
    )(hn, hn, h, w, scale)


def pool_bwd_w(hn, dt, w, scale, *, name):
    S, D = hn.shape
    G, Cg, _ = w.shape
    T = _tile(S, POOL_TILE)

    def body(xm_ref, xh_ref, dt_ref, w_ref, sc_ref, dp_ref, dw_ref, ds_ref):
        g, i = pl.program_id(0), pl.program_id(1)
        win = jnp.left_shift(2, g)
        p = _pool_fwd_window(xm_ref[...], xh_ref[...], i * T, win).astype(BF16)
        dtv = dt_ref[...]
        ypre = jnp.dot(p, w_ref[...], preferred_element_type=F32)
        dy = (dtv * sc_ref[...]).astype(BF16)
        dp_ref[...] = lax.dot_general(dy, w_ref[...], (((1,), (1,)), ((), ())), preferred_element_type=F32)
        dwv = lax.dot_general(p, dy, (((0,), (0,)), ((), ())), preferred_element_type=F32)
        dsv = _colsum(dtv * ypre)

        @pl.when(i == 0)
        def _():
            dw_ref[...] = dwv
            ds_ref[...] = dsv

        @pl.when(i > 0)
        def _():
            dw_ref[...] += dwv
            ds_ref[...] += dsv

    main, halo = _pool_specs(T, Cg, "gi")
    return pl.pallas_call(
        body, grid=(G, S // T),
        in_specs=[main, halo, main, pl.BlockSpec((None, Cg, Cg), lambda g, i: (g, 0, 0)),
                  pl.BlockSpec((1, Cg), lambda g, i: (0, g))],
        out_specs=[main, pl.BlockSpec((None, Cg, Cg), lambda g, i: (g, 0, 0)), pl.BlockSpec((1, Cg), lambda g, i: (0, g))],
        out_shape=[jax.ShapeDtypeStruct((S, D), F32), jax.ShapeDtypeStruct((G, Cg, Cg), F32),
                   jax.ShapeDtypeStruct((1, D), F32)], name=name,
        compiler_params=_params(("parallel", "arbitrary"), 4 * T * Cg * 4),
    )(hn, hn, dt, w, scale)


def pool_bwd_x(dp, G, *, name):
    S, D = dp.shape
    Cg = D // G
    T = _tile(S, POOL_TILE)
    per = T // POOL_HALO
    last = S // POOL_HALO - 1

    def body(dm_ref, dh_ref, o_ref):
        i, g = pl.program_id(0), pl.program_id(1)
        o_ref[...] = _pool_bwd_window(dm_ref[...], dh_ref[...], i * T, jnp.left_shift(2, g), S)

    main = pl.BlockSpec((T, Cg), lambda i, g: (i, g))
    return pl.pallas_call(
        body, grid=(S // T, G),
        in_specs=[main, pl.BlockSpec((POOL_HALO, Cg), lambda i, g: (jnp.minimum((i + 1) * per, last), g))],
        out_specs=main, out_shape=jax.ShapeDtypeStruct((S, D), F32), name=name,
        compiler_params=_params(("parallel", "parallel"), 3 * T * Cg * 4),
    )(dp, dp)


def _sb_logs(z, mask):
    e = jnp.exp(-jnp.abs(z))
    sp = jnp.log(1.0 + e)
    ls = jnp.minimum(z, 0.0) - sp
    lsn = jnp.where(mask, jnp.minimum(-z, 0.0) - sp, 0.0)
    return ls, lsn, e


def _dot2r(x, band):
    hi, lo = _split_bf16(x)
    return jnp.dot(hi, band, preferred_element_type=F32) + jnp.dot(lo, band, preferred_element_type=F32)


def _dot3r(x, band):
    hi = x.astype(BF16)
    r1 = x - hi.astype(F32)
    mid = r1.astype(BF16)
    lo = (r1 - mid.astype(F32)).astype(BF16)
    return (jnp.dot(hi, band, preferred_element_type=F32) + jnp.dot(mid, band, preferred_element_type=F32)
            + jnp.dot(lo, band, preferred_element_type=F32))


def _head_masks(n_lanes):
    lane = lax.broadcasted_iota(jnp.int32, (1, n_lanes), 1)
    return [((lane >= h * SB_HEAD_DIM) & (lane < (h + 1) * SB_HEAD_DIM)) for h in range(n_lanes // SB_HEAD_DIM)]


_NT = (((1,), (1,)), ((), ()))
_TN = (((0,), (0,)), ((), ()))


SB_UNROLL = 4


def _sb_unroll(S):
    return SB_UNROLL if S % (SB_UNROLL * SB_BLOCK) == 0 else 1


def sb_fwd(qkv, *, name):
    S, D3 = qkv.shape
    D = D3 // 3
    B = SB_BLOCK
    npair = D // LANE
    scale = SB_HEAD_DIM ** -0.5

    U = _sb_unroll(S)

    def body(q_ref, k_ref, v_ref, o_ref):
        i = pl.program_id(1)
        masks = _head_masks(LANE)
        q = q_ref[...] * scale
        qh = [jnp.where(m, q, jnp.zeros_like(q)) for m in masks]
        row = lax.broadcasted_iota(jnp.int32, (B, B), 0)
        col = lax.broadcasted_iota(jnp.int32, (B, B), 1)
        upper = (row > col).astype(BF16)
        diag = col < row
        nsuper = i // U + 1

        def step(n, carry):
            os_, rs = list(carry[0]), list(carry[1])
            sup = nsuper - 1 - n
            base = pl.multiple_of(sup * (U * B), U * B)
            kbig = k_ref[pl.ds(base, U * B), :]
            vbig = v_ref[pl.ds(base, U * B), :]
            tiles = [(c, hd) for c in reversed(range(U)) for hd in range(len(masks))]
            kb = {c: kbig[c * B:(c + 1) * B] for c in range(U)}
            vb = {c: vbig[c * B:(c + 1) * B] for c in range(U)}
            mask = {c: jnp.logical_or(sup * U + c < i, jnp.logical_and(sup * U + c == i, diag)) for c in range(U)}
            z = {t: lax.dot_general(qh[t[1]], kb[t[0]], _NT, preferred_element_type=F32) for t in tiles}
            ls, lsn = {}, {}
            for t in tiles:
                ls[t], lsn[t], _ = _sb_logs(z[t], mask[t[0]])
            local = {t: _dot2r(lsn[t], upper) for t in tiles}
            for c, hd in tiles:
                a = jnp.where(mask[c], jnp.exp(ls[(c, hd)] + local[(c, hd)] + rs[hd]), 0.0)
                os_[hd] = os_[hd] + jnp.dot(a.astype(BF16), vb[c], preferred_element_type=F32)
                rs[hd] = rs[hd] + jnp.sum(lsn[(c, hd)], axis=1, keepdims=True)
            return tuple(os_), tuple(rs)

        zero = jnp.zeros((B, 1), F32)
        zacc = jnp.zeros((B, LANE), F32)
        os_, _ = lax.fori_loop(0, nsuper, step, (tuple(zacc for _ in masks), tuple(zero for _ in masks)))
        o = jnp.zeros((B, LANE), F32)
        for m, oh in zip(masks, os_):
            o = jnp.where(m, oh, o)
        o_ref[...] = o

    return pl.pallas_call(
        body, grid=(npair, S // B),
        in_specs=[pl.BlockSpec((B, LANE), lambda p, i: (i, p)),
                  pl.BlockSpec((S, LANE), lambda p, i: (0, npair + p)),
                  pl.BlockSpec((S, LANE), lambda p, i: (0, 2 * npair + p))],
        out_specs=pl.BlockSpec((B, LANE), lambda p, i: (i, p)),
        out_shape=jax.ShapeDtypeStruct((S, D), F32), name=name,
        compiler_params=_params(("parallel", "arbitrary"), 2 * S * LANE * 2),
    )(qkv, qkv, qkv)


def sb_bwd(qkv, o, do, *, name):
    S, D3 = qkv.shape
    D = D3 // 3
    B = SB_BLOCK
    npair = D // LANE
    scale = SB_HEAD_DIM ** -0.5

    U = _sb_unroll(S)

    def body(q_ref, k_ref, v_ref, o_ref, do_ref, dq_ref, dk_ref, dv_ref):
        i = pl.program_id(1)

        @pl.when(i == 0)
        def _():
            dk_ref[...] = jnp.zeros_like(dk_ref)
            dv_ref[...] = jnp.zeros_like(dv_ref)

        masks = _head_masks(LANE)
        q = q_ref[...] * scale
        dov = do_ref[...]
        ov = o_ref[...]
        qh = [jnp.where(m, q, jnp.zeros_like(q)) for m in masks]
        doh = [jnp.where(m, dov, 0.0).astype(BF16) for m in masks]
        gsum = [jnp.sum(dh_.astype(F32) * ov, axis=1, keepdims=True) for dh_ in doh]
        row = lax.broadcasted_iota(jnp.int32, (B, B), 0)
        col = lax.broadcasted_iota(jnp.int32, (B, B), 1)
        upper = (row > col).astype(BF16)
        upper_incl = (row >= col).astype(BF16)
        diag = col < row
        nsuper = i // U + 1

        def step(n, carry):
            dqs, rs, gs = list(carry[0]), list(carry[1]), list(carry[2])
            sup = nsuper - 1 - n
            base = pl.multiple_of(sup * (U * B), U * B)
            kbig = k_ref[pl.ds(base, U * B), :]
            vbig = v_ref[pl.ds(base, U * B), :]
            nh = len(masks)
            tiles = [(c, hd) for c in reversed(range(U)) for hd in range(nh)]
            kb = {c: kbig[c * B:(c + 1) * B] for c in range(U)}
            vb = {c: vbig[c * B:(c + 1) * B] for c in range(U)}
            mask = {c: jnp.logical_or(sup * U + c < i, jnp.logical_and(sup * U + c == i, diag)) for c in range(U)}
            z = {t: lax.dot_general(qh[t[1]], kb[t[0]], _NT, preferred_element_type=F32) for t in tiles}
            da = {t: lax.dot_general(doh[t[1]], vb[t[0]], _NT, preferred_element_type=F32) for t in tiles}
            ls, lsn, sig = {}, {}, {}
            for t in tiles:
                ls[t], lsn[t], e = _sb_logs(z[t], mask[t[0]])
                sig[t] = jnp.where(z[t] >= 0.0, 1.0, e) / (1.0 + e)
            local = {t: _dot2r(lsn[t], upper) for t in tiles}
            ab, g = {}, {}
            for c, hd in tiles:
                a = jnp.where(mask[c], jnp.exp(ls[(c, hd)] + local[(c, hd)] + rs[hd]), 0.0)
                ab[(c, hd)] = a.astype(BF16)
                g[(c, hd)] = ab[(c, hd)].astype(F32) * da[(c, hd)]
                rs[hd] = rs[hd] + jnp.sum(lsn[(c, hd)], axis=1, keepdims=True)
            glocal = {t: _dot3r(g[t], upper_incl) for t in tiles}
            dzb = {}
            for c, hd in tiles:
                t = (c, hd)
                sg = glocal[t] + gs[hd]
                dzb[t] = jnp.where(mask[c], g[t] * (1.0 - sig[t]) - (gsum[hd] - sg) * sig[t], 0.0).astype(BF16)
                gs[hd] = gs[hd] + jnp.sum(g[t], axis=1, keepdims=True)
            for c, hd in tiles:
                dqs[hd] = dqs[hd] + jnp.dot(dzb[(c, hd)], kb[c], preferred_element_type=F32)
            for c in reversed(range(U)):
                dkb = sum(lax.dot_general(dzb[(c, hd)], qh[hd], _TN, preferred_element_type=F32) for hd in range(nh))
                dvb = sum(lax.dot_general(ab[(c, hd)], doh[hd], _TN, preferred_element_type=F32) for hd in range(nh))
                off = pl.multiple_of(base + c * B, B)
                dk_ref[pl.ds(off, B), :] += dkb
                dv_ref[pl.ds(off, B), :] += dvb
            return tuple(dqs), tuple(rs), tuple(gs)

        zero = jnp.zeros((B, 1), F32)
        zs = tuple(zero for _ in masks)
        zacc = jnp.zeros((B, LANE), F32)
        dqs, _, _ = lax.fori_loop(0, nsuper, step, (tuple(zacc for _ in masks), zs, zs))
        dq = jnp.zeros((B, LANE), F32)
        for m, dqh in zip(masks, dqs):
            dq = jnp.where(m, dqh, dq)
        dq_ref[...] = dq * scale

    tile = pl.BlockSpec((B, LANE), lambda p, i: (i, p))
    strip = pl.BlockSpec((S, LANE), lambda p, i: (0, p))
    return pl.pallas_call(
        body, grid=(npair, S // B),
        in_specs=[tile, pl.BlockSpec((S, LANE), lambda p, i: (0, npair + p)),
                  pl.BlockSpec((S, LANE), lambda p, i: (0, 2 * npair + p)), tile, tile],
        out_specs=[tile, strip, strip],
        out_shape=[jax.ShapeDtypeStruct((S, D), F32)] * 3, name=name,
        compiler_params=_params(("parallel", "arbitrary"), 2 * S * LANE * 2 + 2 * S * LANE * 4),
    )(qkv, qkv, qkv, o, do)


S5_HALF = S5_GROUPS_PER_BLOCK * S5_STATE
S5_BLOCK = 2 * S5_HALF


def s5_scan(bu, pw, *, reverse, name, tr=512):
    S, W = bu.shape
    nb = W // S5_BLOCK
    tr = _tile(S, tr, SUBLANE)
    nsub = tr // SUBLANE
    nt = S // tr
    H = S5_HALF

    def body(bu_ref, pw_ref, x_ref, st_re, st_im):
        i = pl.program_id(1)

        @pl.when(i == 0)
        def _():
            st_re[...] = jnp.zeros_like(st_re)
            st_im[...] = jnp.zeros_like(st_im)

        row = lax.broadcasted_iota(jnp.int32, (SUBLANE, H), 0)
        steps = []
        for k, sh in enumerate((1, 2, 4)):
            valid = (row < SUBLANE - sh) if reverse else (row >= sh)
            steps.append((sh, valid, pw_ref[SUBLANE + k:SUBLANE + k + 1, 0:H], pw_ref[SUBLANE + k:SUBLANE + k + 1, H:2 * H]))
        ap_re = pw_ref[0:SUBLANE, 0:H]
        ap_im = pw_ref[0:SUBLANE, H:2 * H]
        edge = (row == 0) if reverse else (row == SUBLANE - 1)

        def sub(n, carry):
            s_re, s_im = carry
            j = (nsub - 1 - n) if reverse else n
            off = pl.multiple_of(j * SUBLANE, SUBLANE)
            r = bu_ref[pl.ds(off, SUBLANE), 0:H]
            m = bu_ref[pl.ds(off, SUBLANE), H:2 * H]
            for sh, valid, a_re, a_im in steps:
                amt = (SUBLANE - sh) if reverse else sh
                rs = jnp.where(valid, pltpu.roll(r, amt, 0), 0.0)
                ms = jnp.where(valid, pltpu.roll(m, amt, 0), 0.0)
                r, m = r + a_re * rs - a_im * ms, m + a_re * ms + a_im * rs
            r, m = r + ap_re * s_re - ap_im * s_im, m + ap_re * s_im + ap_im * s_re
            x_ref[pl.ds(off, SUBLANE), 0:H] = r
            x_ref[pl.ds(off, SUBLANE), H:2 * H] = m
            return (jnp.sum(jnp.where(edge, r, 0.0), axis=0, keepdims=True),
                    jnp.sum(jnp.where(edge, m, 0.0), axis=0, keepdims=True))

        s_re, s_im = lax.fori_loop(0, nsub, sub, (st_re[...], st_im[...]))
        st_re[...] = s_re
        st_im[...] = s_im

    if reverse:
        tile = pl.BlockSpec((tr, S5_BLOCK), lambda b, i: (nt - 1 - i, b))
    else:
        tile = pl.BlockSpec((tr, S5_BLOCK), lambda b, i: (i, b))
    return pl.pallas_call(
        body, grid=(nb, nt),
        in_specs=[tile, pl.BlockSpec((2 * SUBLANE, S5_BLOCK), lambda b, i: (0, b))],
        out_specs=tile, out_shape=jax.ShapeDtypeStruct((S, W), F32),
        scratch_shapes=[pltpu.VMEM((1, H), F32), pltpu.VMEM((1, H), F32)], name=name,
        compiler_params=_params(("parallel", "arbitrary"), 2 * tr * S5_BLOCK * 4),
    )(bu, pw)


def s5_da(lam, x, *, name, tr=512):
    S, W = lam.shape
    nb = W // S5_BLOCK
    tr = _tile(S, tr, SUBLANE)
    nsub = tr // SUBLANE
    nt = S // tr
    H = S5_HALF

    def body(l_ref, x_ref, xh_ref, o_ref, acc_re, acc_im):
        i = pl.program_id(1)

        @pl.when(i == 0)
        def _():
            acc_re[...] = jnp.zeros_like(acc_re)
            acc_im[...] = jnp.zeros_like(acc_im)

        row = lax.broadcasted_iota(jnp.int32, (SUBLANE, H), 0)
        first = row == 0

        def sub(n, carry):
            a_re, a_im = carry
            off = pl.multiple_of(n * SUBLANE, SUBLANE)
            poff = pl.multiple_of(jnp.maximum(n - 1, 0) * SUBLANE, SUBLANE)
            inside = n > 0
            start = jnp.logical_and(i == 0, n == 0)
            out = []
            for lo in (0, H):
                cur = x_ref[pl.ds(off, SUBLANE), lo:lo + H]
                prv = jnp.where(inside, x_ref[pl.ds(poff, SUBLANE), lo:lo + H], xh_ref[:, lo:lo + H])
                xs = jnp.where(first, pltpu.roll(prv, 1, 0), pltpu.roll(cur, 1, 0))
                out.append(jnp.where(jnp.logical_and(start, first), 0.0, xs))
            xs_re, xs_im = out
            l_re = l_ref[pl.ds(off, SUBLANE), 0:H]
            l_im = l_ref[pl.ds(off, SUBLANE), H:2 * H]
            return a_re + l_re * xs_re + l_im * xs_im, a_im + l_im * xs_re - l_re * xs_im

        a_re, a_im = lax.fori_loop(0, nsub, sub, (acc_re[...], acc_im[...]))
        acc_re[...] = a_re
        acc_im[...] = a_im

        @pl.when(i == nt - 1)
        def _():
            o_ref[:, 0:H] = jnp.sum(a_re, axis=0, keepdims=True)
            o_ref[:, H:2 * H] = jnp.sum(a_im, axis=0, keepdims=True)

    per = tr // SUBLANE
    tile = pl.BlockSpec((tr, S5_BLOCK), lambda b, i: (i, b))
    return pl.pallas_call(
        body, grid=(nb, nt),
        in_specs=[tile, tile, pl.BlockSpec((SUBLANE, S5_BLOCK), lambda b, i: (jnp.maximum(i * per - 1, 0), b))],
        out_specs=pl.BlockSpec((1, S5_BLOCK), lambda b, i: (0, b)),
        out_shape=jax.ShapeDtypeStruct((1, W), F32),
        scratch_shapes=[pltpu.VMEM((SUBLANE, H), F32), pltpu.VMEM((SUBLANE, H), F32)], name=name,
        compiler_params=_params(("parallel", "arbitrary"), 2 * tr * S5_BLOCK * 4),
    )(lam, x, x)


def _gelu(y):
    c = math.sqrt(2.0 / math.pi)
    return 0.5 * y * (1.0 + jnp.tanh(c * (y + 0.044715 * y * y * y)))


def _gelu_grad(y):
    c = math.sqrt(2.0 / math.pi)
    th = jnp.tanh(c * (y + 0.044715 * y * y * y))
    return 0.5 * (1.0 + th) + 0.5 * y * (1.0 - th * th) * c * (1.0 + 3.0 * 0.044715 * y * y)


def _sigmoid(x):
    e = jnp.exp(-jnp.abs(x))
    return jnp.where(x >= 0.0, 1.0, e) / (1.0 + e)


def _xa_probs(qh, kh, scale):
    s = lax.dot_general(qh, kh, _NT, preferred_element_type=F32) * scale
    p = jnp.exp(s - jnp.max(s, axis=-1, keepdims=True))
    return p / jnp.sum(p, axis=-1, keepdims=True)


def xa_fwd(q, kv, *, name, tm=512):
    S, D = q.shape
    M = kv.shape[0]
    dh = D // XA_HEADS
    scale = dh ** -0.5
    tm = _tile(S, tm)

    def body(q_ref, kv_ref, o_ref):
        for h in range(XA_HEADS):
            p = _xa_probs(q_ref[:, h * dh:(h + 1) * dh], kv_ref[:, h * dh:(h + 1) * dh], scale)
            o_ref[:, h * dh:(h + 1) * dh] = jnp.dot(p.astype(BF16), kv_ref[:, D + h * dh:D + (h + 1) * dh],
                                                   preferred_element_type=F32).astype(BF16)

    return pl.pallas_call(
        body, grid=(S // tm,),
        in_specs=[pl.BlockSpec((tm, D), lambda i: (i, 0)), pl.BlockSpec((M, 2 * D), lambda i: (0, 0))],
        out_specs=pl.BlockSpec((tm, D), lambda i: (i, 0)),
        out_shape=jax.ShapeDtypeStruct((S, D), BF16), name=name,
        compiler_params=_params(("parallel",), 2 * tm * D * 2 + M * 2 * D * 2),
    )(q, kv)


def xa_bwd(q, kv, do, *, name, tm=512):
    S, D = q.shape
    M = kv.shape[0]
    dh = D // XA_HEADS
    scale = dh ** -0.5
    tm = _tile(S, tm)

    def body(q_ref, kv_ref, do_ref, dq_ref, dkv_ref):
        i = pl.program_id(0)

        @pl.when(i == 0)
        def _():
            dkv_ref[...] = jnp.zeros_like(dkv_ref)

        for h in range(XA_HEADS):
            sl = slice(h * dh, (h + 1) * dh)
            vsl = slice(D + h * dh, D + (h + 1) * dh)
            qh, kh, vh = q_ref[:, sl], kv_ref[:, sl], kv_ref[:, vsl]
            doh = do_ref[:, sl].astype(BF16)
            p = _xa_probs(qh, kh, scale)
            dp = lax.dot_general(doh, vh, _NT, preferred_element_type=F32)
            ds = (p * (dp - jnp.sum(dp * p, axis=-1, keepdims=True)) * scale).astype(BF16)
            dq_ref[:, sl] = jnp.dot(ds, kh, preferred_element_type=F32).astype(BF16)
            dkv_ref[:, sl] += lax.dot_general(ds, qh, _TN, preferred_element_type=F32)
            dkv_ref[:, vsl] += lax.dot_general(p.astype(BF16), doh, _TN, preferred_element_type=F32)

    return pl.pallas_call(
        body, grid=(S // tm,),
        in_specs=[pl.BlockSpec((tm, D), lambda i: (i, 0)), pl.BlockSpec((M, 2 * D), lambda i: (0, 0)),
                  pl.BlockSpec((tm, D), lambda i: (i, 0))],
        out_specs=[pl.BlockSpec((tm, D), lambda i: (i, 0)), pl.BlockSpec((M, 2 * D), lambda i: (0, 0))],
        out_shape=[jax.ShapeDtypeStruct((S, D), BF16), jax.ShapeDtypeStruct((M, 2 * D), F32)], name=name,
        compiler_params=_params(("arbitrary",), 3 * tm * D * 2 + M * 2 * D * 6),
    )(q, kv, do)


FFN_STRIP = 256
FFN_ROWS = 512


def _shift_down(x, k):
    row = lax.broadcasted_iota(jnp.int32, x.shape, 0)
    return jnp.where(row >= k, pltpu.roll(x, k, 0), 0.0)


def _shift_up(x, k):
    n = x.shape[0]
    row = lax.broadcasted_iota(jnp.int32, x.shape, 0)
    return jnp.where(row < n - k, pltpu.roll(x, n - k, 0), 0.0)


FFN_HALO = 2 * SUBLANE


def _rows_with_prev(u_ref, r0, R):
    cur = u_ref[pl.ds(r0, R), :].astype(F32)
    p0 = pl.multiple_of(jnp.maximum(r0 - FFN_HALO, 0), FFN_HALO)
    prev = jnp.where(r0 > 0, u_ref[pl.ds(p0, FFN_HALO), :].astype(F32), 0.0)
    return jnp.concatenate([prev, cur], axis=0), cur


def _conv_rows(u_ref, r0, R, w_ref, b_ref):
    ext, _ = _rows_with_prev(u_ref, r0, R)
    out = w_ref[2:3, :] * ext + w_ref[1:2, :] * _shift_down(ext, 1) + w_ref[0:1, :] * _shift_down(ext, 2) + b_ref[...]
    return out[FFN_HALO:, :]


def ffn_act_fwd(u, conv_w, conv_b, *, name):
    S, F2 = u.shape
    F = F2 // 2
    tc = _tile(F, FFN_STRIP)
    nc = F // tc
    R = _tile(S, FFN_ROWS, 16)

    def body(uv_ref, ug_ref, wv_ref, wg_ref, bv_ref, bg_ref, o_ref):
        def rows(n, _):
            r0 = pl.multiple_of(n * R, R)
            val = _conv_rows(uv_ref, r0, R, wv_ref, bv_ref)
            gate = _conv_rows(ug_ref, r0, R, wg_ref, bg_ref)
            o_ref[pl.ds(r0, R), :] = (gate * _sigmoid(gate) * val).astype(BF16)
            return 0

        lax.fori_loop(0, S // R, rows, 0)

    return pl.pallas_call(
        body, grid=(nc,),
        in_specs=[pl.BlockSpec((S, tc), lambda c: (0, c)), pl.BlockSpec((S, tc), lambda c: (0, nc + c)),
                  pl.BlockSpec((CONV_WIDTH, tc), lambda c: (0, c)), pl.BlockSpec((CONV_WIDTH, tc), lambda c: (0, nc + c)),
                  pl.BlockSpec((1, tc), lambda c: (0, c)), pl.BlockSpec((1, tc), lambda c: (0, nc + c))],
        out_specs=pl.BlockSpec((S, tc), lambda c: (0, c)),
        out_shape=jax.ShapeDtypeStruct((S, F), BF16), name=name,
        compiler_params=_params(("parallel",), 3 * S * tc * 2),
    )(u, u, conv_w, conv_w, conv_b, conv_b)


def ffn_act_bwd(u, dact, conv_w, conv_b, *, name):
    S, F2 = u.shape
    F = F2 // 2
    tc = _tile(F, FFN_STRIP)
    nc = F // tc
    R = _tile(S, FFN_ROWS, 16)
    nr = S // R
    HALO = FFN_HALO

    def body(uv_ref, ug_ref, da_ref, wv_ref, wg_ref, bv_ref, bg_ref,
             duv_ref, dug_ref, dwv_ref, dwg_ref, dbv_ref, dbg_ref, dcv, dcg):
        def p1(n, _):
            r0 = pl.multiple_of(n * R, R)
            val = _conv_rows(uv_ref, r0, R, wv_ref, bv_ref)
            gate = _conv_rows(ug_ref, r0, R, wg_ref, bg_ref)
            d = da_ref[pl.ds(r0, R), :].astype(F32)
            sg = _sigmoid(gate)
            dcv[pl.ds(r0, R), :] = d * gate * sg
            dcg[pl.ds(r0, R), :] = d * val * (sg + gate * sg * (1.0 - sg))
            return 0

        lax.fori_loop(0, nr, p1, 0)

        def p2(n, carry):
            r0 = pl.multiple_of(n * R, R)
            nxt = pl.multiple_of(jnp.minimum(r0 + R, S - HALO), HALO)
            new = []
            for u_ref, dc, w_ref, du_ref, acc in ((uv_ref, dcv, wv_ref, duv_ref, carry[0]),
                                                  (ug_ref, dcg, wg_ref, dug_ref, carry[1])):
                d = dc[pl.ds(r0, R), :]
                after = jnp.where(r0 + R < S, dc[pl.ds(nxt, HALO), :], 0.0)
                ext = jnp.concatenate([d, after], axis=0)
                du = w_ref[2:3, :] * ext + w_ref[1:2, :] * _shift_up(ext, 1) + w_ref[0:1, :] * _shift_up(ext, 2)
                du_ref[pl.ds(r0, R), :] = du[:R, :].astype(BF16)
                uext, cur = _rows_with_prev(u_ref, r0, R)
                u1 = _shift_down(uext, 1)[HALO:, :]
                u2 = _shift_down(uext, 2)[HALO:, :]
                dw2, dw1, dw0, db = acc
                new.append((dw2 + _colsum(d * cur), dw1 + _colsum(d * u1), dw0 + _colsum(d * u2), db + _colsum(d)))
            return tuple(new)

        z = jnp.zeros((1, tc), F32)
        accs = lax.fori_loop(0, nr, p2, ((z, z, z, z), (z, z, z, z)))
        for (dw2, dw1, dw0, db), dw_ref, db_ref in ((accs[0], dwv_ref, dbv_ref), (accs[1], dwg_ref, dbg_ref)):
            dw_ref[0:1, :] = dw0
            dw_ref[1:2, :] = dw1
            dw_ref[2:3, :] = dw2
            db_ref[...] = db

    strip_v = pl.BlockSpec((S, tc), lambda c: (0, c))
    strip_g = pl.BlockSpec((S, tc), lambda c: (0, nc + c))
    w_v = pl.BlockSpec((CONV_WIDTH, tc), lambda c: (0, c))
    w_g = pl.BlockSpec((CONV_WIDTH, tc), lambda c: (0, nc + c))
    b_v = pl.BlockSpec((1, tc), lambda c: (0, c))
    b_g = pl.BlockSpec((1, tc), lambda c: (0, nc + c))
    outs = pl.pallas_call(
        body, grid=(nc,),
        in_specs=[strip_v, strip_g, strip_v, w_v, w_g, b_v, b_g],
        out_specs=[strip_v, strip_v, w_v, w_v, b_v, b_v],
        out_shape=[jax.ShapeDtypeStruct((S, F), BF16), jax.ShapeDtypeStruct((S, F), BF16),
                   jax.ShapeDtypeStruct((CONV_WIDTH, F), F32), jax.ShapeDtypeStruct((CONV_WIDTH, F), F32),
                   jax.ShapeDtypeStruct((1, F), F32), jax.ShapeDtypeStruct((1, F), F32)],
        scratch_shapes=[pltpu.VMEM((S, tc), F32), pltpu.VMEM((S, tc), F32)], name=name,
        compiler_params=_params(("parallel",), 5 * S * tc * 2 + S * tc * 4),
    )(u, u, dact, conv_w, conv_w, conv_b, conv_b)
    duv, dug, dwv, dwg, dbv, dbg = outs
    return duv, dug, jnp.concatenate([dwv, dwg], axis=1), jnp.concatenate([dbv, dbg], axis=1)


def _s5_discretize(a_re, a_im, log_dt, b_re, b_im):
    lam = lax.complex(a_re, a_im)
    dt_lam = lam * jnp.exp(log_dt)[:, None]
    a_bar = jnp.exp(dt_lam)
    b_bar = ((a_bar - 1.0) / lam)[..., None] * lax.complex(b_re, b_im)
    return jnp.real(a_bar), jnp.imag(a_bar), jnp.real(b_bar), jnp.imag(b_bar)


def _s5_cols(z_re, z_im, nb):
    lead = z_re.shape[:-2]
    re = z_re.reshape(*lead, nb, S5_HALF)
    im = z_im.reshape(*lead, nb, S5_HALF)
    return jnp.concatenate([re, im], axis=-1).reshape(*lead, nb * S5_BLOCK)


def _s5_powers(a_re, a_im, log_dt, nb, *, reverse):
    dt_lam = lax.complex(a_re, a_im) * jnp.exp(log_dt)[:, None]
    if reverse:
        dt_lam = jnp.conj(dt_lam)
        carry = jnp.arange(SUBLANE, 0, -1, dtype=F32)
    else:
        carry = jnp.arange(1, SUBLANE + 1, dtype=F32)
    ks = jnp.concatenate([carry, jnp.array([1.0, 2.0, 4.0], F32), jnp.zeros((SUBLANE - 3,), F32)])
    pw = jnp.exp(ks[:, None, None] * dt_lam[None])
    return _s5_cols(jnp.real(pw), jnp.imag(pw), nb)


def _s5_in_weights(bb_re, bb_im, nb):
    eye = jnp.eye(S5_GROUPS_PER_BLOCK, dtype=F32)
    G, P, Cg = bb_re.shape

    def one(bb):
        t = jnp.einsum("bgpi,gh->bgihp", bb.reshape(nb, S5_GROUPS_PER_BLOCK, P, Cg), eye)
        return t.reshape(nb, S5_GROUPS_PER_BLOCK * Cg, S5_HALF)

    return jnp.concatenate([one(bb_re), one(bb_im)], axis=2)


def _s5_out_weights(c_re, c_im, nb):
    eye = jnp.eye(S5_GROUPS_PER_BLOCK, dtype=F32)
    G, Cg, P = c_re.shape

    def one(c):
        t = jnp.einsum("bgip,gh->bgphi", c.reshape(nb, S5_GROUPS_PER_BLOCK, Cg, P), eye)
        return t.reshape(nb, S5_HALF, S5_GROUPS_PER_BLOCK * Cg)

    return jnp.concatenate([one(c_re), -one(c_im)], axis=1)


def _s5_in_weight_grads(dwb, Cg):
    nb = dwb.shape[0]
    eye = jnp.eye(S5_GROUPS_PER_BLOCK, dtype=F32)
    t = dwb.reshape(nb, S5_GROUPS_PER_BLOCK, Cg, 2, S5_GROUPS_PER_BLOCK, S5_STATE)
    out = jnp.einsum("bgirhp,gh->rbgpi", t, eye)
    return out[0].reshape(-1, S5_STATE, Cg), out[1].reshape(-1, S5_STATE, Cg)


def _s5_out_weight_grads(dwc, Cg):
    nb = dwc.shape[0]
    eye = jnp.eye(S5_GROUPS_PER_BLOCK, dtype=F32)
    t = dwc.reshape(nb, 2, S5_GROUPS_PER_BLOCK, S5_STATE, S5_GROUPS_PER_BLOCK, Cg)
    out = jnp.einsum("brgphi,gh->rbgip", t, eye)
    return out[0].reshape(-1, Cg, S5_STATE), -out[1].reshape(-1, Cg, S5_STATE)


ANY = pl.BlockSpec(memory_space=pl.ANY)
N_CHIPS = 4
N_DEV = 8


def _place():
    x, y, c = lax.axis_index("x"), lax.axis_index("y"), lax.axis_index("c")
    chips = [(1 - x, y), (x, 1 - y), (1 - x, 1 - y)]
    return x, y, c, chips


def gather_chips(w, *, name):
    R, C = w.shape
    Hh = R // 2
    assert 2 * Hh == R

    def body(w_ref, out_ref, send_sems, recv_sems):
        x, y, c, chips = _place()
        me = 2 * x + y
        sibling = (x, y, 1 - c)

        def half(chip, hc):
            return out_ref.at[chip, pl.ds(hc * Hh, Hh), :]

        def copy(k, src, dst, to):
            return pltpu.make_async_remote_copy(src_ref=src, dst_ref=dst, send_sem=send_sems.at[k],
                                                recv_sem=recv_sems.at[k], device_id=to, device_id_type=MESH)

        first = [copy(j, w_ref.at[pl.ds(c * Hh, Hh), :], half(me, c), (px, py, c)) for j, (px, py) in enumerate(chips)]
        for cp in first:
            cp.start()
        passed = []
        for j, (px, py) in enumerate(chips):
            landed = half(2 * px + py, c)
            copy(j, landed, landed, (px, py, c)).wait_recv()
            fwd = copy(3 + j, landed, landed, sibling)
            fwd.start()
            passed.append(fwd)
        for j, (px, py) in enumerate(chips):
            theirs = half(2 * px + py, 1 - c)
            copy(3 + j, theirs, theirs, sibling).wait_recv()
        for cp in first + passed:
            cp.wait_send()

    others = pl.pallas_call(
        body, in_specs=[ANY], out_specs=ANY, out_shape=jax.ShapeDtypeStruct((N_CHIPS, R, C), w.dtype),
        scratch_shapes=[pltpu.SemaphoreType.DMA((6,)), pltpu.SemaphoreType.DMA((6,))], name=name,
    )(w)
    return _place_rows(others, w, 2 * lax.axis_index("x") + lax.axis_index("y"), name=name + "_own")


def _place_rows(buf, rows, slot, *, name, tr=592):
    n, R, C = buf.shape
    tr = _tile(R, tr, 16)
    idx = jnp.asarray(slot, jnp.int32).reshape(1)

    def body(s_ref, r_ref, b_ref, o_ref):
        o_ref[...] = r_ref[...]

    return pl.pallas_call(
        body,
        grid_spec=pltpu.PrefetchScalarGridSpec(
            num_scalar_prefetch=1, grid=(R // tr,),
            in_specs=[pl.BlockSpec((tr, C), lambda i, s: (i, 0)), ANY],
            out_specs=pl.BlockSpec((None, tr, C), lambda i, s: (s[0], i, 0))),
        out_shape=jax.ShapeDtypeStruct(buf.shape, buf.dtype), input_output_aliases={2: 0}, name=name,
        compiler_params=_params(("parallel",), 2 * tr * C * 4),
    )(idx, rows, buf)


def swap_halves(g4, *, name):
    n, R, C = g4.shape
    Hh = R // 2

    def body(g_ref, out_ref, send_sem, recv_sem):
        x, y, c, _ = _place()
        cp = pltpu.make_async_remote_copy(
            src_ref=g_ref.at[pl.ds(0, n), pl.ds((1 - c) * Hh, Hh), :], dst_ref=out_ref, send_sem=send_sem,
            recv_sem=recv_sem, device_id=(x, y, 1 - c), device_id_type=MESH)
        cp.start()
        cp.wait()

    return pl.pallas_call(
        body, in_specs=[ANY], out_specs=ANY, out_shape=jax.ShapeDtypeStruct((n, Hh, C), g4.dtype),
        scratch_shapes=[pltpu.SemaphoreType.DMA, pltpu.SemaphoreType.DMA], name=name,
    )(g4)


def add_half(g4, other, *, name, tr=160):
    n, R, C = g4.shape
    Hh = R // 2
    tr = _tile(Hh, tr, 16)
    nblk = Hh // tr
    cidx = lax.axis_index("c").astype(jnp.int32).reshape(1)

    def body(c_ref, g_ref, o_ref, out_ref):
        out_ref[...] = (g_ref[...].astype(F32) + o_ref[...].astype(F32)).astype(out_ref.dtype)

    return pl.pallas_call(
        body,
        grid_spec=pltpu.PrefetchScalarGridSpec(
            num_scalar_prefetch=1, grid=(nblk,),
            in_specs=[pl.BlockSpec((n, tr, C), lambda i, c_ref: (0, c_ref[0] * nblk + i, 0)),
                      pl.BlockSpec((n, tr, C), lambda i, c_ref: (0, i, 0))],
            out_specs=pl.BlockSpec((n, tr, C), lambda i, c_ref: (0, i, 0))),
        out_shape=jax.ShapeDtypeStruct((n, Hh, C), g4.dtype), name=name,
        compiler_params=_params(("parallel",), 3 * n * tr * C * 2),
    )(cidx, g4, other)


def scatter_chips(p4, *, name):
    n, Hh, C = p4.shape

    def body(p_ref, out_ref, send_sems, recv_sems):
        x, y, c, chips = _place()
        me = 2 * x + y
        sends = []
        for j, (px, py) in enumerate(chips):
            cp = pltpu.make_async_remote_copy(src_ref=p_ref.at[2 * px + py], dst_ref=out_ref.at[me], send_sem=send_sems.at[j],
                                              recv_sem=recv_sems.at[j], device_id=(px, py, c), device_id_type=MESH)
            cp.start()
            sends.append(cp)
        for j, (px, py) in enumerate(chips):
            slot = out_ref.at[2 * px + py]
            pltpu.make_async_remote_copy(src_ref=slot, dst_ref=slot, send_sem=send_sems.at[j], recv_sem=recv_sems.at[j],
                                         device_id=(px, py, c), device_id_type=MESH).wait_recv()
        for cp in sends:
            cp.wait_send()

    return pl.pallas_call(
        body, in_specs=[ANY], out_specs=ANY, out_shape=jax.ShapeDtypeStruct((n, Hh, C), p4.dtype),
        scratch_shapes=[pltpu.SemaphoreType.DMA((3,)), pltpu.SemaphoreType.DMA((3,))], name=name,
    )(p4)


def sum_chips(landed, part, *, name, tr=96):
    n, Hh, C = landed.shape
    tr = _tile(Hh, tr, 16)
    nblk = Hh // tr
    x, y, c, _ = _place()
    idx = jnp.stack([2 * x + y, c]).astype(jnp.int32)

    def slot_spec(k):
        return pl.BlockSpec((None, tr, C), lambda i, s: (jnp.where(s[0] == k, (k + 1) % n, k), i, 0))

    def body(s_ref, *refs):
        slots, own_ref, o_ref = refs[:n], refs[n], refs[n + 1]
        own = own_ref[...].astype(F32)
        acc = None
        for k in range(n):
            v = jnp.where(s_ref[0] == k, own, slots[k][...].astype(F32))
            acc = v if acc is None else acc + v
        o_ref[...] = acc

    return pl.pallas_call(
        body,
        grid_spec=pltpu.PrefetchScalarGridSpec(
            num_scalar_prefetch=1, grid=(nblk,),
            in_specs=[slot_spec(k) for k in range(n)] + [pl.BlockSpec((None, tr, C), lambda i, s: (s[0], i, 0))],
            out_specs=pl.BlockSpec((tr, C), lambda i, s: (s[1] * nblk + i, 0))),
        out_shape=jax.ShapeDtypeStruct((2 * Hh, C), F32), name=name,
        compiler_params=_params(("parallel",), 6 * tr * C * 4),
    )(idx, *([landed] * n), part)


def sum_leading(x3, *, name, tr=160, align=16):
    n, R, C = x3.shape
    tr = _tile(R, tr, align)

    def body(x_ref, o_ref):
        acc = x_ref[0].astype(F32)
        for k in range(1, n):
            acc = acc + x_ref[k].astype(F32)
        o_ref[...] = acc

    return pl.pallas_call(
        body, grid=(R // tr,), in_specs=[pl.BlockSpec((n, tr, C), lambda i: (0, i, 0))],
        out_specs=pl.BlockSpec((tr, C), lambda i: (i, 0)), out_shape=jax.ShapeDtypeStruct((R, C), F32), name=name,
        compiler_params=_params(("parallel",), n * tr * C * 4 + tr * C * 4),
    )(x3)


def join_halves(r, *, name):
    R, C = r.shape
    Hh = R // 2

    def body(r_ref, out_ref, send_sem, recv_sem):
        x, y, c, _ = _place()
        mine = out_ref.at[pl.ds(c * Hh, Hh), :]
        theirs = out_ref.at[pl.ds((1 - c) * Hh, Hh), :]
        cp = pltpu.make_async_remote_copy(src_ref=mine, dst_ref=mine, send_sem=send_sem, recv_sem=recv_sem,
                                          device_id=(x, y, 1 - c), device_id_type=MESH)
        cp.start()
        pltpu.make_async_remote_copy(src_ref=theirs, dst_ref=theirs, send_sem=send_sem, recv_sem=recv_sem,
                                     device_id=(x, y, 1 - c), device_id_type=MESH).wait_recv()
        cp.wait_send()

    return pl.pallas_call(
        body, in_specs=[ANY], out_specs=ANY, out_shape=jax.ShapeDtypeStruct((R, C), r.dtype),
        input_output_aliases={0: 0}, scratch_shapes=[pltpu.SemaphoreType.DMA, pltpu.SemaphoreType.DMA], name=name,
    )(r)


def gather_devices(v, *, name):
    m_per, n = v.shape

    def body(x_ref, out_ref, send_sems, recv_sems, local_sem):
        x, y, c, chips = _place()
        me, sibling = (x, y, c), (x, y, 1 - c)

        def rows(px, py, pc):
            return out_ref.at[pl.ds((4 * px + 2 * py + pc) * m_per, m_per), :]

        def copy(k, block, to, src=None):
            return pltpu.make_async_remote_copy(src_ref=rows(*block) if src is None else src, dst_ref=rows(*block),
                                                send_sem=send_sems.at[k], recv_sem=recv_sems.at[k], device_id=to,
                                                device_id_type=MESH)

        mine = pltpu.make_async_copy(x_ref, rows(*me), local_sem)
        mine.start()
        first = [copy(0, me, sibling, src=x_ref)]
        first += [copy(1 + j, me, (*chip, c), src=x_ref) for j, chip in enumerate(chips)]
        for cp in first:
            cp.start()
        passed = [copy(4 + j, (*chip, c), sibling) for j, chip in enumerate(chips)]
        for j, chip in enumerate(chips):
            copy(1 + j, (*chip, c), me).wait_recv()
            passed[j].start()
        copy(0, sibling, me).wait_recv()
        for j, chip in enumerate(chips):
            copy(4 + j, (*chip, 1 - c), me).wait_recv()
        for cp in first + passed:
            cp.wait_send()
        mine.wait()

    return pl.pallas_call(
        body, out_shape=jax.ShapeDtypeStruct((N_DEV * m_per, n), v.dtype),
        in_specs=[pl.BlockSpec(memory_space=pltpu.VMEM)], out_specs=pl.BlockSpec(memory_space=pltpu.VMEM),
        scratch_shapes=[pltpu.SemaphoreType.DMA((7,)), pltpu.SemaphoreType.DMA((7,)), pltpu.SemaphoreType.DMA], name=name,
        compiler_params=pltpu.CompilerParams(vmem_limit_bytes=_vmem_limit(9 * m_per * n * 4)),
    )(v)


def reduce_weight_grads(g4):
    other = swap_halves(g4, name="rs_swap_halves")
    part = add_half(g4, other, name="rs_add_half")
    landed = scatter_chips(part, name="rs_scatter_chips")
    mine = sum_chips(landed, part, name="rs_sum_chips")
    return join_halves(mine, name="rs_join_halves")


PACK_COLS = 1024
BIG = (("pool_w", 2), ("sb_w_qkv", 2), ("sb_w_o", 1), ("s5_w_glu", 2), ("xa_wq", 1), ("xa_wkv", 2), ("xa_wo", 1),
       ("ffn_w_up", 2), ("ffn_w_down", 1))
SMALL_SHARDED = (("pool_scale", 1), ("s5_d", 1), ("ffn_conv_w", 2))
REPLICATED = ("mix_norm_g", "s5_a_re", "s5_a_im", "s5_log_dt", "s5_b_re", "s5_b_im", "s5_c_re", "s5_c_im",
              "xa_norm_g", "mem_norm_g", "ffn_norm_g", "ffn_conv_b", "final_norm_g")
WEIGHTS = ("mix_norm_g", "pool_w", "pool_scale", "sb_w_qkv", "sb_w_o", "s5_a_re", "s5_a_im", "s5_log_dt", "s5_b_re",
           "s5_b_im", "s5_c_re", "s5_c_im", "s5_d", "s5_w_glu", "xa_norm_g", "mem_norm_g", "xa_wq", "xa_wkv", "xa_wo",
           "ffn_norm_g", "ffn_w_up", "ffn_conv_w", "ffn_conv_b", "ffn_w_down", "final_norm_g")


def _pack_rows(parts, row_align):
    flat = jnp.concatenate(parts, axis=-1)
    n = flat.shape[-1]
    per = PACK_COLS * row_align
    padded = -(-n // per) * per
    if padded != n:
        flat = jnp.pad(flat, [(0, 0)] * (flat.ndim - 1) + [(0, padded - n)])
    return flat.reshape(*flat.shape[:-1], padded // PACK_COLS, PACK_COLS)


def _to_natural(g, ax):
    g = jnp.moveaxis(g, 0, ax)
    sh = g.shape
    return g.reshape(*sh[:ax], sh[ax] * sh[ax + 1], *sh[ax + 2:])


def _to_chunks(a, ax):
    sh = a.shape
    a = a.reshape(*sh[:ax], N_CHIPS, sh[ax] // N_CHIPS, *sh[ax + 1:])
    return jnp.moveaxis(a, ax, 0).reshape(N_CHIPS, -1)


def _unpack(flat, shapes):
    out, off = [], 0
    for sh in shapes:
        n = math.prod(sh)
        out.append(flat[..., off:off + n].reshape(*flat.shape[:-1], *sh))
        off += n
    return out


def _mixer_kind(i):
    return i % 3, i // 3


def _s5_forward(hn, h, s5, tag):
    bu = bdmm(hn, s5["w_in"], out_dtype=F32, name=f"{tag}_s5_bu")
    xs = s5_scan(bu, s5["pw_fwd"], reverse=False, name=f"{tag}_s5_scan")
    ycx = bdmm(xs, s5["w_out"], out_dtype=F32, name=f"{tag}_s5_cx")
    D = hn.shape[1]

    def post(i, yv, uv, dv):
        return [_gelu(yv + dv * uv)]

    yg = ew(post, [(ycx, "tile"), (hn, "tile"), (s5["d"], "full")], [(D, BF16, "tile")], rows=hn.shape[0], tr=256,
            name=f"{tag}_s5_gelu")[0]
    vg = mm(yg, s5["w_glu"], out_dtype=F32, name=f"{tag}_s5_glu")

    def glu(i, vgv, hv):
        return [hv + vgv[:, :D] * _sigmoid(vgv[:, D:])]

    h1 = ew(glu, [(vg, "tile"), (h, "tile")], [(D, F32, "tile")], rows=hn.shape[0], tr=256, name=f"{tag}_s5_gate")[0]
    return h1, dict(xs=xs, ycx=ycx, yg=yg, vg=vg)


def _s5_backward(hn, dout, s5, sv, tag):
    S, D = hn.shape

    def dglu(i, vgv, dv):
        sg = _sigmoid(vgv[:, D:])
        return [jnp.concatenate([dv * sg, dv * vgv[:, :D] * sg * (1.0 - sg)], axis=1)]

    dvg = ew(dglu, [(sv["vg"], "tile"), (dout, "tile")], [(2 * D, BF16, "tile")], rows=S, tr=256, name=f"{tag}_s5_dgate")[0]
    dyg = mm(dvg, s5["w_glu"], tb=True, out_dtype=F32, name=f"{tag}_s5_dglu_x")
    dw_glu = mm(sv["yg"], dvg, ta=True, out_dtype=BF16, name=f"{tag}_s5_dglu_w")

    def dgelu(i, dygv, yv, uv, dv):
        dyp = dygv * _gelu_grad(yv + dv * uv)
        return [dyp, _colsum(dyp * uv)]

    dyp, dd = ew(dgelu, [(dyg, "tile"), (sv["ycx"], "tile"), (hn, "tile"), (s5["d"], "full")],
                 [(D, F32, "tile"), (D, F32, "acc")], rows=S, tr=256, name=f"{tag}_s5_dgelu")
    gx = bdmm(dyp, s5["w_out_t"], out_dtype=F32, name=f"{tag}_s5_dcx")
    dw_out = bdmm_tn(sv["xs"], dyp, ka=S5_BLOCK, kd=S5_GROUPS_PER_BLOCK * S5_GROUP, name=f"{tag}_s5_dwout")
    lam = s5_scan(gx, s5["pw_bwd"], reverse=True, name=f"{tag}_s5_scan_bwd")
    da = s5_da(lam, sv["xs"], name=f"{tag}_s5_da")
    dw_in = bdmm_tn(hn, lam, ka=S5_GROUPS_PER_BLOCK * S5_GROUP, kd=S5_BLOCK, name=f"{tag}_s5_dwin")
    du = bdmm(lam, s5["w_in_t"], out_dtype=F32, name=f"{tag}_s5_du")

    def dsum(i, duv, dypv, dv):
        return [duv + dypv * dv]

    dhn = ew(dsum, [(du, "tile"), (dyp, "tile"), (s5["d"], "full")], [(D, F32, "tile")], rows=S, tr=256,
             name=f"{tag}_s5_dhn")[0]
    return dhn, dict(dw_glu=dw_glu, dd=dd, dw_out=dw_out, dw_in=dw_in, da=da)


def _step(x, mem, target, w, m, v):
    S, D = x.shape
    depth = w["mix_norm_g"].shape[0]
    F = w["ffn_w_down"].shape[1] * N_CHIPS
    chip = 2 * lax.axis_index("x") + lax.axis_index("y")

    big_shapes = [w[n].shape for n, _ in BIG]
    packed = _pack_rows([w[n].astype(BF16).reshape(-1) for n, _ in BIG], 32)
    gathered = gather_chips(packed, name="ag_weights").reshape(N_CHIPS, -1)
    full = {n: _to_natural(p, ax) for (n, ax), p in zip(BIG, _unpack(gathered, big_shapes))}
    small_shapes = [w[n].shape for n, _ in SMALL_SHARDED]
    spacked = _pack_rows([w[n].reshape(-1) for n, _ in SMALL_SHARDED], SUBLANE)
    sgathered = gather_devices(spacked, name="ag_small").reshape(N_CHIPS, 2, -1)[:, 0]
    full.update({n: _to_natural(p, ax) for (n, ax), p in zip(SMALL_SHARDED, _unpack(sgathered, small_shapes))})

    n_s5 = w["s5_a_re"].shape[0]
    s5 = []
    for j in range(n_s5):
        G = w["s5_a_re"].shape[1]
        nb = G // S5_GROUPS_PER_BLOCK
        prm = (w["s5_a_re"][j], w["s5_a_im"][j], w["s5_log_dt"][j], w["s5_b_re"][j], w["s5_b_im"][j])
        (ab_re, ab_im, bb_re, bb_im), disc_vjp = jax.vjp(_s5_discretize, *prm)
        w_in = _s5_in_weights(bb_re, bb_im, nb)
        w_out = _s5_out_weights(w["s5_c_re"][j], w["s5_c_im"][j], nb)
        s5.append(dict(
            w_in=w_in, w_in_t=jnp.transpose(w_in, (0, 2, 1)), w_out=w_out, w_out_t=jnp.transpose(w_out, (0, 2, 1)),
            pw_fwd=_s5_powers(prm[0], prm[1], prm[2], nb, reverse=False),
            pw_bwd=_s5_powers(prm[0], prm[1], prm[2], nb, reverse=True),
            d=full["s5_d"][j][None], w_glu=full["s5_w_glu"][j], vjp=disc_vjp, nb=nb))

    h = x
    saved = []
    for i in range(depth):
        kind, j = _mixer_kind(i)
        tag = f"L{i}"
        sv = dict(h=h)
        g_mix = w["mix_norm_g"][i][None]
        if kind == 0:
            hn = rms_fwd(h, g_mix, out_dtype=F32, name=f"{tag}_mix_norm")
            h1 = pool_fwd(hn, h, full["pool_w"][j], full["pool_scale"][j][None], name=f"{tag}_pool")
        elif kind == 1:
            hn = rms_fwd(h, g_mix, out_dtype=BF16, name=f"{tag}_mix_norm")
            qkv = mm(hn, full["sb_w_qkv"][j], out_dtype=BF16, name=f"{tag}_sb_qkv")
            o = sb_fwd(qkv, name=f"{tag}_sb_attn")
            h1 = mm(o, full["sb_w_o"][j], res=h, name=f"{tag}_sb_out")
            sv.update(qkv=qkv, o=o)
        else:
            hn = rms_fwd(h, g_mix, out_dtype=F32, name=f"{tag}_mix_norm")
            h1, s5sv = _s5_forward(hn, h, s5[j], tag)
            sv.update(s5sv)
        sv.update(hn=hn, h1=h1)
        hq = rms_fwd(h1, w["xa_norm_g"][i][None], out_dtype=BF16, name=f"{tag}_xa_norm")
        memn = rms_fwd(mem, w["mem_norm_g"][i][None], out_dtype=BF16, name=f"{tag}_mem_norm", tr=mem.shape[0])
        q = mm(hq, full["xa_wq"][i], out_dtype=BF16, name=f"{tag}_xa_q")
        kv = mm(memn, full["xa_wkv"][i], out_dtype=BF16, name=f"{tag}_xa_kv")
        oa = xa_fwd(q, kv, name=f"{tag}_xa_attn")
        h2 = mm(oa, full["xa_wo"][i], res=h1, name=f"{tag}_xa_out")
        hf = rms_fwd(h2, w["ffn_norm_g"][i][None], out_dtype=BF16, name=f"{tag}_ffn_norm")
        uu = mm(hf, full["ffn_w_up"][i], out_dtype=BF16, tn=1408, name=f"{tag}_ffn_up")
        conv_w, conv_b = full["ffn_conv_w"][i], w["ffn_conv_b"][i][None]
        act = ffn_act_fwd(uu, conv_w, conv_b, name=f"{tag}_ffn_act")
        h3 = mm(act, full["ffn_w_down"][i], res=h2, name=f"{tag}_ffn_down")
        sv.update(hq=hq, memn=memn, q=q, kv=kv, oa=oa, h2=h2, hf=hf, uu=uu, act=act)
        saved.append(sv)
        h = h3

    dh, g_final, loss = loss_head(h, w["final_norm_g"][None], target, name="loss_head")

    gw = {n: [None] * w[n].shape[0] for n in WEIGHTS if n != "final_norm_g"}
    for i in reversed(range(depth)):
        kind, j = _mixer_kind(i)
        tag = f"L{i}b"
        sv = saved[i]
        conv_w, conv_b = full["ffn_conv_w"][i], w["ffn_conv_b"][i][None]
        dact = mm(dh, full["ffn_w_down"][i], tb=True, out_dtype=BF16, tn=1408, name=f"{tag}_ffn_down_x")
        gw["ffn_w_down"][i] = mm(sv["act"], dh, ta=True, out_dtype=BF16, tm=1408, name=f"{tag}_ffn_down_w")
        duv, dug, dcw, dcb = ffn_act_bwd(sv["uu"], dact, conv_w, conv_b, name=f"{tag}_ffn_act")
        gw["ffn_conv_w"][i], gw["ffn_conv_b"][i] = dcw, dcb[0]
        dhf = mm(duv, full["ffn_w_up"][i], tb=True, b_col0=0, name=f"{tag}_ffn_up_xv")
        dhf = mm(dug, full["ffn_w_up"][i], tb=True, b_col0=F, res=dhf, name=f"{tag}_ffn_up_xg")
        gw["ffn_w_up"][i] = jnp.concatenate(
            [mm(sv["hf"], duv, ta=True, out_dtype=BF16, tn=1408, name=f"{tag}_ffn_up_wv"),
             mm(sv["hf"], dug, ta=True, out_dtype=BF16, tn=1408, name=f"{tag}_ffn_up_wg")], axis=1)
        dh2, dg = rms_bwd(sv["h2"], w["ffn_norm_g"][i][None], dhf, dh, name=f"{tag}_ffn_norm")
        gw["ffn_norm_g"][i] = dg[0]

        doa = mm(dh2, full["xa_wo"][i], tb=True, out_dtype=BF16, name=f"{tag}_xa_out_x")
        gw["xa_wo"][i] = mm(sv["oa"], dh2, ta=True, out_dtype=BF16, name=f"{tag}_xa_out_w")
        dq, dkv = xa_bwd(sv["q"], sv["kv"], doa, name=f"{tag}_xa_attn")
        dhq = mm(dq, full["xa_wq"][i], tb=True, name=f"{tag}_xa_q_x")
        gw["xa_wq"][i] = mm(sv["hq"], dq, ta=True, out_dtype=BF16, name=f"{tag}_xa_q_w")
        dmemn = mm(dkv, full["xa_wkv"][i], tb=True, name=f"{tag}_xa_kv_x")
        gw["xa_wkv"][i] = mm(sv["memn"], dkv, ta=True, out_dtype=BF16, name=f"{tag}_xa_kv_w")
        gw["mem_norm_g"][i] = rms_bwd_g(mem, dmemn, name=f"{tag}_mem_norm")[0]
        dh1, dg = rms_bwd(sv["h1"], w["xa_norm_g"][i][None], dhq, dh2, name=f"{tag}_xa_norm")
        gw["xa_norm_g"][i] = dg[0]

        g_mix = w["mix_norm_g"][i][None]
        if kind == 0:
            dp, dpw, dps = pool_bwd_w(sv["hn"], dh1, full["pool_w"][j], full["pool_scale"][j][None], name=f"{tag}_pool_w")
            gw["pool_w"][j], gw["pool_scale"][j] = dpw, dps[0]
            dhn = pool_bwd_x(dp, len(POOL_WINDOWS), name=f"{tag}_pool_x")
        elif kind == 1:
            do = mm(dh1, full["sb_w_o"][j], tb=True, name=f"{tag}_sb_out_x")
            gw["sb_w_o"][j] = mm(sv["o"], dh1, ta=True, out_dtype=BF16, name=f"{tag}_sb_out_w")
            dq3 = sb_bwd(sv["qkv"], sv["o"], do, name=f"{tag}_sb_attn")
            dqkv = jnp.concatenate([t.astype(BF16) for t in dq3], axis=1)
            dhn = mm(dqkv, full["sb_w_qkv"][j], tb=True, name=f"{tag}_sb_qkv_x")
            gw["sb_w_qkv"][j] = mm(sv["hn"], dqkv, ta=True, out_dtype=BF16, name=f"{tag}_sb_qkv_w")
        else:
            dhn, sg = _s5_backward(sv["hn"], dh1, s5[j], sv, tag)
            Cg = w["s5_b_re"].shape[-1]
            nb = s5[j]["nb"]
            gw["s5_w_glu"][j], gw["s5_d"][j] = sg["dw_glu"], sg["dd"][0]
            da = sg["da"].reshape(nb, 2, -1)
            gw["s5_a_re"][j], gw["s5_a_im"][j] = da[:, 0].reshape(-1, S5_STATE), da[:, 1].reshape(-1, S5_STATE)
            gw["s5_b_re"][j], gw["s5_b_im"][j] = _s5_in_weight_grads(sg["dw_in"], Cg)
            gw["s5_c_re"][j], gw["s5_c_im"][j] = _s5_out_weight_grads(sg["dw_out"], Cg)
        dh, dg = rms_bwd(sv["h"], g_mix, dhn, dh1, name=f"{tag}_mix_norm")
        gw["mix_norm_g"][i] = dg[0]
    grad_x = dh

    g4 = _pack_rows([_to_chunks(jnp.stack(gw[n]).astype(BF16), ax) for n, ax in BIG], 32)
    reduced = reduce_weight_grads(g4).reshape(-1)
    grads = dict(zip([n for n, _ in BIG], _unpack(reduced, big_shapes)))

    s5_raw = ("s5_a_re", "s5_a_im", "s5_b_re", "s5_b_im")
    small_names = [n for n in REPLICATED if n not in ("final_norm_g", "s5_log_dt")] + [n for n, _ in SMALL_SHARDED]
    small_full = [jnp.stack(gw[n]).astype(F32) for n in small_names] + [g_final[0]]
    small_full_shapes = [t.shape for t in small_full]
    spk = _pack_rows([t.reshape(-1) for t in small_full], SUBLANE)
    everyone = gather_devices(spk, name="ar_small_gather").reshape(N_DEV, *spk.shape)
    ssum = sum_leading(everyone, name="ar_small_sum", align=SUBLANE).reshape(-1)
    small = dict(zip(small_names + ["final_norm_g"], _unpack(ssum, small_full_shapes)))
    per_layer = [[], [], [], [], []]
    for j in range(n_s5):
        ct = tuple(small[n][j] for n in s5_raw)
        for lst, gpart in zip(per_layer, s5[j]["vjp"]((ct[0], ct[1], ct[2], ct[3]))):
            lst.append(gpart)
    for n, lst in zip(("s5_a_re", "s5_a_im", "s5_log_dt", "s5_b_re", "s5_b_im"), per_layer):
        small[n] = jnp.stack(lst)
    for n, ax in SMALL_SHARDED:
        chunks = _to_chunks(small[n], ax)
        small[n] = lax.dynamic_index_in_dim(chunks, chip, 0, keepdims=False).reshape(w[n].shape)
    for n in REPLICATED:
        grads[n] = small[n].reshape(w[n].shape)
    for n, _ in SMALL_SHARDED:
        grads[n] = small[n]

    delta, new_m, new_v = {}, {}, {}
    for n in WEIGHTS:
        delta[n], new_m[n], new_v[n] = adamw(w[n], grads[n], m[n], v[n], name=f"adamw_{n}")
    total = lax.psum(loss[0, 0], ("x", "y", "c"))
    return (total, grad_x, *[grads[n] for n in WEIGHTS], *[delta[n] for n in WEIGHTS],
            *[new_m[n] for n in WEIGHTS], *[new_v[n] for n in WEIGHTS])


def kernel(x, mem, mix_norm_g, pool_w, pool_scale, sb_w_qkv, sb_w_o, s5_a_re, s5_a_im, s5_log_dt, s5_b_re, s5_b_im,
           s5_c_re, s5_c_im, s5_d, s5_w_glu, xa_norm_g, mem_norm_g, xa_wq, xa_wkv, xa_wo, ffn_norm_g, ffn_w_up,
           ffn_conv_w, ffn_conv_b, ffn_w_down, final_norm_g, loss_target, m_mix_norm_g, m_pool_w, m_pool_scale,
           m_sb_w_qkv, m_sb_w_o, m_s5_a_re, m_s5_a_im, m_s5_log_dt, m_s5_b_re, m_s5_b_im, m_s5_c_re, m_s5_c_im,
           m_s5_d, m_s5_w_glu, m_xa_norm_g, m_mem_norm_g, m_xa_wq, m_xa_wkv, m_xa_wo, m_ffn_norm_g, m_ffn_w_up,
           m_ffn_conv_w, m_ffn_conv_b, m_ffn_w_down, m_final_norm_g, v_mix_norm_g, v_pool_w, v_pool_scale,
           v_sb_w_qkv, v_sb_w_o, v_s5_a_re, v_s5_a_im, v_s5_log_dt, v_s5_b_re, v_s5_b_im, v_s5_c_re, v_s5_c_im,
           v_s5_d, v_s5_w_glu, v_xa_norm_g, v_mem_norm_g, v_xa_wq, v_xa_wkv, v_xa_wo, v_ffn_norm_g, v_ffn_w_up,
           v_ffn_conv_w, v_ffn_conv_b, v_ffn_w_down, v_final_norm_g):
    given = dict(locals())
    w = {n: given[n] for n in WEIGHTS}
    m = {n: given["m_" + n] for n in WEIGHTS}
    v = {n: given["v_" + n] for n in WEIGHTS}
    out = _step(x[0], mem[0], loss_target[0], w, m, v)
    return (out[0], out[1][None], *out[2:])
```

```python
import functools
import math

import jax
import jax.numpy as jnp
from jax import lax
from jax.experimental import pallas as pl
from jax.experimental.pallas import tpu as pltpu

F32 = jnp.float32
BF16 = jnp.bfloat16
MESH = pl.DeviceIdType.MESH

EPS = 1e-6
POOL_WINDOWS = (2, 4, 8, 16)
POOL_HALO = 128
POOL_TILE = 256
SB_HEAD_DIM = 64
SB_BLOCK = 128
S5_GROUP = 16
S5_STATE = 64
S5_GROUPS_PER_BLOCK = 8
XA_HEADS = 4
CONV_WIDTH = 3
ADAM_LR, ADAM_B1, ADAM_B2, ADAM_EPS, ADAM_WD, ADAM_STEP = 0.001, 0.9, 0.999, 1e-08, 0.01, 10

V7X_VMEM_BYTES = 64 * 1024 * 1024
LANE = 128
SUBLANE = 8


def _vmem_limit(block_bytes):
    want = 2 * block_bytes + 16 * 1024 * 1024
    return int(min(V7X_VMEM_BYTES - 6 * 1024 * 1024, max(32 * 1024 * 1024, want)))


def _params(sem, block_bytes):
    return pltpu.CompilerParams(dimension_semantics=sem, vmem_limit_bytes=_vmem_limit(block_bytes))


def _tile(n, cap, align=LANE):
    if n <= cap:
        return n
    t = (cap // align) * align
    while t >= align:
        if n % t == 0:
            return t
        t -= align
    return n


def _nbytes(shape, dtype):
    return math.prod(shape) * jnp.dtype(dtype).itemsize


def mm(a, b, *, ta=False, tb=False, out_dtype=F32, res=None, b_col0=None, name, tm=1024, tn=1024, tk=1024):
    if ta:
        K, M = a.shape
    else:
        M, K = a.shape
    if tb:
        N, Kb = b.shape
    else:
        Kb, N = b.shape
    if b_col0 is None:
        assert K == Kb, (a.shape, b.shape, ta, tb)
    tm, tn, tk = _tile(M, tm), _tile(N, tn), _tile(K, tk)
    nk = K // tk
    koff = 0
    if b_col0 is not None:
        assert tb and b_col0 % tk == 0 and b_col0 + K <= Kb
        koff = b_col0 // tk
    dims = (((0,) if ta else (1,), (1,) if tb else (0,)), ((), ()))

    def body(*refs):
        if res is None:
            a_ref, b_ref, o_ref, acc = refs
            r_ref = None
        else:
            a_ref, b_ref, r_ref, o_ref, acc = refs
        k = pl.program_id(2)

        @pl.when(k == 0)
        def _():
            acc[...] = jnp.zeros_like(acc)

        acc[...] += lax.dot_general(a_ref[...].astype(BF16), b_ref[...].astype(BF16), dims,
                                    preferred_element_type=F32)

        @pl.when(k == nk - 1)
        def _():
            r = acc[...]
            if r_ref is not None:
                r = r + r_ref[...].astype(F32)
            o_ref[...] = r.astype(out_dtype)

    a_spec = pl.BlockSpec((tk, tm), lambda i, j, k: (k, i)) if ta else pl.BlockSpec((tm, tk), lambda i, j, k: (i, k))
    b_spec = pl.BlockSpec((tn, tk), lambda i, j, k: (j, k + koff)) if tb else pl.BlockSpec((tk, tn), lambda i, j, k: (k, j))
    in_specs = [a_spec, b_spec]
    args = [a, b]
    blk = _nbytes((tm, tk), a.dtype) + _nbytes((tk, tn), b.dtype) + _nbytes((tm, tn), out_dtype)
    if res is not None:
        in_specs.append(pl.BlockSpec((tm, tn), lambda i, j, k: (i, j)))
        args.append(res)
        blk += _nbytes((tm, tn), res.dtype)
    return pl.pallas_call(
        body, grid=(M // tm, N // tn, nk), in_specs=in_specs,
        out_specs=pl.BlockSpec((tm, tn), lambda i, j, k: (i, j)),
        out_shape=jax.ShapeDtypeStruct((M, N), out_dtype),
        scratch_shapes=[pltpu.VMEM((tm, tn), F32)], name=name,
        compiler_params=_params(("parallel", "parallel", "arbitrary"), blk + _nbytes((tm, tn), F32)),
    )(*args)


def bdmm(a, w, *, out_dtype, name, tm=512):
    M = a.shape[0]
    nb, ka, kn = w.shape
    tm = _tile(M, tm)

    def body(a_ref, w_ref, o_ref):
        o_ref[...] = jnp.dot(a_ref[...].astype(BF16), w_ref[...].astype(BF16),
                             preferred_element_type=F32).astype(out_dtype)

    blk = _nbytes((tm, ka), a.dtype) + _nbytes((ka, kn), w.dtype) + _nbytes((tm, kn), out_dtype)
    return pl.pallas_call(
        body, grid=(M // tm, nb),
        in_specs=[pl.BlockSpec((tm, ka), lambda i, b: (i, b)), pl.BlockSpec((None, ka, kn), lambda i, b: (b, 0, 0))],
        out_specs=pl.BlockSpec((tm, kn), lambda i, b: (i, b)),
        out_shape=jax.ShapeDtypeStruct((M, nb * kn), out_dtype), name=name,
        compiler_params=_params(("parallel", "parallel"), blk),
    )(a, w)


def bdmm_tn(a, d, *, ka, kd, name, tm=512):
    M = a.shape[0]
    nb = a.shape[1] // ka
    assert d.shape[1] == nb * kd
    tm = _tile(M, tm)

    def body(a_ref, d_ref, o_ref):
        i = pl.program_id(1)
        v = lax.dot_general(a_ref[...].astype(BF16), d_ref[...].astype(BF16), (((0,), (0,)), ((), ())),
                            preferred_element_type=F32)

        @pl.when(i == 0)
        def _():
            o_ref[...] = v

        @pl.when(i > 0)
        def _():
            o_ref[...] += v

    blk = _nbytes((tm, ka), a.dtype) + _nbytes((tm, kd), d.dtype) + _nbytes((ka, kd), F32)
    return pl.pallas_call(
        body, grid=(nb, M // tm),
        in_specs=[pl.BlockSpec((tm, ka), lambda b, i: (i, b)), pl.BlockSpec((tm, kd), lambda b, i: (i, b))],
        out_specs=pl.BlockSpec((None, ka, kd), lambda b, i: (b, 0, 0)),
        out_shape=jax.ShapeDtypeStruct((nb, ka, kd), F32), name=name,
        compiler_params=_params(("parallel", "arbitrary"), blk),
    )(a, d)


def ew(fn, ins, outs, *, rows, tr, name):
    n = rows // tr
    assert n * tr == rows
    in_specs, args, blk = [], [], 0
    for a, kind in ins:
        if kind == "tile":
            assert a.shape[0] == rows, (name, a.shape, rows)
            in_specs.append(pl.BlockSpec((tr, a.shape[1]), lambda i: (i, 0)))
            blk += _nbytes((tr, a.shape[1]), a.dtype)
        else:
            in_specs.append(pl.BlockSpec(a.shape, lambda i, nd=a.ndim: (0,) * nd))
            blk += _nbytes(a.shape, a.dtype)
        args.append(a)
    out_shape, out_specs = [], []
    for c, dt, kind in outs:
        if kind == "tile":
            out_shape.append(jax.ShapeDtypeStruct((rows, c), dt))
            out_specs.append(pl.BlockSpec((tr, c), lambda i: (i, 0)))
            blk += _nbytes((tr, c), dt)
        else:
            out_shape.append(jax.ShapeDtypeStruct((1, c), dt))
            out_specs.append(pl.BlockSpec((1, c), lambda i: (0, 0)))
    nin = len(ins)

    def body(*refs):
        i = pl.program_id(0)
        vals = fn(i, *[r[...] for r in refs[:nin]])
        for (c, dt, kind), o, v in zip(outs, refs[nin:], vals):
            if kind == "tile":
                o[...] = v.astype(dt)
            else:
                @pl.when(i == 0)
                def _():
                    o[...] = v.astype(dt)

                @pl.when(i > 0)
                def _():
                    o[...] += v.astype(dt)

    has_acc = any(k == "acc" for _, _, k in outs)
    return pl.pallas_call(
        body, grid=(n,), in_specs=in_specs, out_specs=out_specs, out_shape=out_shape, name=name,
        compiler_params=_params(("arbitrary" if has_acc else "parallel",), 3 * blk),
    )(*args)


def _colsum(x):
    return jnp.sum(x, axis=0, keepdims=True)


def rms_fwd(x, g, *, out_dtype, name, tr=256):
    def fn(i, xv, gv):
        r = lax.rsqrt(jnp.mean(xv * xv, axis=-1, keepdims=True) + EPS)
        return [xv * r * gv]

    return ew(fn, [(x, "tile"), (g, "full")], [(x.shape[1], out_dtype, "tile")], rows=x.shape[0], tr=tr, name=name)[0]


def rms_bwd(x, g, dy, dres, *, name, tr=256):
    def fn(i, xv, gv, dyv, drv):
        dyv = dyv.astype(F32)
        r = lax.rsqrt(jnp.mean(xv * xv, axis=-1, keepdims=True) + EPS)
        xh = xv * r
        gy = dyv * gv
        dx = r * (gy - xh * jnp.mean(gy * xh, axis=-1, keepdims=True))
        return [dx + drv, _colsum(dyv * xh)]

    D = x.shape[1]
    return ew(fn, [(x, "tile"), (g, "full"), (dy, "tile"), (dres, "tile")], [(D, F32, "tile"), (D, F32, "acc")],
              rows=x.shape[0], tr=tr, name=name)


def rms_bwd_g(x, dy, *, name, tr=256):
    def fn(i, xv, dyv):
        r = lax.rsqrt(jnp.mean(xv * xv, axis=-1, keepdims=True) + EPS)
        return [_colsum(dyv.astype(F32) * xv * r)]

    return ew(fn, [(x, "tile"), (dy, "tile")], [(x.shape[1], F32, "acc")], rows=x.shape[0],
              tr=_tile(x.shape[0], tr, SUBLANE), name=name)[0]


def loss_head(h, g, target, *, name, tr=256):
    D = h.shape[1]

    def fn(i, xv, gv, tv):
        r = lax.rsqrt(jnp.mean(xv * xv, axis=-1, keepdims=True) + EPS)
        xh = xv * r
        err = xh * gv - tv
        dy = err * (1.0 / D)
        gy = dy * gv
        dx = r * (gy - xh * jnp.mean(gy * xh, axis=-1, keepdims=True))
        part = _colsum(err * err) * (0.5 / D)
        return [dx, _colsum(dy * xh), jnp.sum(part, axis=1, keepdims=True)]

    return ew(fn, [(h, "tile"), (g, "full"), (target, "tile")], [(D, F32, "tile"), (D, F32, "acc"), (1, F32, "acc")],
              rows=h.shape[0], tr=tr, name=name)


def _as2d(a):
    if a.ndim >= 2 and a.shape[-1] >= LANE:
        return a.reshape(-1, a.shape[-1])
    if a.size % (8 * LANE) == 0:
        return a.reshape(-1, 8 * LANE)
    return a.reshape(1, -1)


def adamw(w, g, m, v, *, name):
    shape = w.shape
    w2, g2, m2, v2 = (_as2d(t) for t in (w, g.astype(F32).reshape(shape), m, v))
    R, C = w2.shape
    tr = R
    if R * C * 4 > (1 << 20):
        tr = _tile(R, max(SUBLANE, (1 << 20) // (C * 4) // SUBLANE * SUBLANE), SUBLANE)
    c1 = 1.0 / (1.0 - ADAM_B1 ** ADAM_STEP)
    c2 = 1.0 / (1.0 - ADAM_B2 ** ADAM_STEP)

    def fn(i, wv, gv, mv, vv):
        mn = ADAM_B1 * mv + (1.0 - ADAM_B1) * gv
        vn = ADAM_B2 * vv + (1.0 - ADAM_B2) * (gv * gv)
        delta = -ADAM_LR * ((mn * c1) / (jnp.sqrt(vn * c2) + ADAM_EPS) + ADAM_WD * wv)
        return [delta, mn, vn]

    d, mn, vn = ew(fn, [(w2, "tile"), (g2, "tile"), (m2, "tile"), (v2, "tile")], [(C, F32, "tile")] * 3,
                   rows=R, tr=tr, name=name)
    return d.reshape(shape), mn.reshape(shape), vn.reshape(shape)


def _split_bf16(x):
    hi = x.astype(BF16)
    return hi, (x - hi.astype(F32)).astype(BF16)


def _dot2(band, x):
    hi, lo = _split_bf16(x)
    return jnp.dot(band, hi, preferred_element_type=F32) + jnp.dot(band, lo, preferred_element_type=F32)


def _pool_fwd_window(xm, xh, r0, win):
    T = xm.shape[0]
    t = r0 + lax.broadcasted_iota(jnp.int32, (T, 1), 0)
    s_main = r0 + lax.broadcasted_iota(jnp.int32, (1, T), 1)
    s_halo = r0 - POOL_HALO + lax.broadcasted_iota(jnp.int32, (1, POOL_HALO), 1)
    band_m = ((s_main <= t) & (s_main > t - win)).astype(BF16)
    band_h = ((s_halo > t - win) & (s_halo >= 0)).astype(BF16)
    ws = _dot2(band_m, xm) + _dot2(band_h, xh)
    cnt = jnp.minimum(t + 1, win).astype(F32)
    return ws / cnt - xm


def _pool_bwd_window(dm, dh, r0, win, S):
    T = dm.shape[0]
    s = r0 + lax.broadcasted_iota(jnp.int32, (T, 1), 0)
    t_main = r0 + lax.broadcasted_iota(jnp.int32, (1, T), 1)
    t_halo = r0 + T + lax.broadcasted_iota(jnp.int32, (1, POOL_HALO), 1)
    band_m = ((t_main >= s) & (t_main < s + win)).astype(BF16)
    band_h = ((t_halo < s + win) & (t_halo < S)).astype(BF16)
    tm_col = r0 + lax.broadcasted_iota(jnp.int32, (T, 1), 0)
    th_col = r0 + T + lax.broadcasted_iota(jnp.int32, (POOL_HALO, 1), 0)
    dmc = dm / jnp.minimum(tm_col + 1, win).astype(F32)
    dhc = dh / jnp.minimum(th_col + 1, win).astype(F32)
    return _dot2(band_m, dmc) + _dot2(band_h, dhc) - dm


def _pool_specs(T, Cg, order):
    per = T // POOL_HALO
    if order == "ig":
        return (pl.BlockSpec((T, Cg), lambda i, g: (i, g)),
                pl.BlockSpec((POOL_HALO, Cg), lambda i, g: (jnp.maximum(i * per - 1, 0), g)))
    return (pl.BlockSpec((T, Cg), lambda g, i: (i, g)),
            pl.BlockSpec((POOL_HALO, Cg), lambda g, i: (jnp.maximum(i * per - 1, 0), g)))


def pool_fwd(hn, h, w, scale, *, name):
    S, D = hn.shape
    G, Cg, _ = w.shape
    T = _tile(S, POOL_TILE)

    def body(xm_ref, xh_ref, h_ref, w_ref, sc_ref, o_ref):
        i, g = pl.program_id(0), pl.program_id(1)
        win = jnp.left_shift(2, g)
        p = _pool_fwd_window(xm_ref[...], xh_ref[...], i * T, win)
        y = jnp.dot(p.astype(BF16), w_ref[...], preferred_element_type=F32)
        o_ref[...] = h_ref[...] + y * sc_ref[...]

    main, halo = _pool_specs(T, Cg, "ig")
    return pl.pallas_call(
        body, grid=(S // T, G),
        in_specs=[main, halo, main, pl.BlockSpec((None, Cg, Cg), lambda i, g: (g, 0, 0)),
                  pl.BlockSpec((1, Cg), lambda i, g: (0, g))],
        out_specs=main, out_shape=jax.ShapeDtypeStruct((S, D), F32), name=name,
        compiler_params=_params(("parallel", "parallel"), 4 * T * Cg * 4),
    )(hn, hn, h, w, scale)


def pool_bwd_w(hn, dt, w, scale, *, name):
    S, D = hn.shape
    G, Cg, _ = w.shape
    T = _tile(S, POOL_TILE)

    def body(xm_ref, xh_ref, dt_ref, w_ref, sc_ref, dp_ref, dw_ref, ds_ref):
        g, i = pl.program_id(0), pl.program_id(1)
        win = jnp.left_shift(2, g)
        p = _pool_fwd_window(xm_ref[...], xh_ref[...], i * T, win).astype(BF16)
        dtv = dt_ref[...]
        ypre = jnp.dot(p, w_ref[...], preferred_element_type=F32)
        dy = (dtv * sc_ref[...]).astype(BF16)
        dp_ref[...] = lax.dot_general(dy, w_ref[...], (((1,), (1,)), ((), ())), preferred_element_type=F32)
        dwv = lax.dot_general(p, dy, (((0,), (0,)), ((), ())), preferred_element_type=F32)
        dsv = _colsum(dtv * ypre)

        @pl.when(i == 0)
        def _():
            dw_ref[...] = dwv
            ds_ref[...] = dsv

        @pl.when(i > 0)
        def _():
            dw_ref[...] += dwv
            ds_ref[...] += dsv

    main, halo = _pool_specs(T, Cg, "gi")
    return pl.pallas_call(
        body, grid=(G, S // T),
        in_specs=[main, halo, main, pl.BlockSpec((None, Cg, Cg), lambda g, i: (g, 0, 0)),
                  pl.BlockSpec((1, Cg), lambda g, i: (0, g))],
        out_specs=[main, pl.BlockSpec((None, Cg, Cg), lambda g, i: (g, 0, 0)), pl.BlockSpec((1, Cg), lambda g, i: (0, g))],
        out_shape=[jax.ShapeDtypeStruct((S, D), F32), jax.ShapeDtypeStruct((G, Cg, Cg), F32),
                   jax.ShapeDtypeStruct((1, D), F32)], name=name,
        compiler_params=_params(("parallel", "arbitrary"), 4 * T * Cg * 4),
    )(hn, hn, dt, w, scale)


def pool_bwd_x(dp, G, *, name):
    S, D = dp.shape
    Cg = D // G
    T = _tile(S, POOL_TILE)
    per = T // POOL_HALO
    last = S // POOL_HALO - 1

    def body(dm_ref, dh_ref, o_ref):
        i, g = pl.program_id(0), pl.program_id(1)
        o_ref[...] = _pool_bwd_window(dm_ref[...], dh_ref[...], i * T, jnp.left_shift(2, g), S)

    main = pl.BlockSpec((T, Cg), lambda i, g: (i, g))
    return pl.pallas_call(
        body, grid=(S // T, G),
        in_specs=[main, pl.BlockSpec((POOL_HALO, Cg), lambda i, g: (jnp.minimum((i + 1) * per, last), g))],
        out_specs=main, out_shape=jax.ShapeDtypeStruct((S, D), F32), name=name,
        compiler_params=_params(("parallel", "parallel"), 3 * T * Cg * 4),
    )(dp, dp)


def _sb_logs(z, mask):
    e = jnp.exp(-jnp.abs(z))
    sp = jnp.log(1.0 + e)
    ls = jnp.minimum(z, 0.0) - sp
    lsn = jnp.where(mask, jnp.minimum(-z, 0.0) - sp, 0.0)
    return ls, lsn, e


def _dot2r(x, band):
    hi, lo = _split_bf16(x)
    return jnp.dot(hi, band, preferred_element_type=F32) + jnp.dot(lo, band, preferred_element_type=F32)


def _dot3r(x, band):
    hi = x.astype(BF16)
    r1 = x - hi.astype(F32)
    mid = r1.astype(BF16)
    lo = (r1 - mid.astype(F32)).astype(BF16)
    return (jnp.dot(hi, band, preferred_element_type=F32) + jnp.dot(mid, band, preferred_element_type=F32)
            + jnp.dot(lo, band, preferred_element_type=F32))


def _head_masks(n_lanes):
    lane = lax.broadcasted_iota(jnp.int32, (1, n_lanes), 1)
    return [((lane >= h * SB_HEAD_DIM) & (lane < (h + 1) * SB_HEAD_DIM)) for h in range(n_lanes // SB_HEAD_DIM)]


_NT = (((1,), (1,)), ((), ()))
_TN = (((0,), (0,)), ((), ()))


SB_UNROLL = 4


def _sb_unroll(S):
    return SB_UNROLL if S % (SB_UNROLL * SB_BLOCK) == 0 else 1


SB_DEAD_LOG = -105.0


def _sb_loop(n_steps, step, init, *, rs_at):
    def cond(state):
        n, alive, _ = state
        return jnp.logical_and(n < n_steps, alive > 0)

    def body(state):
        n, _, carry = state
        carry = step(n, carry)
        top = functools.reduce(jnp.maximum, [jnp.max(r) for r in carry[rs_at]])
        return n + 1, (top > SB_DEAD_LOG).astype(jnp.int32), carry

    return lax.while_loop(cond, body, (jnp.int32(0), jnp.int32(1), init))[2]


def sb_fwd(qkv, *, name):
    S, D3 = qkv.shape
    D = D3 // 3
    B = SB_BLOCK
    npair = D // LANE
    scale = SB_HEAD_DIM ** -0.5

    U = _sb_unroll(S)

    def body(q_ref, k_ref, v_ref, o_ref):
        i = pl.program_id(1)
        masks = _head_masks(LANE)
        q = q_ref[...] * scale
        qh = [jnp.where(m, q, jnp.zeros_like(q)) for m in masks]
        row = lax.broadcasted_iota(jnp.int32, (B, B), 0)
        col = lax.broadcasted_iota(jnp.int32, (B, B), 1)
        upper = (row > col).astype(BF16)
        diag = col < row
        nsuper = i // U + 1

        def step(n, carry):
            os_, rs = list(carry[0]), list(carry[1])
            sup = nsuper - 1 - n
            base = pl.multiple_of(sup * (U * B), U * B)
            kbig = k_ref[pl.ds(base, U * B), :]
            vbig = v_ref[pl.ds(base, U * B), :]
            tiles = [(c, hd) for c in reversed(range(U)) for hd in range(len(masks))]
            kb = {c: kbig[c * B:(c + 1) * B] for c in range(U)}
            vb = {c: vbig[c * B:(c + 1) * B] for c in range(U)}
            mask = {c: jnp.logical_or(sup * U + c < i, jnp.logical_and(sup * U + c == i, diag)) for c in range(U)}
            z = {t: lax.dot_general(qh[t[1]], kb[t[0]], _NT, preferred_element_type=F32) for t in tiles}
            ls, lsn = {}, {}
            for t in tiles:
                ls[t], lsn[t], _ = _sb_logs(z[t], mask[t[0]])
            local = {t: _dot2r(lsn[t], upper) for t in tiles}
            for c, hd in tiles:
                a = jnp.where(mask[c], jnp.exp(ls[(c, hd)] + local[(c, hd)] + rs[hd]), 0.0)
                os_[hd] = os_[hd] + jnp.dot(a.astype(BF16), vb[c], preferred_element_type=F32)
                rs[hd] = rs[hd] + jnp.sum(lsn[(c, hd)], axis=1, keepdims=True)
            return tuple(os_), tuple(rs)

        zero = jnp.zeros((B, 1), F32)
        zacc = jnp.zeros((B, LANE), F32)
        os_, _ = _sb_loop(nsuper, step, (tuple(zacc for _ in masks), tuple(zero for _ in masks)), rs_at=1)
        o = jnp.zeros((B, LANE), F32)
        for m, oh in zip(masks, os_):
            o = jnp.where(m, oh, o)
        o_ref[...] = o

    return pl.pallas_call(
        body, grid=(npair, S // B),
        in_specs=[pl.BlockSpec((B, LANE), lambda p, i: (i, p)),
                  pl.BlockSpec((S, LANE), lambda p, i: (0, npair + p)),
                  pl.BlockSpec((S, LANE), lambda p, i: (0, 2 * npair + p))],
        out_specs=pl.BlockSpec((B, LANE), lambda p, i: (i, p)),
        out_shape=jax.ShapeDtypeStruct((S, D), F32), name=name,
        compiler_params=_params(("parallel", "arbitrary"), 2 * S * LANE * 2),
    )(qkv, qkv, qkv)


def sb_bwd(qkv, o, do, *, name):
    S, D3 = qkv.shape
    D = D3 // 3
    B = SB_BLOCK
    npair = D // LANE
    scale = SB_HEAD_DIM ** -0.5

    U = _sb_unroll(S)

    def body(q_ref, k_ref, v_ref, o_ref, do_ref, dq_ref, dk_ref, dv_ref):
        i = pl.program_id(1)

        @pl.when(i == 0)
        def _():
            dk_ref[...] = jnp.zeros_like(dk_ref)
            dv_ref[...] = jnp.zeros_like(dv_ref)

        masks = _head_masks(LANE)
        q = q_ref[...] * scale
        dov = do_ref[...]
        ov = o_ref[...]
        qh = [jnp.where(m, q, jnp.zeros_like(q)) for m in masks]
        doh = [jnp.where(m, dov, 0.0).astype(BF16) for m in masks]
        gsum = [jnp.sum(dh_.astype(F32) * ov, axis=1, keepdims=True) for dh_ in doh]
        row = lax.broadcasted_iota(jnp.int32, (B, B), 0)
        col = lax.broadcasted_iota(jnp.int32, (B, B), 1)
        upper = (row > col).astype(BF16)
        upper_incl = (row >= col).astype(BF16)
        diag = col < row
        nsuper = i // U + 1

        def step(n, carry):
            dqs, rs, gs = list(carry[0]), list(carry[1]), list(carry[2])
            sup = nsuper - 1 - n
            base = pl.multiple_of(sup * (U * B), U * B)
            kbig = k_ref[pl.ds(base, U * B), :]
            vbig = v_ref[pl.ds(base, U * B), :]
            nh = len(masks)
            tiles = [(c, hd) for c in reversed(range(U)) for hd in range(nh)]
            kb = {c: kbig[c * B:(c + 1) * B] for c in range(U)}
            vb = {c: vbig[c * B:(c + 1) * B] for c in range(U)}
            mask = {c: jnp.logical_or(sup * U + c < i, jnp.logical_and(sup * U + c == i, diag)) for c in range(U)}
            z = {t: lax.dot_general(qh[t[1]], kb[t[0]], _NT, preferred_element_type=F32) for t in tiles}
            da = {t: lax.dot_general(doh[t[1]], vb[t[0]], _NT, preferred_element_type=F32) for t in tiles}
            ls, lsn, sig = {}, {}, {}
            for t in tiles:
                ls[t], lsn[t], e = _sb_logs(z[t], mask[t[0]])
                sig[t] = jnp.exp(ls[t])
            local = {t: _dot2r(lsn[t], upper) for t in tiles}
            ab, g = {}, {}
            for c, hd in tiles:
                a = jnp.where(mask[c], jnp.exp(ls[(c, hd)] + local[(c, hd)] + rs[hd]), 0.0)
                ab[(c, hd)] = a.astype(BF16)
                g[(c, hd)] = ab[(c, hd)].astype(F32) * da[(c, hd)]
                rs[hd] = rs[hd] + jnp.sum(lsn[(c, hd)], axis=1, keepdims=True)
            glocal = {t: _dot3r(g[t], upper_incl) for t in tiles}
            dzb = {}
            for c, hd in tiles:
                t = (c, hd)
                sg = glocal[t] + gs[hd]
                dzb[t] = jnp.where(mask[c], g[t] * (1.0 - sig[t]) - (gsum[hd] - sg) * sig[t], 0.0).astype(BF16)
                gs[hd] = gs[hd] + jnp.sum(g[t], axis=1, keepdims=True)
            for c, hd in tiles:
                dqs[hd] = dqs[hd] + jnp.dot(dzb[(c, hd)], kb[c], preferred_element_type=F32)
            for c in reversed(range(U)):
                dkb = sum(lax.dot_general(dzb[(c, hd)], qh[hd], _TN, preferred_element_type=F32) for hd in range(nh))
                dvb = sum(lax.dot_general(ab[(c, hd)], doh[hd], _TN, preferred_element_type=F32) for hd in range(nh))
                off = pl.multiple_of(base + c * B, B)
                dk_ref[pl.ds(off, B), :] += dkb
                dv_ref[pl.ds(off, B), :] += dvb
            return tuple(dqs), tuple(rs), tuple(gs)

        zero = jnp.zeros((B, 1), F32)
        zs = tuple(zero for _ in masks)
        zacc = jnp.zeros((B, LANE), F32)
        dqs, _, _ = _sb_loop(nsuper, step, (tuple(zacc for _ in masks), zs, zs), rs_at=1)
        dq = jnp.zeros((B, LANE), F32)
        for m, dqh in zip(masks, dqs):
            dq = jnp.where(m, dqh, dq)
        dq_ref[...] = dq * scale

    tile = pl.BlockSpec((B, LANE), lambda p, i: (i, p))
    strip = pl.BlockSpec((S, LANE), lambda p, i: (0, p))
    return pl.pallas_call(
        body, grid=(npair, S // B),
        in_specs=[tile, pl.BlockSpec((S, LANE), lambda p, i: (0, npair + p)),
                  pl.BlockSpec((S, LANE), lambda p, i: (0, 2 * npair + p)), tile, tile],
        out_specs=[tile, strip, strip],
        out_shape=[jax.ShapeDtypeStruct((S, D), F32)] * 3, name=name,
        compiler_params=_params(("parallel", "arbitrary"), 2 * S * LANE * 2 + 2 * S * LANE * 4),
    )(qkv, qkv, qkv, o, do)


S5_HALF = S5_GROUPS_PER_BLOCK * S5_STATE
S5_BLOCK = 2 * S5_HALF


def s5_scan(bu, pw, *, reverse, name, tr=512):
    S, W = bu.shape
    nb = W // S5_BLOCK
    tr = _tile(S, tr, SUBLANE)
    nsub = tr // SUBLANE
    nt = S // tr
    H = S5_HALF

    def body(bu_ref, pw_ref, x_ref, st_re, st_im):
        i = pl.program_id(1)

        @pl.when(i == 0)
        def _():
            st_re[...] = jnp.zeros_like(st_re)
            st_im[...] = jnp.zeros_like(st_im)

        row = lax.broadcasted_iota(jnp.int32, (SUBLANE, H), 0)
        steps = []
        for k, sh in enumerate((1, 2, 4)):
            valid = (row < SUBLANE - sh) if reverse else (row >= sh)
            steps.append((sh, valid, pw_ref[SUBLANE + k:SUBLANE + k + 1, 0:H], pw_ref[SUBLANE + k:SUBLANE + k + 1, H:2 * H]))
        ap_re = pw_ref[0:SUBLANE, 0:H]
        ap_im = pw_ref[0:SUBLANE, H:2 * H]
        edge = (row == 0) if reverse else (row == SUBLANE - 1)

        def sub(n, carry):
            s_re, s_im = carry
            j = (nsub - 1 - n) if reverse else n
            off = pl.multiple_of(j * SUBLANE, SUBLANE)
            r = bu_ref[pl.ds(off, SUBLANE), 0:H]
            m = bu_ref[pl.ds(off, SUBLANE), H:2 * H]
            for sh, valid, a_re, a_im in steps:
                amt = (SUBLANE - sh) if reverse else sh
                rs = jnp.where(valid, pltpu.roll(r, amt, 0), 0.0)
                ms = jnp.where(valid, pltpu.roll(m, amt, 0), 0.0)
                r, m = r + a_re * rs - a_im * ms, m + a_re * ms + a_im * rs
            r, m = r + ap_re * s_re - ap_im * s_im, m + ap_re * s_im + ap_im * s_re
            x_ref[pl.ds(off, SUBLANE), 0:H] = r
            x_ref[pl.ds(off, SUBLANE), H:2 * H] = m
            return (jnp.sum(jnp.where(edge, r, 0.0), axis=0, keepdims=True),
                    jnp.sum(jnp.where(edge, m, 0.0), axis=0, keepdims=True))

        s_re, s_im = lax.fori_loop(0, nsub, sub, (st_re[...], st_im[...]), unroll=4)
        st_re[...] = s_re
        st_im[...] = s_im

    if reverse:
        tile = pl.BlockSpec((tr, S5_BLOCK), lambda b, i: (nt - 1 - i, b))
    else:
        tile = pl.BlockSpec((tr, S5_BLOCK), lambda b, i: (i, b))
    return pl.pallas_call(
        body, grid=(nb, nt),
        in_specs=[tile, pl.BlockSpec((2 * SUBLANE, S5_BLOCK), lambda b, i: (0, b))],
        out_specs=tile, out_shape=jax.ShapeDtypeStruct((S, W), F32),
        scratch_shapes=[pltpu.VMEM((1, H), F32), pltpu.VMEM((1, H), F32)], name=name,
        compiler_params=_params(("parallel", "arbitrary"), 2 * tr * S5_BLOCK * 4),
    )(bu, pw)


def s5_da(lam, x, *, name, tr=512):
    S, W = lam.shape
    nb = W // S5_BLOCK
    tr = _tile(S, tr, SUBLANE)
    nsub = tr // SUBLANE
    nt = S // tr
    H = S5_HALF

    def body(l_ref, x_ref, xh_ref, o_ref, acc_re, acc_im):
        i = pl.program_id(1)

        @pl.when(i == 0)
        def _():
            acc_re[...] = jnp.zeros_like(acc_re)
            acc_im[...] = jnp.zeros_like(acc_im)

        row = lax.broadcasted_iota(jnp.int32, (SUBLANE, H), 0)
        first = row == 0

        def sub(n, carry):
            a_re, a_im = carry
            off = pl.multiple_of(n * SUBLANE, SUBLANE)
            poff = pl.multiple_of(jnp.maximum(n - 1, 0) * SUBLANE, SUBLANE)
            inside = n > 0
            start = jnp.logical_and(i == 0, n == 0)
            out = []
            for lo in (0, H):
                cur = x_ref[pl.ds(off, SUBLANE), lo:lo + H]
                prv = jnp.where(inside, x_ref[pl.ds(poff, SUBLANE), lo:lo + H], xh_ref[:, lo:lo + H])
                xs = jnp.where(first, pltpu.roll(prv, 1, 0), pltpu.roll(cur, 1, 0))
                out.append(jnp.where(jnp.logical_and(start, first), 0.0, xs))
            xs_re, xs_im = out
            l_re = l_ref[pl.ds(off, SUBLANE), 0:H]
            l_im = l_ref[pl.ds(off, SUBLANE), H:2 * H]
            return a_re + l_re * xs_re + l_im * xs_im, a_im + l_im * xs_re - l_re * xs_im

        a_re, a_im = lax.fori_loop(0, nsub, sub, (acc_re[...], acc_im[...]), unroll=4)
        acc_re[...] = a_re
        acc_im[...] = a_im

        @pl.when(i == nt - 1)
        def _():
            o_ref[:, 0:H] = jnp.sum(a_re, axis=0, keepdims=True)
            o_ref[:, H:2 * H] = jnp.sum(a_im, axis=0, keepdims=True)

    per = tr // SUBLANE
    tile = pl.BlockSpec((tr, S5_BLOCK), lambda b, i: (i, b))
    return pl.pallas_call(
        body, grid=(nb, nt),
        in_specs=[tile, tile, pl.BlockSpec((SUBLANE, S5_BLOCK), lambda b, i: (jnp.maximum(i * per - 1, 0), b))],
        out_specs=pl.BlockSpec((1, S5_BLOCK), lambda b, i: (0, b)),
        out_shape=jax.ShapeDtypeStruct((1, W), F32),
        scratch_shapes=[pltpu.VMEM((SUBLANE, H), F32), pltpu.VMEM((SUBLANE, H), F32)], name=name,
        compiler_params=_params(("parallel", "arbitrary"), 2 * tr * S5_BLOCK * 4),
    )(lam, x, x)


def _gelu(y):
    c = math.sqrt(2.0 / math.pi)
    return 0.5 * y * (1.0 + jnp.tanh(c * (y + 0.044715 * y * y * y)))


def _gelu_grad(y):
    c = math.sqrt(2.0 / math.pi)
    th = jnp.tanh(c * (y + 0.044715 * y * y * y))
    return 0.5 * (1.0 + th) + 0.5 * y * (1.0 - th * th) * c * (1.0 + 3.0 * 0.044715 * y * y)


def _sigmoid(x):
    e = jnp.exp(-jnp.abs(x))
    return jnp.where(x >= 0.0, 1.0, e) / (1.0 + e)


def _xa_probs(qh, kh, scale):
    s = lax.dot_general(qh, kh, _NT, preferred_element_type=F32) * scale
    p = jnp.exp(s - jnp.max(s, axis=-1, keepdims=True))
    return p / jnp.sum(p, axis=-1, keepdims=True)


def xa_fwd(q, kv, *, name, tm=512):
    S, D = q.shape
    M = kv.shape[0]
    dh = D // XA_HEADS
    scale = dh ** -0.5
    tm = _tile(S, tm)

    def body(q_ref, kv_ref, o_ref):
        for h in range(XA_HEADS):
            p = _xa_probs(q_ref[:, h * dh:(h + 1) * dh], kv_ref[:, h * dh:(h + 1) * dh], scale)
            o_ref[:, h * dh:(h + 1) * dh] = jnp.dot(p.astype(BF16), kv_ref[:, D + h * dh:D + (h + 1) * dh],
                                                   preferred_element_type=F32).astype(BF16)

    return pl.pallas_call(
        body, grid=(S // tm,),
        in_specs=[pl.BlockSpec((tm, D), lambda i: (i, 0)), pl.BlockSpec((M, 2 * D), lambda i: (0, 0))],
        out_specs=pl.BlockSpec((tm, D), lambda i: (i, 0)),
        out_shape=jax.ShapeDtypeStruct((S, D), BF16), name=name,
        compiler_params=_params(("parallel",), 2 * tm * D * 2 + M * 2 * D * 2),
    )(q, kv)


def xa_bwd(q, kv, do, *, name, tm=512):
    S, D = q.shape
    M = kv.shape[0]
    dh = D // XA_HEADS
    scale = dh ** -0.5
    tm = _tile(S, tm)

    def body(q_ref, kv_ref, do_ref, dq_ref, dkv_ref):
        i = pl.program_id(0)

        @pl.when(i == 0)
        def _():
            dkv_ref[...] = jnp.zeros_like(dkv_ref)

        for h in range(XA_HEADS):
            sl = slice(h * dh, (h + 1) * dh)
            vsl = slice(D + h * dh, D + (h + 1) * dh)
            qh, kh, vh = q_ref[:, sl], kv_ref[:, sl], kv_ref[:, vsl]
            doh = do_ref[:, sl].astype(BF16)
            p = _xa_probs(qh, kh, scale)
            dp = lax.dot_general(doh, vh, _NT, preferred_element_type=F32)
            ds = (p * (dp - jnp.sum(dp * p, axis=-1, keepdims=True)) * scale).astype(BF16)
            dq_ref[:, sl] = jnp.dot(ds, kh, preferred_element_type=F32).astype(BF16)
            dkv_ref[:, sl] += lax.dot_general(ds, qh, _TN, preferred_element_type=F32)
            dkv_ref[:, vsl] += lax.dot_general(p.astype(BF16), doh, _TN, preferred_element_type=F32)

    return pl.pallas_call(
        body, grid=(S // tm,),
        in_specs=[pl.BlockSpec((tm, D), lambda i: (i, 0)), pl.BlockSpec((M, 2 * D), lambda i: (0, 0)),
                  pl.BlockSpec((tm, D), lambda i: (i, 0))],
        out_specs=[pl.BlockSpec((tm, D), lambda i: (i, 0)), pl.BlockSpec((M, 2 * D), lambda i: (0, 0))],
        out_shape=[jax.ShapeDtypeStruct((S, D), BF16), jax.ShapeDtypeStruct((M, 2 * D), F32)], name=name,
        compiler_params=_params(("arbitrary",), 3 * tm * D * 2 + M * 2 * D * 6),
    )(q, kv, do)


FFN_STRIP = 256
FFN_ROWS = 512


def _shift_down(x, k):
    row = lax.broadcasted_iota(jnp.int32, x.shape, 0)
    return jnp.where(row >= k, pltpu.roll(x, k, 0), 0.0)


def _shift_up(x, k):
    n = x.shape[0]
    row = lax.broadcasted_iota(jnp.int32, x.shape, 0)
    return jnp.where(row < n - k, pltpu.roll(x, n - k, 0), 0.0)


FFN_HALO = 2 * SUBLANE


def _rows_with_prev(u_ref, r0, R):
    cur = u_ref[pl.ds(r0, R), :].astype(F32)
    p0 = pl.multiple_of(jnp.maximum(r0 - FFN_HALO, 0), FFN_HALO)
    prev = jnp.where(r0 > 0, u_ref[pl.ds(p0, FFN_HALO), :].astype(F32), 0.0)
    return jnp.concatenate([prev, cur], axis=0), cur


def _conv_rows(u_ref, r0, R, w_ref, b_ref):
    ext, _ = _rows_with_prev(u_ref, r0, R)
    out = w_ref[2:3, :] * ext + w_ref[1:2, :] * _shift_down(ext, 1) + w_ref[0:1, :] * _shift_down(ext, 2) + b_ref[...]
    return out[FFN_HALO:, :]


def ffn_act_fwd(u, conv_w, conv_b, *, name):
    S, F2 = u.shape
    F = F2 // 2
    tc = _tile(F, FFN_STRIP)
    nc = F // tc
    R = _tile(S, FFN_ROWS, 16)

    def body(uv_ref, ug_ref, wv_ref, wg_ref, bv_ref, bg_ref, o_ref):
        def rows(n, _):
            r0 = pl.multiple_of(n * R, R)
            val = _conv_rows(uv_ref, r0, R, wv_ref, bv_ref)
            gate = _conv_rows(ug_ref, r0, R, wg_ref, bg_ref)
            o_ref[pl.ds(r0, R), :] = (gate * _sigmoid(gate) * val).astype(BF16)
            return 0

        lax.fori_loop(0, S // R, rows, 0)

    return pl.pallas_call(
        body, grid=(nc,),
        in_specs=[pl.BlockSpec((S, tc), lambda c: (0, c)), pl.BlockSpec((S, tc), lambda c: (0, nc + c)),
                  pl.BlockSpec((CONV_WIDTH, tc), lambda c: (0, c)), pl.BlockSpec((CONV_WIDTH, tc), lambda c: (0, nc + c)),
                  pl.BlockSpec((1, tc), lambda c: (0, c)), pl.BlockSpec((1, tc), lambda c: (0, nc + c))],
        out_specs=pl.BlockSpec((S, tc), lambda c: (0, c)),
        out_shape=jax.ShapeDtypeStruct((S, F), BF16), name=name,
        compiler_params=_params(("parallel",), 3 * S * tc * 2),
    )(u, u, conv_w, conv_w, conv_b, conv_b)


def ffn_act_bwd(u, dact, conv_w, conv_b, *, name):
    S, F2 = u.shape
    F = F2 // 2
    tc = _tile(F, FFN_STRIP)
    nc = F // tc
    R = _tile(S, FFN_ROWS, 16)
    nr = S // R
    HALO = FFN_HALO

    def body(uv_ref, ug_ref, da_ref, wv_ref, wg_ref, bv_ref, bg_ref,
             duv_ref, dug_ref, dwv_ref, dwg_ref, dbv_ref, dbg_ref, dcv, dcg):
        def p1(n, _):
            r0 = pl.multiple_of(n * R, R)
            val = _conv_rows(uv_ref, r0, R, wv_ref, bv_ref)
            gate = _conv_rows(ug_ref, r0, R, wg_ref, bg_ref)
            d = da_ref[pl.ds(r0, R), :].astype(F32)
            sg = _sigmoid(gate)
            dcv[pl.ds(r0, R), :] = d * gate * sg
            dcg[pl.ds(r0, R), :] = d * val * (sg + gate * sg * (1.0 - sg))
            return 0

        lax.fori_loop(0, nr, p1, 0)

        def p2(n, carry):
            r0 = pl.multiple_of(n * R, R)
            nxt = pl.multiple_of(jnp.minimum(r0 + R, S - HALO), HALO)
            new = []
            for u_ref, dc, w_ref, du_ref, acc in ((uv_ref, dcv, wv_ref, duv_ref, carry[0]),
                                                  (ug_ref, dcg, wg_ref, dug_ref, carry[1])):
                d = dc[pl.ds(r0, R), :]
                after = jnp.where(r0 + R < S, dc[pl.ds(nxt, HALO), :], 0.0)
                ext = jnp.concatenate([d, after], axis=0)
                du = w_ref[2:3, :] * ext + w_ref[1:2, :] * _shift_up(ext, 1) + w_ref[0:1, :] * _shift_up(ext, 2)
                du_ref[pl.ds(r0, R), :] = du[:R, :].astype(BF16)
                uext, cur = _rows_with_prev(u_ref, r0, R)
                u1 = _shift_down(uext, 1)[HALO:, :]
                u2 = _shift_down(uext, 2)[HALO:, :]
                dw2, dw1, dw0, db = acc
                new.append((dw2 + _colsum(d * cur), dw1 + _colsum(d * u1), dw0 + _colsum(d * u2), db + _colsum(d)))
            return tuple(new)

        z = jnp.zeros((1, tc), F32)
        accs = lax.fori_loop(0, nr, p2, ((z, z, z, z), (z, z, z, z)))
        for (dw2, dw1, dw0, db), dw_ref, db_ref in ((accs[0], dwv_ref, dbv_ref), (accs[1], dwg_ref, dbg_ref)):
            dw_ref[0:1, :] = dw0
            dw_ref[1:2, :] = dw1
            dw_ref[2:3, :] = dw2
            db_ref[...] = db

    strip_v = pl.BlockSpec((S, tc), lambda c: (0, c))
    strip_g = pl.BlockSpec((S, tc), lambda c: (0, nc + c))
    w_v = pl.BlockSpec((CONV_WIDTH, tc), lambda c: (0, c))
    w_g = pl.BlockSpec((CONV_WIDTH, tc), lambda c: (0, nc + c))
    b_v = pl.BlockSpec((1, tc), lambda c: (0, c))
    b_g = pl.BlockSpec((1, tc), lambda c: (0, nc + c))
    outs = pl.pallas_call(
        body, grid=(nc,),
        in_specs=[strip_v, strip_g, strip_v, w_v, w_g, b_v, b_g],
        out_specs=[strip_v, strip_v, w_v, w_v, b_v, b_v],
        out_shape=[jax.ShapeDtypeStruct((S, F), BF16), jax.ShapeDtypeStruct((S, F), BF16),
                   jax.ShapeDtypeStruct((CONV_WIDTH, F), F32), jax.ShapeDtypeStruct((CONV_WIDTH, F), F32),
                   jax.ShapeDtypeStruct((1, F), F32), jax.ShapeDtypeStruct((1, F), F32)],
        scratch_shapes=[pltpu.VMEM((S, tc), F32), pltpu.VMEM((S, tc), F32)], name=name,
        compiler_params=_params(("parallel",), 5 * S * tc * 2 + S * tc * 4),
    )(u, u, dact, conv_w, conv_w, conv_b, conv_b)
    duv, dug, dwv, dwg, dbv, dbg = outs
    return duv, dug, jnp.concatenate([dwv, dwg], axis=1), jnp.concatenate([dbv, dbg], axis=1)


def _s5_discretize(a_re, a_im, log_dt, b_re, b_im):
    lam = lax.complex(a_re, a_im)
    dt_lam = lam * jnp.exp(log_dt)[:, None]
    a_bar = jnp.exp(dt_lam)
    b_bar = ((a_bar - 1.0) / lam)[..., None] * lax.complex(b_re, b_im)
    return jnp.real(a_bar), jnp.imag(a_bar), jnp.real(b_bar), jnp.imag(b_bar)


def _s5_cols(z_re, z_im, nb):
    lead = z_re.shape[:-2]
    re = z_re.reshape(*lead, nb, S5_HALF)
    im = z_im.reshape(*lead, nb, S5_HALF)
    return jnp.concatenate([re, im], axis=-1).reshape(*lead, nb * S5_BLOCK)


def _s5_powers(a_re, a_im, log_dt, nb, *, reverse):
    dt_lam = lax.complex(a_re, a_im) * jnp.exp(log_dt)[:, None]
    if reverse:
        dt_lam = jnp.conj(dt_lam)
        carry = jnp.arange(SUBLANE, 0, -1, dtype=F32)
    else:
        carry = jnp.arange(1, SUBLANE + 1, dtype=F32)
    ks = jnp.concatenate([carry, jnp.array([1.0, 2.0, 4.0], F32), jnp.zeros((SUBLANE - 3,), F32)])
    pw = jnp.exp(ks[:, None, None] * dt_lam[None])
    return _s5_cols(jnp.real(pw), jnp.imag(pw), nb)


def _s5_in_weights(bb_re, bb_im, nb):
    eye = jnp.eye(S5_GROUPS_PER_BLOCK, dtype=F32)
    G, P, Cg = bb_re.shape

    def one(bb):
        t = jnp.einsum("bgpi,gh->bgihp", bb.reshape(nb, S5_GROUPS_PER_BLOCK, P, Cg), eye)
        return t.reshape(nb, S5_GROUPS_PER_BLOCK * Cg, S5_HALF)

    return jnp.concatenate([one(bb_re), one(bb_im)], axis=2)


def _s5_out_weights(c_re, c_im, nb):
    eye = jnp.eye(S5_GROUPS_PER_BLOCK, dtype=F32)
    G, Cg, P = c_re.shape

    def one(c):
        t = jnp.einsum("bgip,gh->bgphi", c.reshape(nb, S5_GROUPS_PER_BLOCK, Cg, P), eye)
        return t.reshape(nb, S5_HALF, S5_GROUPS_PER_BLOCK * Cg)

    return jnp.concatenate([one(c_re), -one(c_im)], axis=1)


def _s5_in_weight_grads(dwb, Cg):
    nb = dwb.shape[0]
    eye = jnp.eye(S5_GROUPS_PER_BLOCK, dtype=F32)
    t = dwb.reshape(nb, S5_GROUPS_PER_BLOCK, Cg, 2, S5_GROUPS_PER_BLOCK, S5_STATE)
    out = jnp.einsum("bgirhp,gh->rbgpi", t, eye)
    return out[0].reshape(-1, S5_STATE, Cg), out[1].reshape(-1, S5_STATE, Cg)


def _s5_out_weight_grads(dwc, Cg):
    nb = dwc.shape[0]
    eye = jnp.eye(S5_GROUPS_PER_BLOCK, dtype=F32)
    t = dwc.reshape(nb, 2, S5_GROUPS_PER_BLOCK, S5_STATE, S5_GROUPS_PER_BLOCK, Cg)
    out = jnp.einsum("brgphi,gh->rbgip", t, eye)
    return out[0].reshape(-1, Cg, S5_STATE), -out[1].reshape(-1, Cg, S5_STATE)


ANY = pl.BlockSpec(memory_space=pl.ANY)
N_CHIPS = 4
N_DEV = 8


def _place():
    x, y, c = lax.axis_index("x"), lax.axis_index("y"), lax.axis_index("c")
    chips = [(1 - x, y), (x, 1 - y), (1 - x, 1 - y)]
    return x, y, c, chips


def gather_chips(w, *, name):
    R, C = w.shape
    Hh = R // 2
    assert 2 * Hh == R

    def body(w_ref, out_ref, send_sems, recv_sems):
        x, y, c, chips = _place()
        me = 2 * x + y
        sibling = (x, y, 1 - c)

        def half(chip, hc):
            return out_ref.at[chip, pl.ds(hc * Hh, Hh), :]

        def copy(k, src, dst, to):
            return pltpu.make_async_remote_copy(src_ref=src, dst_ref=dst, send_sem=send_sems.at[k],
                                                recv_sem=recv_sems.at[k], device_id=to, device_id_type=MESH)

        first = [copy(j, w_ref.at[pl.ds(c * Hh, Hh), :], half(me, c), (px, py, c)) for j, (px, py) in enumerate(chips)]
        for cp in first:
            cp.start()
        passed = []
        for j, (px, py) in enumerate(chips):
            landed = half(2 * px + py, c)
            copy(j, landed, landed, (px, py, c)).wait_recv()
            fwd = copy(3 + j, landed, landed, sibling)
            fwd.start()
            passed.append(fwd)
        for j, (px, py) in enumerate(chips):
            theirs = half(2 * px + py, 1 - c)
            copy(3 + j, theirs, theirs, sibling).wait_recv()
        for cp in first + passed:
            cp.wait_send()

    others = pl.pallas_call(
        body, in_specs=[ANY], out_specs=ANY, out_shape=jax.ShapeDtypeStruct((N_CHIPS, R, C), w.dtype),
        scratch_shapes=[pltpu.SemaphoreType.DMA((6,)), pltpu.SemaphoreType.DMA((6,))], name=name,
    )(w)
    return _place_rows(others, w, 2 * lax.axis_index("x") + lax.axis_index("y"), name=name + "_own")


def _place_rows(buf, rows, slot, *, name, tr=592):
    n, R, C = buf.shape
    tr = _tile(R, tr, 16)
    idx = jnp.asarray(slot, jnp.int32).reshape(1)

    def body(s_ref, r_ref, b_ref, o_ref):
        o_ref[...] = r_ref[...]

    return pl.pallas_call(
        body,
        grid_spec=pltpu.PrefetchScalarGridSpec(
            num_scalar_prefetch=1, grid=(R // tr,),
            in_specs=[pl.BlockSpec((tr, C), lambda i, s: (i, 0)), ANY],
            out_specs=pl.BlockSpec((None, tr, C), lambda i, s: (s[0], i, 0))),
        out_shape=jax.ShapeDtypeStruct(buf.shape, buf.dtype), input_output_aliases={2: 0}, name=name,
        compiler_params=_params(("parallel",), 2 * tr * C * 4),
    )(idx, rows, buf)


def swap_halves(g4, *, name):
    n, R, C = g4.shape
    Hh = R // 2

    def body(g_ref, out_ref, send_sem, recv_sem):
        x, y, c, _ = _place()
        cp = pltpu.make_async_remote_copy(
            src_ref=g_ref.at[pl.ds(0, n), pl.ds((1 - c) * Hh, Hh), :], dst_ref=out_ref, send_sem=send_sem,
            recv_sem=recv_sem, device_id=(x, y, 1 - c), device_id_type=MESH)
        cp.start()
        cp.wait()

    return pl.pallas_call(
        body, in_specs=[ANY], out_specs=ANY, out_shape=jax.ShapeDtypeStruct((n, Hh, C), g4.dtype),
        scratch_shapes=[pltpu.SemaphoreType.DMA, pltpu.SemaphoreType.DMA], name=name,
    )(g4)


def add_half(g4, other, *, name, tr=160):
    n, R, C = g4.shape
    Hh = R // 2
    tr = _tile(Hh, tr, 16)
    nblk = Hh // tr
    cidx = lax.axis_index("c").astype(jnp.int32).reshape(1)

    def body(c_ref, g_ref, o_ref, out_ref):
        out_ref[...] = (g_ref[...].astype(F32) + o_ref[...].astype(F32)).astype(out_ref.dtype)

    return pl.pallas_call(
        body,
        grid_spec=pltpu.PrefetchScalarGridSpec(
            num_scalar_prefetch=1, grid=(nblk,),
            in_specs=[pl.BlockSpec((n, tr, C), lambda i, c_ref: (0, c_ref[0] * nblk + i, 0)),
                      pl.BlockSpec((n, tr, C), lambda i, c_ref: (0, i, 0))],
            out_specs=pl.BlockSpec((n, tr, C), lambda i, c_ref: (0, i, 0))),
        out_shape=jax.ShapeDtypeStruct((n, Hh, C), g4.dtype), name=name,
        compiler_params=_params(("parallel",), 3 * n * tr * C * 2),
    )(cidx, g4, other)


def scatter_chips(p4, *, name):
    n, Hh, C = p4.shape

    def body(p_ref, out_ref, send_sems, recv_sems):
        x, y, c, chips = _place()
        me = 2 * x + y
        sends = []
        for j, (px, py) in enumerate(chips):
            cp = pltpu.make_async_remote_copy(src_ref=p_ref.at[2 * px + py], dst_ref=out_ref.at[me], send_sem=send_sems.at[j],
                                              recv_sem=recv_sems.at[j], device_id=(px, py, c), device_id_type=MESH)
            cp.start()
            sends.append(cp)
        for j, (px, py) in enumerate(chips):
            slot = out_ref.at[2 * px + py]
            pltpu.make_async_remote_copy(src_ref=slot, dst_ref=slot, send_sem=send_sems.at[j], recv_sem=recv_sems.at[j],
                                         device_id=(px, py, c), device_id_type=MESH).wait_recv()
        for cp in sends:
            cp.wait_send()

    return pl.pallas_call(
        body, in_specs=[ANY], out_specs=ANY, out_shape=jax.ShapeDtypeStruct((n, Hh, C), p4.dtype),
        scratch_shapes=[pltpu.SemaphoreType.DMA((3,)), pltpu.SemaphoreType.DMA((3,))], name=name,
    )(p4)


def sum_chips(landed, part, *, name, tr=96):
    n, Hh, C = landed.shape
    tr = _tile(Hh, tr, 16)
    nblk = Hh // tr
    x, y, c, _ = _place()
    idx = jnp.stack([2 * x + y, c]).astype(jnp.int32)

    def slot_spec(k):
        return pl.BlockSpec((None, tr, C), lambda i, s: (jnp.where(s[0] == k, (k + 1) % n, k), i, 0))

    def body(s_ref, *refs):
        slots, own_ref, o_ref = refs[:n], refs[n], refs[n + 1]
        own = own_ref[...].astype(F32)
        acc = None
        for k in range(n):
            v = jnp.where(s_ref[0] == k, own, slots[k][...].astype(F32))
            acc = v if acc is None else acc + v
        o_ref[...] = acc

    return pl.pallas_call(
        body,
        grid_spec=pltpu.PrefetchScalarGridSpec(
            num_scalar_prefetch=1, grid=(nblk,),
            in_specs=[slot_spec(k) for k in range(n)] + [pl.BlockSpec((None, tr, C), lambda i, s: (s[0], i, 0))],
            out_specs=pl.BlockSpec((tr, C), lambda i, s: (s[1] * nblk + i, 0))),
        out_shape=jax.ShapeDtypeStruct((2 * Hh, C), F32), name=name,
        compiler_params=_params(("parallel",), 6 * tr * C * 4),
    )(idx, *([landed] * n), part)


def sum_leading(x3, *, name, tr=160, align=16):
    n, R, C = x3.shape
    tr = _tile(R, tr, align)

    def body(x_ref, o_ref):
        acc = x_ref[0].astype(F32)
        for k in range(1, n):
            acc = acc + x_ref[k].astype(F32)
        o_ref[...] = acc

    return pl.pallas_call(
        body, grid=(R // tr,), in_specs=[pl.BlockSpec((n, tr, C), lambda i: (0, i, 0))],
        out_specs=pl.BlockSpec((tr, C), lambda i: (i, 0)), out_shape=jax.ShapeDtypeStruct((R, C), F32), name=name,
        compiler_params=_params(("parallel",), n * tr * C * 4 + tr * C * 4),
    )(x3)


def join_halves(r, *, name):
    R, C = r.shape
    Hh = R // 2

    def body(r_ref, out_ref, send_sem, recv_sem):
        x, y, c, _ = _place()
        mine = out_ref.at[pl.ds(c * Hh, Hh), :]
        theirs = out_ref.at[pl.ds((1 - c) * Hh, Hh), :]
        cp = pltpu.make_async_remote_copy(src_ref=mine, dst_ref=mine, send_sem=send_sem, recv_sem=recv_sem,
                                          device_id=(x, y, 1 - c), device_id_type=MESH)
        cp.start()
        pltpu.make_async_remote_copy(src_ref=theirs, dst_ref=theirs, send_sem=send_sem, recv_sem=recv_sem,
                                     device_id=(x, y, 1 - c), device_id_type=MESH).wait_recv()
        cp.wait_send()

    return pl.pallas_call(
        body, in_specs=[ANY], out_specs=ANY, out_shape=jax.ShapeDtypeStruct((R, C), r.dtype),
        input_output_aliases={0: 0}, scratch_shapes=[pltpu.SemaphoreType.DMA, pltpu.SemaphoreType.DMA], name=name,
    )(r)


def gather_devices(v, *, name):
    m_per, n = v.shape

    def body(x_ref, out_ref, send_sems, recv_sems, local_sem):
        x, y, c, chips = _place()
        me, sibling = (x, y, c), (x, y, 1 - c)

        def rows(px, py, pc):
            return out_ref.at[pl.ds((4 * px + 2 * py + pc) * m_per, m_per), :]

        def copy(k, block, to, src=None):
            return pltpu.make_async_remote_copy(src_ref=rows(*block) if src is None else src, dst_ref=rows(*block),
                                                send_sem=send_sems.at[k], recv_sem=recv_sems.at[k], device_id=to,
                                                device_id_type=MESH)

        mine = pltpu.make_async_copy(x_ref, rows(*me), local_sem)
        mine.start()
        first = [copy(0, me, sibling, src=x_ref)]
        first += [copy(1 + j, me, (*chip, c), src=x_ref) for j, chip in enumerate(chips)]
        for cp in first:
            cp.start()
        passed = [copy(4 + j, (*chip, c), sibling) for j, chip in enumerate(chips)]
        for j, chip in enumerate(chips):
            copy(1 + j, (*chip, c), me).wait_recv()
            passed[j].start()
        copy(0, sibling, me).wait_recv()
        for j, chip in enumerate(chips):
            copy(4 + j, (*chip, 1 - c), me).wait_recv()
        for cp in first + passed:
            cp.wait_send()
        mine.wait()

    return pl.pallas_call(
        body, out_shape=jax.ShapeDtypeStruct((N_DEV * m_per, n), v.dtype),
        in_specs=[pl.BlockSpec(memory_space=pltpu.VMEM)], out_specs=pl.BlockSpec(memory_space=pltpu.VMEM),
        scratch_shapes=[pltpu.SemaphoreType.DMA((7,)), pltpu.SemaphoreType.DMA((7,)), pltpu.SemaphoreType.DMA], name=name,
        compiler_params=pltpu.CompilerParams(vmem_limit_bytes=_vmem_limit(9 * m_per * n * 4)),
    )(v)


def reduce_weight_grads(g4):
    other = swap_halves(g4, name="rs_swap_halves")
    part = add_half(g4, other, name="rs_add_half")
    landed = scatter_chips(part, name="rs_scatter_chips")
    mine = sum_chips(landed, part, name="rs_sum_chips")
    return join_halves(mine, name="rs_join_halves")


PACK_COLS = 1024
BIG = (("pool_w", 2), ("sb_w_qkv", 2), ("sb_w_o", 1), ("s5_w_glu", 2), ("xa_wq", 1), ("xa_wkv", 2), ("xa_wo", 1),
       ("ffn_w_up", 2), ("ffn_w_down", 1))
SMALL_SHARDED = (("pool_scale", 1), ("s5_d", 1), ("ffn_conv_w", 2))
REPLICATED = ("mix_norm_g", "s5_a_re", "s5_a_im", "s5_log_dt", "s5_b_re", "s5_b_im", "s5_c_re", "s5_c_im",
              "xa_norm_g", "mem_norm_g", "ffn_norm_g", "ffn_conv_b", "final_norm_g")
WEIGHTS = ("mix_norm_g", "pool_w", "pool_scale", "sb_w_qkv", "sb_w_o", "s5_a_re", "s5_a_im", "s5_log_dt", "s5_b_re",
           "s5_b_im", "s5_c_re", "s5_c_im", "s5_d", "s5_w_glu", "xa_norm_g", "mem_norm_g", "xa_wq", "xa_wkv", "xa_wo",
           "ffn_norm_g", "ffn_w_up", "ffn_conv_w", "ffn_conv_b", "ffn_w_down", "final_norm_g")


def _pack_rows(parts, row_align):
    flat = jnp.concatenate(parts, axis=-1)
    n = flat.shape[-1]
    per = PACK_COLS * row_align
    padded = -(-n // per) * per
    if padded != n:
        flat = jnp.pad(flat, [(0, 0)] * (flat.ndim - 1) + [(0, padded - n)])
    return flat.reshape(*flat.shape[:-1], padded // PACK_COLS, PACK_COLS)


def _to_natural(g, ax):
    g = jnp.moveaxis(g, 0, ax)
    sh = g.shape
    return g.reshape(*sh[:ax], sh[ax] * sh[ax + 1], *sh[ax + 2:])


def _to_chunks(a, ax):
    sh = a.shape
    a = a.reshape(*sh[:ax], N_CHIPS, sh[ax] // N_CHIPS, *sh[ax + 1:])
    return jnp.moveaxis(a, ax, 0).reshape(N_CHIPS, -1)


def _unpack(flat, shapes):
    out, off = [], 0
    for sh in shapes:
        n = math.prod(sh)
        out.append(flat[..., off:off + n].reshape(*flat.shape[:-1], *sh))
        off += n
    return out


def _mixer_kind(i):
    return i % 3, i // 3


def _s5_forward(hn, h, s5, tag):
    bu = bdmm(hn, s5["w_in"], out_dtype=F32, name=f"{tag}_s5_bu")
    xs = s5_scan(bu, s5["pw_fwd"], reverse=False, name=f"{tag}_s5_scan")
    ycx = bdmm(xs, s5["w_out"], out_dtype=F32, name=f"{tag}_s5_cx")
    D = hn.shape[1]

    def post(i, yv, uv, dv):
        return [_gelu(yv + dv * uv)]

    yg = ew(post, [(ycx, "tile"), (hn, "tile"), (s5["d"], "full")], [(D, BF16, "tile")], rows=hn.shape[0], tr=256,
            name=f"{tag}_s5_gelu")[0]
    vg = mm(yg, s5["w_glu"], out_dtype=F32, name=f"{tag}_s5_glu")

    def glu(i, vgv, hv):
        return [hv + vgv[:, :D] * _sigmoid(vgv[:, D:])]

    h1 = ew(glu, [(vg, "tile"), (h, "tile")], [(D, F32, "tile")], rows=hn.shape[0], tr=256, name=f"{tag}_s5_gate")[0]
    return h1, dict(xs=xs, ycx=ycx, yg=yg, vg=vg)


def _s5_backward(hn, dout, s5, sv, tag):
    S, D = hn.shape

    def dglu(i, vgv, dv):
        sg = _sigmoid(vgv[:, D:])
        return [jnp.concatenate([dv * sg, dv * vgv[:, :D] * sg * (1.0 - sg)], axis=1)]

    dvg = ew(dglu, [(sv["vg"], "tile"), (dout, "tile")], [(2 * D, BF16, "tile")], rows=S, tr=256, name=f"{tag}_s5_dgate")[0]
    dyg = mm(dvg, s5["w_glu"], tb=True, out_dtype=F32, name=f"{tag}_s5_dglu_x")
    dw_glu = mm(sv["yg"], dvg, ta=True, out_dtype=BF16, name=f"{tag}_s5_dglu_w")

    def dgelu(i, dygv, yv, uv, dv):
        dyp = dygv * _gelu_grad(yv + dv * uv)
        return [dyp, _colsum(dyp * uv)]

    dyp, dd = ew(dgelu, [(dyg, "tile"), (sv["ycx"], "tile"), (hn, "tile"), (s5["d"], "full")],
                 [(D, F32, "tile"), (D, F32, "acc")], rows=S, tr=256, name=f"{tag}_s5_dgelu")
    gx = bdmm(dyp, s5["w_out_t"], out_dtype=F32, name=f"{tag}_s5_dcx")
    dw_out = bdmm_tn(sv["xs"], dyp, ka=S5_BLOCK, kd=S5_GROUPS_PER_BLOCK * S5_GROUP, name=f"{tag}_s5_dwout")
    lam = s5_scan(gx, s5["pw_bwd"], reverse=True, name=f"{tag}_s5_scan_bwd")
    da = s5_da(lam, sv["xs"], name=f"{tag}_s5_da")
    dw_in = bdmm_tn(hn, lam, ka=S5_GROUPS_PER_BLOCK * S5_GROUP, kd=S5_BLOCK, name=f"{tag}_s5_dwin")
    du = bdmm(lam, s5["w_in_t"], out_dtype=F32, name=f"{tag}_s5_du")

    def dsum(i, duv, dypv, dv):
        return [duv + dypv * dv]

    dhn = ew(dsum, [(du, "tile"), (dyp, "tile"), (s5["d"], "full")], [(D, F32, "tile")], rows=S, tr=256,
             name=f"{tag}_s5_dhn")[0]
    return dhn, dict(dw_glu=dw_glu, dd=dd, dw_out=dw_out, dw_in=dw_in, da=da)


def _step(x, mem, target, w, m, v):
    S, D = x.shape
    depth = w["mix_norm_g"].shape[0]
    F = w["ffn_w_down"].shape[1] * N_CHIPS
    chip = 2 * lax.axis_index("x") + lax.axis_index("y")

    big_shapes = [w[n].shape for n, _ in BIG]
    packed = _pack_rows([w[n].astype(BF16).reshape(-1) for n, _ in BIG], 32)
    gathered = gather_chips(packed, name="ag_weights").reshape(N_CHIPS, -1)
    full = {n: _to_natural(p, ax) for (n, ax), p in zip(BIG, _unpack(gathered, big_shapes))}
    small_shapes = [w[n].shape for n, _ in SMALL_SHARDED]
    spacked = _pack_rows([w[n].reshape(-1) for n, _ in SMALL_SHARDED], SUBLANE)
    sgathered = gather_devices(spacked, name="ag_small").reshape(N_CHIPS, 2, -1)[:, 0]
    full.update({n: _to_natural(p, ax) for (n, ax), p in zip(SMALL_SHARDED, _unpack(sgathered, small_shapes))})

    n_s5 = w["s5_a_re"].shape[0]
    s5 = []
    for j in range(n_s5):
        G = w["s5_a_re"].shape[1]
        nb = G // S5_GROUPS_PER_BLOCK
        prm = (w["s5_a_re"][j], w["s5_a_im"][j], w["s5_log_dt"][j], w["s5_b_re"][j], w["s5_b_im"][j])
        (ab_re, ab_im, bb_re, bb_im), disc_vjp = jax.vjp(_s5_discretize, *prm)
        w_in = _s5_in_weights(bb_re, bb_im, nb)
        w_out = _s5_out_weights(w["s5_c_re"][j], w["s5_c_im"][j], nb)
        s5.append(dict(
            w_in=w_in, w_in_t=jnp.transpose(w_in, (0, 2, 1)), w_out=w_out, w_out_t=jnp.transpose(w_out, (0, 2, 1)),
            pw_fwd=_s5_powers(prm[0], prm[1], prm[2], nb, reverse=False),
            pw_bwd=_s5_powers(prm[0], prm[1], prm[2], nb, reverse=True),
            d=full["s5_d"][j][None], w_glu=full["s5_w_glu"][j], vjp=disc_vjp, nb=nb))

    h = x
    saved = []
    for i in range(depth):
        kind, j = _mixer_kind(i)
        tag = f"L{i}"
        sv = dict(h=h)
        g_mix = w["mix_norm_g"][i][None]
        if kind == 0:
            hn = rms_fwd(h, g_mix, out_dtype=F32, name=f"{tag}_mix_norm")
            h1 = pool_fwd(hn, h, full["pool_w"][j], full["pool_scale"][j][None], name=f"{tag}_pool")
        elif kind == 1:
            hn = rms_fwd(h, g_mix, out_dtype=BF16, name=f"{tag}_mix_norm")
            qkv = mm(hn, full["sb_w_qkv"][j], out_dtype=BF16, name=f"{tag}_sb_qkv")
            o = sb_fwd(qkv, name=f"{tag}_sb_attn")
            h1 = mm(o, full["sb_w_o"][j], res=h, name=f"{tag}_sb_out")
            sv.update(qkv=qkv, o=o)
        else:
            hn = rms_fwd(h, g_mix, out_dtype=F32, name=f"{tag}_mix_norm")
            h1, s5sv = _s5_forward(hn, h, s5[j], tag)
            sv.update(s5sv)
        sv.update(hn=hn, h1=h1)
        hq = rms_fwd(h1, w["xa_norm_g"][i][None], out_dtype=BF16, name=f"{tag}_xa_norm")
        memn = rms_fwd(mem, w["mem_norm_g"][i][None], out_dtype=BF16, name=f"{tag}_mem_norm", tr=mem.shape[0])
        q = mm(hq, full["xa_wq"][i], out_dtype=BF16, name=f"{tag}_xa_q")
        kv = mm(memn, full["xa_wkv"][i], out_dtype=BF16, name=f"{tag}_xa_kv")
        oa = xa_fwd(q, kv, name=f"{tag}_xa_attn")
        h2 = mm(oa, full["xa_wo"][i], res=h1, name=f"{tag}_xa_out")
        hf = rms_fwd(h2, w["ffn_norm_g"][i][None], out_dtype=BF16, name=f"{tag}_ffn_norm")
        uu = mm(hf, full["ffn_w_up"][i], out_dtype=BF16, tn=1408, name=f"{tag}_ffn_up")
        conv_w, conv_b = full["ffn_conv_w"][i], w["ffn_conv_b"][i][None]
        act = ffn_act_fwd(uu, conv_w, conv_b, name=f"{tag}_ffn_act")
        h3 = mm(act, full["ffn_w_down"][i], res=h2, tk=1408, name=f"{tag}_ffn_down")
        sv.update(hq=hq, memn=memn, q=q, kv=kv, oa=oa, h2=h2, hf=hf, uu=uu, act=act)
        saved.append(sv)
        h = h3

    dh, g_final, loss = loss_head(h, w["final_norm_g"][None], target, name="loss_head")

    gw = {n: [None] * w[n].shape[0] for n in WEIGHTS if n != "final_norm_g"}
    for i in reversed(range(depth)):
        kind, j = _mixer_kind(i)
        tag = f"L{i}b"
        sv = saved[i]
        conv_w, conv_b = full["ffn_conv_w"][i], w["ffn_conv_b"][i][None]
        dact = mm(dh, full["ffn_w_down"][i], tb=True, out_dtype=BF16, tn=1408, name=f"{tag}_ffn_down_x")
        gw["ffn_w_down"][i] = mm(sv["act"], dh, ta=True, out_dtype=BF16, tm=1408, name=f"{tag}_ffn_down_w")
        duv, dug, dcw, dcb = ffn_act_bwd(sv["uu"], dact, conv_w, conv_b, name=f"{tag}_ffn_act")
        gw["ffn_conv_w"][i], gw["ffn_conv_b"][i] = dcw, dcb[0]
        dhf = mm(duv, full["ffn_w_up"][i], tb=True, b_col0=0, tk=1408, name=f"{tag}_ffn_up_xv")
        dhf = mm(dug, full["ffn_w_up"][i], tb=True, b_col0=F, res=dhf, tk=1408, name=f"{tag}_ffn_up_xg")
        gw["ffn_w_up"][i] = jnp.concatenate(
            [mm(sv["hf"], duv, ta=True, out_dtype=BF16, tn=1408, name=f"{tag}_ffn_up_wv"),
             mm(sv["hf"], dug, ta=True, out_dtype=BF16, tn=1408, name=f"{tag}_ffn_up_wg")], axis=1)
        dh2, dg = rms_bwd(sv["h2"], w["ffn_norm_g"][i][None], dhf, dh, name=f"{tag}_ffn_norm")
        gw["ffn_norm_g"][i] = dg[0]

        doa = mm(dh2, full["xa_wo"][i], tb=True, out_dtype=BF16, name=f"{tag}_xa_out_x")
        gw["xa_wo"][i] = mm(sv["oa"], dh2, ta=True, out_dtype=BF16, name=f"{tag}_xa_out_w")
        dq, dkv = xa_bwd(sv["q"], sv["kv"], doa, name=f"{tag}_xa_attn")
        dhq = mm(dq, full["xa_wq"][i], tb=True, name=f"{tag}_xa_q_x")
        gw["xa_wq"][i] = mm(sv["hq"], dq, ta=True, out_dtype=BF16, name=f"{tag}_xa_q_w")
        dmemn = mm(dkv, full["xa_wkv"][i], tb=True, name=f"{tag}_xa_kv_x")
        gw["xa_wkv"][i] = mm(sv["memn"], dkv, ta=True, out_dtype=BF16, name=f"{tag}_xa_kv_w")
        gw["mem_norm_g"][i] = rms_bwd_g(mem, dmemn, name=f"{tag}_mem_norm")[0]
        dh1, dg = rms_bwd(sv["h1"], w["xa_norm_g"][i][None], dhq, dh2, name=f"{tag}_xa_norm")
        gw["xa_norm_g"][i] = dg[0]

        g_mix = w["mix_norm_g"][i][None]
        if kind == 0:
            dp, dpw, dps = pool_bwd_w(sv["hn"], dh1, full["pool_w"][j], full["pool_scale"][j][None], name=f"{tag}_pool_w")
            gw["pool_w"][j], gw["pool_scale"][j] = dpw, dps[0]
            dhn = pool_bwd_x(dp, len(POOL_WINDOWS), name=f"{tag}_pool_x")
        elif kind == 1:
            do = mm(dh1, full["sb_w_o"][j], tb=True, name=f"{tag}_sb_out_x")
            gw["sb_w_o"][j] = mm(sv["o"], dh1, ta=True, out_dtype=BF16, name=f"{tag}_sb_out_w")
            dq3 = sb_bwd(sv["qkv"], sv["o"], do, name=f"{tag}_sb_attn")
            dqkv = jnp.concatenate([t.astype(BF16) for t in dq3], axis=1)
            dhn = mm(dqkv, full["sb_w_qkv"][j], tb=True, name=f"{tag}_sb_qkv_x")
            gw["sb_w_qkv"][j] = mm(sv["hn"], dqkv, ta=True, out_dtype=BF16, name=f"{tag}_sb_qkv_w")
        else:
            dhn, sg = _s5_backward(sv["hn"], dh1, s5[j], sv, tag)
            Cg = w["s5_b_re"].shape[-1]
            nb = s5[j]["nb"]
            gw["s5_w_glu"][j], gw["s5_d"][j] = sg["dw_glu"], sg["dd"][0]
            da = sg["da"].reshape(nb, 2, -1)
            gw["s5_a_re"][j], gw["s5_a_im"][j] = da[:, 0].reshape(-1, S5_STATE), da[:, 1].reshape(-1, S5_STATE)
            gw["s5_b_re"][j], gw["s5_b_im"][j] = _s5_in_weight_grads(sg["dw_in"], Cg)
            gw["s5_c_re"][j], gw["s5_c_im"][j] = _s5_out_weight_grads(sg["dw_out"], Cg)
        dh, dg = rms_bwd(sv["h"], g_mix, dhn, dh1, name=f"{tag}_mix_norm")
        gw["mix_norm_g"][i] = dg[0]
    grad_x = dh

    g4 = _pack_rows([_to_chunks(jnp.stack(gw[n]).astype(BF16), ax) for n, ax in BIG], 32)
    reduced = reduce_weight_grads(g4).reshape(-1)
    grads = dict(zip([n for n, _ in BIG], _unpack(reduced, big_shapes)))

    s5_raw = ("s5_a_re", "s5_a_im", "s5_b_re", "s5_b_im")
    small_names = [n for n in REPLICATED if n not in ("final_norm_g", "s5_log_dt")] + [n for n, _ in SMALL_SHARDED]
    small_full = [jnp.stack(gw[n]).astype(F32) for n in small_names] + [g_final[0]]
    small_full_shapes = [t.shape for t in small_full]
    spk = _pack_rows([t.reshape(-1) for t in small_full], SUBLANE)
    everyone = gather_devices(spk, name="ar_small_gather").reshape(N_DEV, *spk.shape)
    ssum = sum_leading(everyone, name="ar_small_sum", align=SUBLANE).reshape(-1)
    small = dict(zip(small_names + ["final_norm_g"], _unpack(ssum, small_full_shapes)))
    per_layer = [[], [], [], [], []]
    for j in range(n_s5):
        ct = tuple(small[n][j] for n in s5_raw)
        for lst, gpart in zip(per_layer, s5[j]["vjp"]((ct[0], ct[1], ct[2], ct[3]))):
            lst.append(gpart)
    for n, lst in zip(("s5_a_re", "s5_a_im", "s5_log_dt", "s5_b_re", "s5_b_im"), per_layer):
        small[n] = jnp.stack(lst)
    for n, ax in SMALL_SHARDED:
        chunks = _to_chunks(small[n], ax)
        small[n] = lax.dynamic_index_in_dim(chunks, chip, 0, keepdims=False).reshape(w[n].shape)
    for n in REPLICATED:
        grads[n] = small[n].reshape(w[n].shape)
    for n, _ in SMALL_SHARDED:
        grads[n] = small[n]

    delta, new_m, new_v = {}, {}, {}
    for n in WEIGHTS:
        delta[n], new_m[n], new_v[n] = adamw(w[n], grads[n], m[n], v[n], name=f"adamw_{n}")
    total = lax.psum(loss[0, 0], ("x", "y", "c"))
    return (total, grad_x, *[grads[n] for n in WEIGHTS], *[delta[n] for n in WEIGHTS],
            *[new_m[n] for n in WEIGHTS], *[new_v[n] for n in WEIGHTS])


def kernel(x, mem, mix_norm_g, pool_w, pool_scale, sb_w_qkv, sb_w_o, s5_a_re, s5_a_im, s5_log_dt, s5_b_re, s5_b_im,
           s5_c_re, s5_c_im, s5_d, s5_w_glu, xa_norm_g, mem_norm_g, xa_wq, xa_wkv, xa_wo, ffn_norm_g, ffn_w_up,
           ffn_conv_w, ffn_conv_b, ffn_w_down, final_norm_g, loss_target, m_mix_norm_g, m_pool_w, m_pool_scale,
           m_sb_w_qkv, m_sb_w_o, m_s5_a_re, m_s5_a_im, m_s5_log_dt, m_s5_b_re, m_s5_b_im, m_s5_c_re, m_s5_c_im,
           m_s5_d, m_s5_w_glu, m_xa_norm_g, m_mem_norm_g, m_xa_wq, m_xa_wkv, m_xa_wo, m_ffn_norm_g, m_ffn_w_up,
           m_ffn_conv_w, m_ffn_conv_b, m_ffn_w_down, m_final_norm_g, v_mix_norm_g, v_pool_w, v_pool_scale,
           v_sb_w_qkv, v_sb_w_o, v_s5_a_re, v_s5_a_im, v_s5_log_dt, v_s5_b_re, v_s5_b_im, v_s5_c_re, v_s5_c_im,
           v_s5_d, v_s5_w_glu, v_xa_norm_g, v_mem_norm_g, v_xa_wq, v_xa_wkv, v_xa_wo, v_ffn_norm_g, v_ffn_w_up,
           v_ffn_conv_w, v_ffn_conv_b, v_ffn_w_down, v_final_norm_g):
    given = dict(locals())
    w = {n: given[n] for n in WEIGHTS}
    m = {n: given["m_" + n] for n in WEIGHTS}
    v = {n: given["v_" + n] for n in WEIGHTS}
    out = _step(x[0], mem[0], loss_target[0], w, m, v)
    return (out[0], out[1][None], *out[2:])
```

```python
import functools
import math

import jax
import jax.numpy as jnp
from jax import lax
from jax.experimental import pallas as pl
from jax.experimental.pallas import tpu as pltpu

F32 = jnp.float32
BF16 = jnp.bfloat16
MESH = pl.DeviceIdType.MESH

EPS = 1e-6
POOL_WINDOWS = (2, 4, 8, 16)
POOL_HALO = 128
POOL_TILE = 256
SB_HEAD_DIM = 64
SB_BLOCK = 128
S5_GROUP = 16
S5_STATE = 64
S5_GROUPS_PER_BLOCK = 8
XA_HEADS = 4
CONV_WIDTH = 3
ADAM_LR, ADAM_B1, ADAM_B2, ADAM_EPS, ADAM_WD, ADAM_STEP = 0.001, 0.9, 0.999, 1e-08, 0.01, 10

V7X_VMEM_BYTES = 64 * 1024 * 1024
LANE = 128
SUBLANE = 8


def _vmem_limit(block_bytes):
    want = 2 * block_bytes + 16 * 1024 * 1024
    return int(min(V7X_VMEM_BYTES - 6 * 1024 * 1024, max(32 * 1024 * 1024, want)))


def _params(sem, block_bytes):
    return pltpu.CompilerParams(dimension_semantics=sem, vmem_limit_bytes=_vmem_limit(block_bytes))


def _tile(n, cap, align=LANE):
    if n <= cap:
        return n
    t = (cap // align) * align
    while t >= align:
        if n % t == 0:
            return t
        t -= align
    return n


def _nbytes(shape, dtype):
    return math.prod(shape) * jnp.dtype(dtype).itemsize


def mm(a, b, *, ta=False, tb=False, out_dtype=F32, res=None, b_col0=None, name, tm=1024, tn=1024, tk=1024):
    if ta:
        K, M = a.shape
    else:
        M, K = a.shape
    if tb:
        N, Kb = b.shape
    else:
        Kb, N = b.shape
    if b_col0 is None:
        assert K == Kb, (a.shape, b.shape, ta, tb)
    tm, tn, tk = _tile(M, tm), _tile(N, tn), _tile(K, tk)
    nk = K // tk
    koff = 0
    if b_col0 is not None:
        assert tb and b_col0 % tk == 0 and b_col0 + K <= Kb
        koff = b_col0 // tk
    dims = (((0,) if ta else (1,), (1,) if tb else (0,)), ((), ()))

    def body(*refs):
        if res is None:
            a_ref, b_ref, o_ref, acc = refs
            r_ref = None
        else:
            a_ref, b_ref, r_ref, o_ref, acc = refs
        k = pl.program_id(2)

        @pl.when(k == 0)
        def _():
            acc[...] = jnp.zeros_like(acc)

        acc[...] += lax.dot_general(a_ref[...].astype(BF16), b_ref[...].astype(BF16), dims,
                                    preferred_element_type=F32)

        @pl.when(k == nk - 1)
        def _():
            r = acc[...]
            if r_ref is not None:
                r = r + r_ref[...].astype(F32)
            o_ref[...] = r.astype(out_dtype)

    a_spec = pl.BlockSpec((tk, tm), lambda i, j, k: (k, i)) if ta else pl.BlockSpec((tm, tk), lambda i, j, k: (i, k))
    b_spec = pl.BlockSpec((tn, tk), lambda i, j, k: (j, k + koff)) if tb else pl.BlockSpec((tk, tn), lambda i, j, k: (k, j))
    in_specs = [a_spec, b_spec]
    args = [a, b]
    blk = _nbytes((tm, tk), a.dtype) + _nbytes((tk, tn), b.dtype) + _nbytes((tm, tn), out_dtype)
    if res is not None:
        in_specs.append(pl.BlockSpec((tm, tn), lambda i, j, k: (i, j)))
        args.append(res)
        blk += _nbytes((tm, tn), res.dtype)
    return pl.pallas_call(
        body, grid=(M // tm, N // tn, nk), in_specs=in_specs,
        out_specs=pl.BlockSpec((tm, tn), lambda i, j, k: (i, j)),
        out_shape=jax.ShapeDtypeStruct((M, N), out_dtype),
        scratch_shapes=[pltpu.VMEM((tm, tn), F32)], name=name,
        compiler_params=_params(("parallel", "parallel", "arbitrary"), blk + _nbytes((tm, tn), F32)),
    )(*args)


def bdmm(a, w, *, out_dtype, name, tm=512):
    M = a.shape[0]
    nb, ka, kn = w.shape
    tm = _tile(M, tm)

    def body(a_ref, w_ref, o_ref):
        o_ref[...] = jnp.dot(a_ref[...].astype(BF16), w_ref[...].astype(BF16),
                             preferred_element_type=F32).astype(out_dtype)

    blk = _nbytes((tm, ka), a.dtype) + _nbytes((ka, kn), w.dtype) + _nbytes((tm, kn), out_dtype)
    return pl.pallas_call(
        body, grid=(M // tm, nb),
        in_specs=[pl.BlockSpec((tm, ka), lambda i, b: (i, b)), pl.BlockSpec((None, ka, kn), lambda i, b: (b, 0, 0))],
        out_specs=pl.BlockSpec((tm, kn), lambda i, b: (i, b)),
        out_shape=jax.ShapeDtypeStruct((M, nb * kn), out_dtype), name=name,
        compiler_params=_params(("parallel", "parallel"), blk),
    )(a, w)


def bdmm_tn(a, d, *, ka, kd, name, tm=512):
    M = a.shape[0]
    nb = a.shape[1] // ka
    assert d.shape[1] == nb * kd
    tm = _tile(M, tm)

    def body(a_ref, d_ref, o_ref):
        i = pl.program_id(1)
        v = lax.dot_general(a_ref[...].astype(BF16), d_ref[...].astype(BF16), (((0,), (0,)), ((), ())),
                            preferred_element_type=F32)

        @pl.when(i == 0)
        def _():
            o_ref[...] = v

        @pl.when(i > 0)
        def _():
            o_ref[...] += v

    blk = _nbytes((tm, ka), a.dtype) + _nbytes((tm, kd), d.dtype) + _nbytes((ka, kd), F32)
    return pl.pallas_call(
        body, grid=(nb, M // tm),
        in_specs=[pl.BlockSpec((tm, ka), lambda b, i: (i, b)), pl.BlockSpec((tm, kd), lambda b, i: (i, b))],
        out_specs=pl.BlockSpec((None, ka, kd), lambda b, i: (b, 0, 0)),
        out_shape=jax.ShapeDtypeStruct((nb, ka, kd), F32), name=name,
        compiler_params=_params(("parallel", "arbitrary"), blk),
    )(a, d)


def ew(fn, ins, outs, *, rows, tr, name):
    n = rows // tr
    assert n * tr == rows
    in_specs, args, blk = [], [], 0
    for a, kind in ins:
        if kind == "tile":
            assert a.shape[0] == rows, (name, a.shape, rows)
            in_specs.append(pl.BlockSpec((tr, a.shape[1]), lambda i: (i, 0)))
            blk += _nbytes((tr, a.shape[1]), a.dtype)
        else:
            in_specs.append(pl.BlockSpec(a.shape, lambda i, nd=a.ndim: (0,) * nd))
            blk += _nbytes(a.shape, a.dtype)
        args.append(a)
    out_shape, out_specs = [], []
    for c, dt, kind in outs:
        if kind == "tile":
            out_shape.append(jax.ShapeDtypeStruct((rows, c), dt))
            out_specs.append(pl.BlockSpec((tr, c), lambda i: (i, 0)))
            blk += _nbytes((tr, c), dt)
        else:
            out_shape.append(jax.ShapeDtypeStruct((1, c), dt))
            out_specs.append(pl.BlockSpec((1, c), lambda i: (0, 0)))
    nin = len(ins)

    def body(*refs):
        i = pl.program_id(0)
        vals = fn(i, *[r[...] for r in refs[:nin]])
        for (c, dt, kind), o, v in zip(outs, refs[nin:], vals):
            if kind == "tile":
                o[...] = v.astype(dt)
            else:
                @pl.when(i == 0)
                def _():
                    o[...] = v.astype(dt)

                @pl.when(i > 0)
                def _():
                    o[...] += v.astype(dt)

    has_acc = any(k == "acc" for _, _, k in outs)
    return pl.pallas_call(
        body, grid=(n,), in_specs=in_specs, out_specs=out_specs, out_shape=out_shape, name=name,
        compiler_params=_params(("arbitrary" if has_acc else "parallel",), 3 * blk),
    )(*args)


def _colsum(x):
    return jnp.sum(x, axis=0, keepdims=True)


def rms_fwd(x, g, *, out_dtype, name, tr=256):
    def fn(i, xv, gv):
        r = lax.rsqrt(jnp.mean(xv * xv, axis=-1, keepdims=True) + EPS)
        return [xv * r * gv]

    return ew(fn, [(x, "tile"), (g, "full")], [(x.shape[1], out_dtype, "tile")], rows=x.shape[0], tr=tr, name=name)[0]


def rms_bwd(x, g, dy, dres, *, name, tr=256):
    def fn(i, xv, gv, dyv, drv):
        dyv = dyv.astype(F32)
        r = lax.rsqrt(jnp.mean(xv * xv, axis=-1, keepdims=True) + EPS)
        xh = xv * r
        gy = dyv * gv
        dx = r * (gy - xh * jnp.mean(gy * xh, axis=-1, keepdims=True))
        return [dx + drv, _colsum(dyv * xh)]

    D = x.shape[1]
    return ew(fn, [(x, "tile"), (g, "full"), (dy, "tile"), (dres, "tile")], [(D, F32, "tile"), (D, F32, "acc")],
              rows=x.shape[0], tr=tr, name=name)


def rms_bwd_g(x, dy, *, name, tr=256):
    def fn(i, xv, dyv):
        r = lax.rsqrt(jnp.mean(xv * xv, axis=-1, keepdims=True) + EPS)
        return [_colsum(dyv.astype(F32) * xv * r)]

    return ew(fn, [(x, "tile"), (dy, "tile")], [(x.shape[1], F32, "acc")], rows=x.shape[0],
              tr=_tile(x.shape[0], tr, SUBLANE), name=name)[0]


def loss_head(h, g, target, *, name, tr=256):
    D = h.shape[1]

    def fn(i, xv, gv, tv):
        r = lax.rsqrt(jnp.mean(xv * xv, axis=-1, keepdims=True) + EPS)
        xh = xv * r
        err = xh * gv - tv
        dy = err * (1.0 / D)
        gy = dy * gv
        dx = r * (gy - xh * jnp.mean(gy * xh, axis=-1, keepdims=True))
        part = _colsum(err * err) * (0.5 / D)
        return [dx, _colsum(dy * xh), jnp.sum(part, axis=1, keepdims=True)]

    return ew(fn, [(h, "tile"), (g, "full"), (target, "tile")], [(D, F32, "tile"), (D, F32, "acc"), (1, F32, "acc")],
              rows=h.shape[0], tr=tr, name=name)


def _as2d(a):
    if a.ndim >= 2 and a.shape[-1] >= LANE:
        return a.reshape(-1, a.shape[-1])
    if a.size % (8 * LANE) == 0:
        return a.reshape(-1, 8 * LANE)
    return a.reshape(1, -1)


def adamw(w, g, m, v, *, name):
    shape = w.shape
    w2, g2, m2, v2 = (_as2d(t) for t in (w, g.astype(F32).reshape(shape), m, v))
    R, C = w2.shape
    tr = R
    if R * C * 4 > (1 << 20):
        tr = _tile(R, max(SUBLANE, (1 << 20) // (C * 4) // SUBLANE * SUBLANE), SUBLANE)
    c1 = 1.0 / (1.0 - ADAM_B1 ** ADAM_STEP)
    c2 = 1.0 / (1.0 - ADAM_B2 ** ADAM_STEP)

    def fn(i, wv, gv, mv, vv):
        mn = ADAM_B1 * mv + (1.0 - ADAM_B1) * gv
        vn = ADAM_B2 * vv + (1.0 - ADAM_B2) * (gv * gv)
        delta = -ADAM_LR * ((mn * c1) / (jnp.sqrt(vn * c2) + ADAM_EPS) + ADAM_WD * wv)
        return [delta, mn, vn]

    d, mn, vn = ew(fn, [(w2, "tile"), (g2, "tile"), (m2, "tile"), (v2, "tile")], [(C, F32, "tile")] * 3,
                   rows=R, tr=tr, name=name)
    return d.reshape(shape), mn.reshape(shape), vn.reshape(shape)


def _split_bf16(x):
    hi = x.astype(BF16)
    return hi, (x - hi.astype(F32)).astype(BF16)


def _dot2(band, x):
    hi, lo = _split_bf16(x)
    return jnp.dot(band, hi, preferred_element_type=F32) + jnp.dot(band, lo, preferred_element_type=F32)


def _pool_fwd_window(xm, xh, r0, win):
    T = xm.shape[0]
    t = r0 + lax.broadcasted_iota(jnp.int32, (T, 1), 0)
    s_main = r0 + lax.broadcasted_iota(jnp.int32, (1, T), 1)
    s_halo = r0 - POOL_HALO + lax.broadcasted_iota(jnp.int32, (1, POOL_HALO), 1)
    band_m = ((s_main <= t) & (s_main > t - win)).astype(BF16)
    band_h = ((s_halo > t - win) & (s_halo >= 0)).astype(BF16)
    ws = _dot2(band_m, xm) + _dot2(band_h, xh)
    cnt = jnp.minimum(t + 1, win).astype(F32)
    return ws / cnt - xm


def _pool_bwd_window(dm, dh, r0, win, S):
    T = dm.shape[0]
    s = r0 + lax.broadcasted_iota(jnp.int32, (T, 1), 0)
    t_main = r0 + lax.broadcasted_iota(jnp.int32, (1, T), 1)
    t_halo = r0 + T + lax.broadcasted_iota(jnp.int32, (1, POOL_HALO), 1)
    band_m = ((t_main >= s) & (t_main < s + win)).astype(BF16)
    band_h = ((t_halo < s + win) & (t_halo < S)).astype(BF16)
    tm_col = r0 + lax.broadcasted_iota(jnp.int32, (T, 1), 0)
    th_col = r0 + T + lax.broadcasted_iota(jnp.int32, (POOL_HALO, 1), 0)
    dmc = dm / jnp.minimum(tm_col + 1, win).astype(F32)
    dhc = dh / jnp.minimum(th_col + 1, win).astype(F32)
    return _dot2(band_m, dmc) + _dot2(band_h, dhc) - dm


def _pool_specs(T, Cg, order):
    per = T // POOL_HALO
    if order == "ig":
        return (pl.BlockSpec((T, Cg), lambda i, g: (i, g)),
                pl.BlockSpec((POOL_HALO, Cg), lambda i, g: (jnp.maximum(i * per - 1, 0), g)))
    return (pl.BlockSpec((T, Cg), lambda g, i: (i, g)),
            pl.BlockSpec((POOL_HALO, Cg), lambda g, i: (jnp.maximum(i * per - 1, 0), g)))


def pool_fwd(hn, h, w, scale, *, name):
    S, D = hn.shape
    G, Cg, _ = w.shape
    T = _tile(S, POOL_TILE)

    def body(xm_ref, xh_ref, h_ref, w_ref, sc_ref, o_ref):
        i, g = pl.program_id(0), pl.program_id(1)
        win = jnp.left_shift(2, g)
        p = _pool_fwd_window(xm_ref[...], xh_ref[...], i * T, win)
        y = jnp.dot(p.astype(BF16), w_ref[...], preferred_element_type=F32)
        o_ref[...] = h_ref[...] + y * sc_ref[...]

    main, halo = _pool_specs(T, Cg, "ig")
    return pl.pallas_call(
        body, grid=(S // T, G),
        in_specs=[main, halo, main, pl.BlockSpec((None, Cg, Cg), lambda i, g: (g, 0, 0)),
                  pl.BlockSpec((1, Cg), lambda i, g: (0, g))],
        out_specs=main, out_shape=jax.ShapeDtypeStruct((S, D), F32), name=name,
        compiler_params=_params(("parallel", "parallel"), 4 * T * Cg * 4),
    )(hn, hn, h, w, scale)


def pool_bwd_w(hn, dt, w, scale, *, name):
    S, D = hn.shape
    G, Cg, _ = w.shape
    T = _tile(S, POOL_TILE)

    def body(xm_ref, xh_ref, dt_ref, w_ref, sc_ref, dp_ref, dw_ref, ds_ref):
        g, i = pl.program_id(0), pl.program_id(1)
        win = jnp.left_shift(2, g)
        p = _pool_fwd_window(xm_ref[...], xh_ref[...], i * T, win).astype(BF16)
        dtv = dt_ref[...]
        ypre = jnp.dot(p, w_ref[...], preferred_element_type=F32)
        dy = (dtv * sc_ref[...]).astype(BF16)
        dp_ref[...] = lax.dot_general(dy, w_ref[...], (((1,), (1,)), ((), ())), preferred_element_type=F32)
        dwv = lax.dot_general(p, dy, (((0,), (0,)), ((), ())), preferred_element_type=F32)
        dsv = _colsum(dtv * ypre)

        @pl.when(i == 0)
        def _():
            dw_ref[...] = dwv
            ds_ref[...] = dsv

        @pl.when(i > 0)
        def _():
            dw_ref[...] += dwv
            ds_ref[...] += dsv

    main, halo = _pool_specs(T, Cg, "gi")
    return pl.pallas_call(
        body, grid=(G, S // T),
        in_specs=[main, halo, main, pl.BlockSpec((None, Cg, Cg), lambda g, i: (g, 0, 0)),
                  pl.BlockSpec((1, Cg), lambda g, i: (0, g))],
        out_specs=[main, pl.BlockSpec((None, Cg, Cg), lambda g, i: (g, 0, 0)), pl.BlockSpec((1, Cg), lambda g, i: (0, g))],
        out_shape=[jax.ShapeDtypeStruct((S, D), F32), jax.ShapeDtypeStruct((G, Cg, Cg), F32),
                   jax.ShapeDtypeStruct((1, D), F32)], name=name,
        compiler_params=_params(("parallel", "arbitrary"), 4 * T * Cg * 4),
    )(hn, hn, dt, w, scale)


def pool_bwd_x(dp, G, *, name):
    S, D = dp.shape
    Cg = D // G
    T = _tile(S, POOL_TILE)
    per = T // POOL_HALO
    last = S // POOL_HALO - 1

    def body(dm_ref, dh_ref, o_ref):
        i, g = pl.program_id(0), pl.program_id(1)
        o_ref[...] = _pool_bwd_window(dm_ref[...], dh_ref[...], i * T, jnp.left_shift(2, g), S)

    main = pl.BlockSpec((T, Cg), lambda i, g: (i, g))
    return pl.pallas_call(
        body, grid=(S // T, G),
        in_specs=[main, pl.BlockSpec((POOL_HALO, Cg), lambda i, g: (jnp.minimum((i + 1) * per, last), g))],
        out_specs=main, out_shape=jax.ShapeDtypeStruct((S, D), F32), name=name,
        compiler_params=_params(("parallel", "parallel"), 3 * T * Cg * 4),
    )(dp, dp)


def _sb_logs(z, mask):
    e = jnp.exp(-jnp.abs(z))
    sp = jnp.log(1.0 + e)
    ls = jnp.minimum(z, 0.0) - sp
    lsn = jnp.where(mask, jnp.minimum(-z, 0.0) - sp, 0.0)
    return ls, lsn, e


def _dot2r(x, band):
    hi, lo = _split_bf16(x)
    return jnp.dot(hi, band, preferred_element_type=F32) + jnp.dot(lo, band, preferred_element_type=F32)


def _dot3r(x, band):
    hi = x.astype(BF16)
    r1 = x - hi.astype(F32)
    mid = r1.astype(BF16)
    lo = (r1 - mid.astype(F32)).astype(BF16)
    return (jnp.dot(hi, band, preferred_element_type=F32) + jnp.dot(mid, band, preferred_element_type=F32)
            + jnp.dot(lo, band, preferred_element_type=F32))


def _head_masks(n_lanes):
    lane = lax.broadcasted_iota(jnp.int32, (1, n_lanes), 1)
    return [((lane >= h * SB_HEAD_DIM) & (lane < (h + 1) * SB_HEAD_DIM)) for h in range(n_lanes // SB_HEAD_DIM)]


_NT = (((1,), (1,)), ((), ()))
_TN = (((0,), (0,)), ((), ()))


SB_UNROLL = 4


def _sb_unroll(S):
    return SB_UNROLL if S % (SB_UNROLL * SB_BLOCK) == 0 else 1


SB_DEAD_LOG = -105.0


def _sb_loop(n_steps, step, init, *, rs_at):
    def cond(state):
        n, alive, _ = state
        return jnp.logical_and(n < n_steps, alive > 0)

    def body(state):
        n, _, carry = state
        carry = step(n, carry)
        top = functools.reduce(jnp.maximum, [jnp.max(r) for r in carry[rs_at]])
        return n + 1, (top > SB_DEAD_LOG).astype(jnp.int32), carry

    return lax.while_loop(cond, body, (jnp.int32(0), jnp.int32(1), init))[2]


def sb_fwd(qkv, *, name):
    S, D3 = qkv.shape
    D = D3 // 3
    B = SB_BLOCK
    npair = D // LANE
    scale = SB_HEAD_DIM ** -0.5

    U = _sb_unroll(S)

    def body(q_ref, k_ref, v_ref, o_ref):
        i = pl.program_id(1)
        masks = _head_masks(LANE)
        q = q_ref[...] * scale
        qh = [jnp.where(m, q, jnp.zeros_like(q)) for m in masks]
        row = lax.broadcasted_iota(jnp.int32, (B, B), 0)
        col = lax.broadcasted_iota(jnp.int32, (B, B), 1)
        upper = (row > col).astype(BF16)
        diag = col < row
        nsuper = i // U + 1

        def step(n, carry):
            os_, rs = list(carry[0]), list(carry[1])
            sup = nsuper - 1 - n
            base = pl.multiple_of(sup * (U * B), U * B)
            kbig = k_ref[pl.ds(base, U * B), :]
            vbig = v_ref[pl.ds(base, U * B), :]
            tiles = [(c, hd) for c in reversed(range(U)) for hd in range(len(masks))]
            kb = {c: kbig[c * B:(c + 1) * B] for c in range(U)}
            vb = {c: vbig[c * B:(c + 1) * B] for c in range(U)}
            mask = {c: jnp.logical_or(sup * U + c < i, jnp.logical_and(sup * U + c == i, diag)) for c in range(U)}
            z = {t: lax.dot_general(qh[t[1]], kb[t[0]], _NT, preferred_element_type=F32) for t in tiles}
            ls, lsn = {}, {}
            for t in tiles:
                ls[t], lsn[t], _ = _sb_logs(z[t], mask[t[0]])
            local = {t: _dot2r(lsn[t], upper) for t in tiles}
            for c, hd in tiles:
                a = jnp.where(mask[c], jnp.exp(ls[(c, hd)] + local[(c, hd)] + rs[hd]), 0.0)
                os_[hd] = os_[hd] + jnp.dot(a.astype(BF16), vb[c], preferred_element_type=F32)
                rs[hd] = rs[hd] + jnp.sum(lsn[(c, hd)], axis=1, keepdims=True)
            return tuple(os_), tuple(rs)

        zero = jnp.zeros((B, 1), F32)
        zacc = jnp.zeros((B, LANE), F32)
        os_, _ = _sb_loop(nsuper, step, (tuple(zacc for _ in masks), tuple(zero for _ in masks)), rs_at=1)
        o = jnp.zeros((B, LANE), F32)
        for m, oh in zip(masks, os_):
            o = jnp.where(m, oh, o)
        o_ref[...] = o

    return pl.pallas_call(
        body, grid=(npair, S // B),
        in_specs=[pl.BlockSpec((B, LANE), lambda p, i: (i, p)),
                  pl.BlockSpec((S, LANE), lambda p, i: (0, npair + p)),
                  pl.BlockSpec((S, LANE), lambda p, i: (0, 2 * npair + p))],
        out_specs=pl.BlockSpec((B, LANE), lambda p, i: (i, p)),
        out_shape=jax.ShapeDtypeStruct((S, D), F32), name=name,
        compiler_params=_params(("parallel", "arbitrary"), 2 * S * LANE * 2),
    )(qkv, qkv, qkv)


def sb_bwd(qkv, o, do, *, name):
    S, D3 = qkv.shape
    D = D3 // 3
    B = SB_BLOCK
    npair = D // LANE
    scale = SB_HEAD_DIM ** -0.5

    U = _sb_unroll(S)

    def body(q_ref, k_ref, v_ref, o_ref, do_ref, dq_ref, dk_ref, dv_ref):
        i = pl.program_id(1)

        @pl.when(i == 0)
        def _():
            dk_ref[...] = jnp.zeros_like(dk_ref)
            dv_ref[...] = jnp.zeros_like(dv_ref)

        masks = _head_masks(LANE)
        q = q_ref[...] * scale
        dov = do_ref[...]
        ov = o_ref[...]
        qh = [jnp.where(m, q, jnp.zeros_like(q)) for m in masks]
        doh = [jnp.where(m, dov, 0.0).astype(BF16) for m in masks]
        gsum = [jnp.sum(dh_.astype(F32) * ov, axis=1, keepdims=True) for dh_ in doh]
        row = lax.broadcasted_iota(jnp.int32, (B, B), 0)
        col = lax.broadcasted_iota(jnp.int32, (B, B), 1)
        upper = (row > col).astype(BF16)
        upper_incl = (row >= col).astype(BF16)
        diag = col < row
        nsuper = i // U + 1

        def step(n, carry):
            dqs, rs, gs = list(carry[0]), list(carry[1]), list(carry[2])
            sup = nsuper - 1 - n
            base = pl.multiple_of(sup * (U * B), U * B)
            kbig = k_ref[pl.ds(base, U * B), :]
            vbig = v_ref[pl.ds(base, U * B), :]
            nh = len(masks)
            tiles = [(c, hd) for c in reversed(range(U)) for hd in range(nh)]
            kb = {c: kbig[c * B:(c + 1) * B] for c in range(U)}
            vb = {c: vbig[c * B:(c + 1) * B] for c in range(U)}
            mask = {c: jnp.logical_or(sup * U + c < i, jnp.logical_and(sup * U + c == i, diag)) for c in range(U)}
            z = {t: lax.dot_general(qh[t[1]], kb[t[0]], _NT, preferred_element_type=F32) for t in tiles}
            da = {t: lax.dot_general(doh[t[1]], vb[t[0]], _NT, preferred_element_type=F32) for t in tiles}
            ls, lsn, sig = {}, {}, {}
            for t in tiles:
                ls[t], lsn[t], e = _sb_logs(z[t], mask[t[0]])
                sig[t] = jnp.exp(ls[t])
            local = {t: _dot2r(lsn[t], upper) for t in tiles}
            ab, g = {}, {}
            for c, hd in tiles:
                a = jnp.where(mask[c], jnp.exp(ls[(c, hd)] + local[(c, hd)] + rs[hd]), 0.0)
                ab[(c, hd)] = a.astype(BF16)
                g[(c, hd)] = ab[(c, hd)].astype(F32) * da[(c, hd)]
                rs[hd] = rs[hd] + jnp.sum(lsn[(c, hd)], axis=1, keepdims=True)
            glocal = {t: _dot3r(g[t], upper_incl) for t in tiles}
            dzb = {}
            for c, hd in tiles:
                t = (c, hd)
                sg = glocal[t] + gs[hd]
                dzb[t] = jnp.where(mask[c], g[t] * (1.0 - sig[t]) - (gsum[hd] - sg) * sig[t], 0.0).astype(BF16)
                gs[hd] = gs[hd] + jnp.sum(g[t], axis=1, keepdims=True)
            for c, hd in tiles:
                dqs[hd] = dqs[hd] + jnp.dot(dzb[(c, hd)], kb[c], preferred_element_type=F32)
            for c in reversed(range(U)):
                dkb = sum(lax.dot_general(dzb[(c, hd)], qh[hd], _TN, preferred_element_type=F32) for hd in range(nh))
                dvb = sum(lax.dot_general(ab[(c, hd)], doh[hd], _TN, preferred_element_type=F32) for hd in range(nh))
                off = pl.multiple_of(base + c * B, B)
                dk_ref[pl.ds(off, B), :] += dkb
                dv_ref[pl.ds(off, B), :] += dvb
            return tuple(dqs), tuple(rs), tuple(gs)

        zero = jnp.zeros((B, 1), F32)
        zs = tuple(zero for _ in masks)
        zacc = jnp.zeros((B, LANE), F32)
        dqs, _, _ = _sb_loop(nsuper, step, (tuple(zacc for _ in masks), zs, zs), rs_at=1)
        dq = jnp.zeros((B, LANE), F32)
        for m, dqh in zip(masks, dqs):
            dq = jnp.where(m, dqh, dq)
        dq_ref[...] = dq * scale

    tile = pl.BlockSpec((B, LANE), lambda p, i: (i, p))
    strip = pl.BlockSpec((S, LANE), lambda p, i: (0, p))
    return pl.pallas_call(
        body, grid=(npair, S // B),
        in_specs=[tile, pl.BlockSpec((S, LANE), lambda p, i: (0, npair + p)),
                  pl.BlockSpec((S, LANE), lambda p, i: (0, 2 * npair + p)), tile, tile],
        out_specs=[tile, strip, strip],
        out_shape=[jax.ShapeDtypeStruct((S, D), F32)] * 3, name=name,
        compiler_params=_params(("parallel", "arbitrary"), 2 * S * LANE * 2 + 2 * S * LANE * 4),
    )(qkv, qkv, qkv, o, do)


S5_HALF = S5_GROUPS_PER_BLOCK * S5_STATE
S5_BLOCK = 2 * S5_HALF


def s5_scan(bu, pw, *, reverse, name, tr=512):
    S, W = bu.shape
    nb = W // S5_BLOCK
    tr = _tile(S, tr, SUBLANE)
    nsub = tr // SUBLANE
    nt = S // tr
    H = S5_HALF

    def body(bu_ref, pw_ref, x_ref, st_re, st_im):
        i = pl.program_id(1)

        @pl.when(i == 0)
        def _():
            st_re[...] = jnp.zeros_like(st_re)
            st_im[...] = jnp.zeros_like(st_im)

        row = lax.broadcasted_iota(jnp.int32, (SUBLANE, H), 0)
        steps = []
        for k, sh in enumerate((1, 2, 4)):
            valid = (row < SUBLANE - sh) if reverse else (row >= sh)
            steps.append((sh, valid, pw_ref[SUBLANE + k:SUBLANE + k + 1, 0:H], pw_ref[SUBLANE + k:SUBLANE + k + 1, H:2 * H]))
        ap_re = pw_ref[0:SUBLANE, 0:H]
        ap_im = pw_ref[0:SUBLANE, H:2 * H]
        edge = (row == 0) if reverse else (row == SUBLANE - 1)

        def sub(n, carry):
            s_re, s_im = carry
            j = (nsub - 1 - n) if reverse else n
            off = pl.multiple_of(j * SUBLANE, SUBLANE)
            r = bu_ref[pl.ds(off, SUBLANE), 0:H]
            m = bu_ref[pl.ds(off, SUBLANE), H:2 * H]
            for sh, valid, a_re, a_im in steps:
                amt = (SUBLANE - sh) if reverse else sh
                rs = jnp.where(valid, pltpu.roll(r, amt, 0), 0.0)
                ms = jnp.where(valid, pltpu.roll(m, amt, 0), 0.0)
                r, m = r + a_re * rs - a_im * ms, m + a_re * ms + a_im * rs
            r, m = r + ap_re * s_re - ap_im * s_im, m + ap_re * s_im + ap_im * s_re
            x_ref[pl.ds(off, SUBLANE), 0:H] = r
            x_ref[pl.ds(off, SUBLANE), H:2 * H] = m
            return (jnp.sum(jnp.where(edge, r, 0.0), axis=0, keepdims=True),
                    jnp.sum(jnp.where(edge, m, 0.0), axis=0, keepdims=True))

        s_re, s_im = lax.fori_loop(0, nsub, sub, (st_re[...], st_im[...]), unroll=4)
        st_re[...] = s_re
        st_im[...] = s_im

    if reverse:
        tile = pl.BlockSpec((tr, S5_BLOCK), lambda b, i: (nt - 1 - i, b))
    else:
        tile = pl.BlockSpec((tr, S5_BLOCK), lambda b, i: (i, b))
    return pl.pallas_call(
        body, grid=(nb, nt),
        in_specs=[tile, pl.BlockSpec((2 * SUBLANE, S5_BLOCK), lambda b, i: (0, b))],
        out_specs=tile, out_shape=jax.ShapeDtypeStruct((S, W), F32),
        scratch_shapes=[pltpu.VMEM((1, H), F32), pltpu.VMEM((1, H), F32)], name=name,
        compiler_params=_params(("parallel", "arbitrary"), 2 * tr * S5_BLOCK * 4),
    )(bu, pw)


def s5_da(lam, x, *, name, tr=512):
    S, W = lam.shape
    nb = W // S5_BLOCK
    tr = _tile(S, tr, SUBLANE)
    nsub = tr // SUBLANE
    nt = S // tr
    H = S5_HALF

    def body(l_ref, x_ref, xh_ref, o_ref, acc_re, acc_im):
        i = pl.program_id(1)

        @pl.when(i == 0)
        def _():
            acc_re[...] = jnp.zeros_like(acc_re)
            acc_im[...] = jnp.zeros_like(acc_im)

        row = lax.broadcasted_iota(jnp.int32, (SUBLANE, H), 0)
        first = row == 0

        def sub(n, carry):
            a_re, a_im = carry
            off = pl.multiple_of(n * SUBLANE, SUBLANE)
            poff = pl.multiple_of(jnp.maximum(n - 1, 0) * SUBLANE, SUBLANE)
            inside = n > 0
            start = jnp.logical_and(i == 0, n == 0)
            out = []
            for lo in (0, H):
                cur = x_ref[pl.ds(off, SUBLANE), lo:lo + H]
                prv = jnp.where(inside, x_ref[pl.ds(poff, SUBLANE), lo:lo + H], xh_ref[:, lo:lo + H])
                xs = jnp.where(first, pltpu.roll(prv, 1, 0), pltpu.roll(cur, 1, 0))
                out.append(jnp.where(jnp.logical_and(start, first), 0.0, xs))
            xs_re, xs_im = out
            l_re = l_ref[pl.ds(off, SUBLANE), 0:H]
            l_im = l_ref[pl.ds(off, SUBLANE), H:2 * H]
            return a_re + l_re * xs_re + l_im * xs_im, a_im + l_im * xs_re - l_re * xs_im

        a_re, a_im = lax.fori_loop(0, nsub, sub, (acc_re[...], acc_im[...]), unroll=4)
        acc_re[...] = a_re
        acc_im[...] = a_im

        @pl.when(i == nt - 1)
        def _():
            o_ref[:, 0:H] = jnp.sum(a_re, axis=0, keepdims=True)
            o_ref[:, H:2 * H] = jnp.sum(a_im, axis=0, keepdims=True)

    per = tr // SUBLANE
    tile = pl.BlockSpec((tr, S5_BLOCK), lambda b, i: (i, b))
    return pl.pallas_call(
        body, grid=(nb, nt),
        in_specs=[tile, tile, pl.BlockSpec((SUBLANE, S5_BLOCK), lambda b, i: (jnp.maximum(i * per - 1, 0), b))],
        out_specs=pl.BlockSpec((1, S5_BLOCK), lambda b, i: (0, b)),
        out_shape=jax.ShapeDtypeStruct((1, W), F32),
        scratch_shapes=[pltpu.VMEM((SUBLANE, H), F32), pltpu.VMEM((SUBLANE, H), F32)], name=name,
        compiler_params=_params(("parallel", "arbitrary"), 2 * tr * S5_BLOCK * 4),
    )(lam, x, x)


def _gelu(y):
    c = math.sqrt(2.0 / math.pi)
    return 0.5 * y * (1.0 + jnp.tanh(c * (y + 0.044715 * y * y * y)))


def _gelu_grad(y):
    c = math.sqrt(2.0 / math.pi)
    th = jnp.tanh(c * (y + 0.044715 * y * y * y))
    return 0.5 * (1.0 + th) + 0.5 * y * (1.0 - th * th) * c * (1.0 + 3.0 * 0.044715 * y * y)


def _sigmoid(x):
    return 0.5 + 0.5 * jnp.tanh(0.5 * x)


def _xa_probs(qh, kh, scale):
    s = lax.dot_general(qh, kh, _NT, preferred_element_type=F32) * scale
    p = jnp.exp(s - jnp.max(s, axis=-1, keepdims=True))
    return p / jnp.sum(p, axis=-1, keepdims=True)


def xa_fwd(q, kv, *, name, tm=512):
    S, D = q.shape
    M = kv.shape[0]
    dh = D // XA_HEADS
    scale = dh ** -0.5
    tm = _tile(S, tm)

    def body(q_ref, kv_ref, o_ref):
        for h in range(XA_HEADS):
            p = _xa_probs(q_ref[:, h * dh:(h + 1) * dh], kv_ref[:, h * dh:(h + 1) * dh], scale)
            o_ref[:, h * dh:(h + 1) * dh] = jnp.dot(p.astype(BF16), kv_ref[:, D + h * dh:D + (h + 1) * dh],
                                                   preferred_element_type=F32).astype(BF16)

    return pl.pallas_call(
        body, grid=(S // tm,),
        in_specs=[pl.BlockSpec((tm, D), lambda i: (i, 0)), pl.BlockSpec((M, 2 * D), lambda i: (0, 0))],
        out_specs=pl.BlockSpec((tm, D), lambda i: (i, 0)),
        out_shape=jax.ShapeDtypeStruct((S, D), BF16), name=name,
        compiler_params=_params(("parallel",), 2 * tm * D * 2 + M * 2 * D * 2),
    )(q, kv)


def xa_bwd(q, kv, do, *, name, tm=512):
    S, D = q.shape
    M = kv.shape[0]
    dh = D // XA_HEADS
    scale = dh ** -0.5
    tm = _tile(S, tm)

    def body(q_ref, kv_ref, do_ref, dq_ref, dkv_ref):
        i = pl.program_id(0)

        @pl.when(i == 0)
        def _():
            dkv_ref[...] = jnp.zeros_like(dkv_ref)

        for h in range(XA_HEADS):
            sl = slice(h * dh, (h + 1) * dh)
            vsl = slice(D + h * dh, D + (h + 1) * dh)
            qh, kh, vh = q_ref[:, sl], kv_ref[:, sl], kv_ref[:, vsl]
            doh = do_ref[:, sl].astype(BF16)
            p = _xa_probs(qh, kh, scale)
            dp = lax.dot_general(doh, vh, _NT, preferred_element_type=F32)
            ds = (p * (dp - jnp.sum(dp * p, axis=-1, keepdims=True)) * scale).astype(BF16)
            dq_ref[:, sl] = jnp.dot(ds, kh, preferred_element_type=F32).astype(BF16)
            dkv_ref[:, sl] += lax.dot_general(ds, qh, _TN, preferred_element_type=F32)
            dkv_ref[:, vsl] += lax.dot_general(p.astype(BF16), doh, _TN, preferred_element_type=F32)

    return pl.pallas_call(
        body, grid=(S // tm,),
        in_specs=[pl.BlockSpec((tm, D), lambda i: (i, 0)), pl.BlockSpec((M, 2 * D), lambda i: (0, 0)),
                  pl.BlockSpec((tm, D), lambda i: (i, 0))],
        out_specs=[pl.BlockSpec((tm, D), lambda i: (i, 0)), pl.BlockSpec((M, 2 * D), lambda i: (0, 0))],
        out_shape=[jax.ShapeDtypeStruct((S, D), BF16), jax.ShapeDtypeStruct((M, 2 * D), F32)], name=name,
        compiler_params=_params(("arbitrary",), 3 * tm * D * 2 + M * 2 * D * 6),
    )(q, kv, do)


FFN_STRIP = 256
FFN_ROWS = 512


def _shift_down(x, k):
    return pltpu.roll(x, k, 0)


def _shift_up(x, k):
    return pltpu.roll(x, x.shape[0] - k, 0)


FFN_HALO = 2 * SUBLANE


def _rows_with_prev(u_ref, r0, R):
    cur = u_ref[pl.ds(r0, R), :].astype(F32)
    p0 = pl.multiple_of(jnp.maximum(r0 - FFN_HALO, 0), FFN_HALO)
    prev = jnp.where(r0 > 0, u_ref[pl.ds(p0, FFN_HALO), :].astype(F32), 0.0)
    return jnp.concatenate([prev, cur], axis=0), cur


def _conv_rows(u_ref, r0, R, w_ref, b_ref):
    ext, _ = _rows_with_prev(u_ref, r0, R)
    out = w_ref[2:3, :] * ext + w_ref[1:2, :] * _shift_down(ext, 1) + w_ref[0:1, :] * _shift_down(ext, 2) + b_ref[...]
    return out[FFN_HALO:, :]


def ffn_act_fwd(u, conv_w, conv_b, *, name):
    S, F2 = u.shape
    F = F2 // 2
    tc = _tile(F, FFN_STRIP)
    nc = F // tc
    R = _tile(S, FFN_ROWS, 16)

    def body(uv_ref, ug_ref, wv_ref, wg_ref, bv_ref, bg_ref, o_ref):
        def rows(n, _):
            r0 = pl.multiple_of(n * R, R)
            val = _conv_rows(uv_ref, r0, R, wv_ref, bv_ref)
            gate = _conv_rows(ug_ref, r0, R, wg_ref, bg_ref)
            o_ref[pl.ds(r0, R), :] = (gate * _sigmoid(gate) * val).astype(BF16)
            return 0

        lax.fori_loop(0, S // R, rows, 0)

    return pl.pallas_call(
        body, grid=(nc,),
        in_specs=[pl.BlockSpec((S, tc), lambda c: (0, c)), pl.BlockSpec((S, tc), lambda c: (0, nc + c)),
                  pl.BlockSpec((CONV_WIDTH, tc), lambda c: (0, c)), pl.BlockSpec((CONV_WIDTH, tc), lambda c: (0, nc + c)),
                  pl.BlockSpec((1, tc), lambda c: (0, c)), pl.BlockSpec((1, tc), lambda c: (0, nc + c))],
        out_specs=pl.BlockSpec((S, tc), lambda c: (0, c)),
        out_shape=jax.ShapeDtypeStruct((S, F), BF16), name=name,
        compiler_params=_params(("parallel",), 3 * S * tc * 2),
    )(u, u, conv_w, conv_w, conv_b, conv_b)


def ffn_act_bwd(u, dact, conv_w, conv_b, *, name):
    S, F2 = u.shape
    F = F2 // 2
    tc = _tile(F, FFN_STRIP)
    nc = F // tc
    R = _tile(S, FFN_ROWS, 16)
    nr = S // R
    HALO = FFN_HALO

    def body(uv_ref, ug_ref, da_ref, wv_ref, wg_ref, bv_ref, bg_ref,
             duv_ref, dug_ref, dwv_ref, dwg_ref, dbv_ref, dbg_ref, dcv, dcg):
        def p1(n, _):
            r0 = pl.multiple_of(n * R, R)
            val = _conv_rows(uv_ref, r0, R, wv_ref, bv_ref)
            gate = _conv_rows(ug_ref, r0, R, wg_ref, bg_ref)
            d = da_ref[pl.ds(r0, R), :].astype(F32)
            sg = _sigmoid(gate)
            dcv[pl.ds(r0, R), :] = d * gate * sg
            dcg[pl.ds(r0, R), :] = d * val * (sg + gate * sg * (1.0 - sg))
            return 0

        lax.fori_loop(0, nr, p1, 0)

        def p2(n, carry):
            r0 = pl.multiple_of(n * R, R)
            nxt = pl.multiple_of(jnp.minimum(r0 + R, S - HALO), HALO)
            new = []
            for u_ref, dc, w_ref, du_ref, acc in ((uv_ref, dcv, wv_ref, duv_ref, carry[0]),
                                                  (ug_ref, dcg, wg_ref, dug_ref, carry[1])):
                d = dc[pl.ds(r0, R), :]
                after = jnp.where(r0 + R < S, dc[pl.ds(nxt, HALO), :], 0.0)
                ext = jnp.concatenate([d, after], axis=0)
                du = w_ref[2:3, :] * ext + w_ref[1:2, :] * _shift_up(ext, 1) + w_ref[0:1, :] * _shift_up(ext, 2)
                du_ref[pl.ds(r0, R), :] = du[:R, :].astype(BF16)
                uext, cur = _rows_with_prev(u_ref, r0, R)
                u1 = _shift_down(uext, 1)[HALO:, :]
                u2 = _shift_down(uext, 2)[HALO:, :]
                dw2, dw1, dw0, db = acc
                new.append((dw2 + _colsum(d * cur), dw1 + _colsum(d * u1), dw0 + _colsum(d * u2), db + _colsum(d)))
            return tuple(new)

        z = jnp.zeros((1, tc), F32)
        accs = lax.fori_loop(0, nr, p2, ((z, z, z, z), (z, z, z, z)))
        for (dw2, dw1, dw0, db), dw_ref, db_ref in ((accs[0], dwv_ref, dbv_ref), (accs[1], dwg_ref, dbg_ref)):
            dw_ref[0:1, :] = dw0
            dw_ref[1:2, :] = dw1
            dw_ref[2:3, :] = dw2
            db_ref[...] = db

    strip_v = pl.BlockSpec((S, tc), lambda c: (0, c))
    strip_g = pl.BlockSpec((S, tc), lambda c: (0, nc + c))
    w_v = pl.BlockSpec((CONV_WIDTH, tc), lambda c: (0, c))
    w_g = pl.BlockSpec((CONV_WIDTH, tc), lambda c: (0, nc + c))
    b_v = pl.BlockSpec((1, tc), lambda c: (0, c))
    b_g = pl.BlockSpec((1, tc), lambda c: (0, nc + c))
    outs = pl.pallas_call(
        body, grid=(nc,),
        in_specs=[strip_v, strip_g, strip_v, w_v, w_g, b_v, b_g],
        out_specs=[strip_v, strip_v, w_v, w_v, b_v, b_v],
        out_shape=[jax.ShapeDtypeStruct((S, F), BF16), jax.ShapeDtypeStruct((S, F), BF16),
                   jax.ShapeDtypeStruct((CONV_WIDTH, F), F32), jax.ShapeDtypeStruct((CONV_WIDTH, F), F32),
                   jax.ShapeDtypeStruct((1, F), F32), jax.ShapeDtypeStruct((1, F), F32)],
        scratch_shapes=[pltpu.VMEM((S, tc), F32), pltpu.VMEM((S, tc), F32)], name=name,
        compiler_params=_params(("parallel",), 5 * S * tc * 2 + S * tc * 4),
    )(u, u, dact, conv_w, conv_w, conv_b, conv_b)
    duv, dug, dwv, dwg, dbv, dbg = outs
    return duv, dug, jnp.concatenate([dwv, dwg], axis=1), jnp.concatenate([dbv, dbg], axis=1)


def _s5_discretize(a_re, a_im, log_dt, b_re, b_im):
    lam = lax.complex(a_re, a_im)
    dt_lam = lam * jnp.exp(log_dt)[:, None]
    a_bar = jnp.exp(dt_lam)
    b_bar = ((a_bar - 1.0) / lam)[..., None] * lax.complex(b_re, b_im)
    return jnp.real(a_bar), jnp.imag(a_bar), jnp.real(b_bar), jnp.imag(b_bar)


def _s5_cols(z_re, z_im, nb):
    lead = z_re.shape[:-2]
    re = z_re.reshape(*lead, nb, S5_HALF)
    im = z_im.reshape(*lead, nb, S5_HALF)
    return jnp.concatenate([re, im], axis=-1).reshape(*lead, nb * S5_BLOCK)


def _s5_powers(a_re, a_im, log_dt, nb, *, reverse):
    dt_lam = lax.complex(a_re, a_im) * jnp.exp(log_dt)[:, None]
    if reverse:
        dt_lam = jnp.conj(dt_lam)
        carry = jnp.arange(SUBLANE, 0, -1, dtype=F32)
    else:
        carry = jnp.arange(1, SUBLANE + 1, dtype=F32)
    ks = jnp.concatenate([carry, jnp.array([1.0, 2.0, 4.0], F32), jnp.zeros((SUBLANE - 3,), F32)])
    pw = jnp.exp(ks[:, None, None] * dt_lam[None])
    return _s5_cols(jnp.real(pw), jnp.imag(pw), nb)


def _s5_in_weights(bb_re, bb_im, nb):
    eye = jnp.eye(S5_GROUPS_PER_BLOCK, dtype=F32)
    G, P, Cg = bb_re.shape

    def one(bb):
        t = jnp.einsum("bgpi,gh->bgihp", bb.reshape(nb, S5_GROUPS_PER_BLOCK, P, Cg), eye)
        return t.reshape(nb, S5_GROUPS_PER_BLOCK * Cg, S5_HALF)

    return jnp.concatenate([one(bb_re), one(bb_im)], axis=2)


def _s5_out_weights(c_re, c_im, nb):
    eye = jnp.eye(S5_GROUPS_PER_BLOCK, dtype=F32)
    G, Cg, P = c_re.shape

    def one(c):
        t = jnp.einsum("bgip,gh->bgphi", c.reshape(nb, S5_GROUPS_PER_BLOCK, Cg, P), eye)
        return t.reshape(nb, S5_HALF, S5_GROUPS_PER_BLOCK * Cg)

    return jnp.concatenate([one(c_re), -one(c_im)], axis=1)


def _s5_in_weight_grads(dwb, Cg):
    nb = dwb.shape[0]
    eye = jnp.eye(S5_GROUPS_PER_BLOCK, dtype=F32)
    t = dwb.reshape(nb, S5_GROUPS_PER_BLOCK, Cg, 2, S5_GROUPS_PER_BLOCK, S5_STATE)
    out = jnp.einsum("bgirhp,gh->rbgpi", t, eye)
    return out[0].reshape(-1, S5_STATE, Cg), out[1].reshape(-1, S5_STATE, Cg)


def _s5_out_weight_grads(dwc, Cg):
    nb = dwc.shape[0]
    eye = jnp.eye(S5_GROUPS_PER_BLOCK, dtype=F32)
    t = dwc.reshape(nb, 2, S5_GROUPS_PER_BLOCK, S5_STATE, S5_GROUPS_PER_BLOCK, Cg)
    out = jnp.einsum("brgphi,gh->rbgip", t, eye)
    return out[0].reshape(-1, Cg, S5_STATE), -out[1].reshape(-1, Cg, S5_STATE)


ANY = pl.BlockSpec(memory_space=pl.ANY)
N_CHIPS = 4
N_DEV = 8


def _place():
    x, y, c = lax.axis_index("x"), lax.axis_index("y"), lax.axis_index("c")
    chips = [(1 - x, y), (x, 1 - y), (1 - x, 1 - y)]
    return x, y, c, chips


def gather_chips(w, *, name):
    R, C = w.shape
    Hh = R // 2
    assert 2 * Hh == R

    def body(w_ref, out_ref, send_sems, recv_sems):
        x, y, c, chips = _place()
        me = 2 * x + y
        sibling = (x, y, 1 - c)

        def half(chip, hc):
            return out_ref.at[chip, pl.ds(hc * Hh, Hh), :]

        def copy(k, src, dst, to):
            return pltpu.make_async_remote_copy(src_ref=src, dst_ref=dst, send_sem=send_sems.at[k],
                                                recv_sem=recv_sems.at[k], device_id=to, device_id_type=MESH)

        first = [copy(j, w_ref.at[pl.ds(c * Hh, Hh), :], half(me, c), (px, py, c)) for j, (px, py) in enumerate(chips)]
        for cp in first:
            cp.start()
        passed = []
        for j, (px, py) in enumerate(chips):
            landed = half(2 * px + py, c)
            copy(j, landed, landed, (px, py, c)).wait_recv()
            fwd = copy(3 + j, landed, landed, sibling)
            fwd.start()
            passed.append(fwd)
        for j, (px, py) in enumerate(chips):
            theirs = half(2 * px + py, 1 - c)
            copy(3 + j, theirs, theirs, sibling).wait_recv()
        for cp in first + passed:
            cp.wait_send()

    others = pl.pallas_call(
        body, in_specs=[ANY], out_specs=ANY, out_shape=jax.ShapeDtypeStruct((N_CHIPS, R, C), w.dtype),
        scratch_shapes=[pltpu.SemaphoreType.DMA((6,)), pltpu.SemaphoreType.DMA((6,))], name=name,
    )(w)
    return _place_rows(others, w, 2 * lax.axis_index("x") + lax.axis_index("y"), name=name + "_own")


def _place_rows(buf, rows, slot, *, name, tr=592):
    n, R, C = buf.shape
    tr = _tile(R, tr, 16)
    idx = jnp.asarray(slot, jnp.int32).reshape(1)

    def body(s_ref, r_ref, b_ref, o_ref):
        o_ref[...] = r_ref[...]

    return pl.pallas_call(
        body,
        grid_spec=pltpu.PrefetchScalarGridSpec(
            num_scalar_prefetch=1, grid=(R // tr,),
            in_specs=[pl.BlockSpec((tr, C), lambda i, s: (i, 0)), ANY],
            out_specs=pl.BlockSpec((None, tr, C), lambda i, s: (s[0], i, 0))),
        out_shape=jax.ShapeDtypeStruct(buf.shape, buf.dtype), input_output_aliases={2: 0}, name=name,
        compiler_params=_params(("parallel",), 2 * tr * C * 4),
    )(idx, rows, buf)


def swap_halves(g4, *, name):
    n, R, C = g4.shape
    Hh = R // 2

    def body(g_ref, out_ref, send_sem, recv_sem):
        x, y, c, _ = _place()
        cp = pltpu.make_async_remote_copy(
            src_ref=g_ref.at[pl.ds(0, n), pl.ds((1 - c) * Hh, Hh), :], dst_ref=out_ref, send_sem=send_sem,
            recv_sem=recv_sem, device_id=(x, y, 1 - c), device_id_type=MESH)
        cp.start()
        cp.wait()

    return pl.pallas_call(
        body, in_specs=[ANY], out_specs=ANY, out_shape=jax.ShapeDtypeStruct((n, Hh, C), g4.dtype),
        scratch_shapes=[pltpu.SemaphoreType.DMA, pltpu.SemaphoreType.DMA], name=name,
    )(g4)


def add_half(g4, other, *, name, tr=160):
    n, R, C = g4.shape
    Hh = R // 2
    tr = _tile(Hh, tr, 16)
    nblk = Hh // tr
    cidx = lax.axis_index("c").astype(jnp.int32).reshape(1)

    def body(c_ref, g_ref, o_ref, out_ref):
        out_ref[...] = (g_ref[...].astype(F32) + o_ref[...].astype(F32)).astype(out_ref.dtype)

    return pl.pallas_call(
        body,
        grid_spec=pltpu.PrefetchScalarGridSpec(
            num_scalar_prefetch=1, grid=(nblk,),
            in_specs=[pl.BlockSpec((n, tr, C), lambda i, c_ref: (0, c_ref[0] * nblk + i, 0)),
                      pl.BlockSpec((n, tr, C), lambda i, c_ref: (0, i, 0))],
            out_specs=pl.BlockSpec((n, tr, C), lambda i, c_ref: (0, i, 0))),
        out_shape=jax.ShapeDtypeStruct((n, Hh, C), g4.dtype), name=name,
        compiler_params=_params(("parallel",), 3 * n * tr * C * 2),
    )(cidx, g4, other)


def scatter_chips(p4, *, name):
    n, Hh, C = p4.shape

    def body(p_ref, out_ref, send_sems, recv_sems):
        x, y, c, chips = _place()
        me = 2 * x + y
        sends = []
        for j, (px, py) in enumerate(chips):
            cp = pltpu.make_async_remote_copy(src_ref=p_ref.at[2 * px + py], dst_ref=out_ref.at[me], send_sem=send_sems.at[j],
                                              recv_sem=recv_sems.at[j], device_id=(px, py, c), device_id_type=MESH)
            cp.start()
            sends.append(cp)
        for j, (px, py) in enumerate(chips):
            slot = out_ref.at[2 * px + py]
            pltpu.make_async_remote_copy(src_ref=slot, dst_ref=slot, send_sem=send_sems.at[j], recv_sem=recv_sems.at[j],
                                         device_id=(px, py, c), device_id_type=MESH).wait_recv()
        for cp in sends:
            cp.wait_send()

    return pl.pallas_call(
        body, in_specs=[ANY], out_specs=ANY, out_shape=jax.ShapeDtypeStruct((n, Hh, C), p4.dtype),
        scratch_shapes=[pltpu.SemaphoreType.DMA((3,)), pltpu.SemaphoreType.DMA((3,))], name=name,
    )(p4)


def sum_chips(landed, part, *, name, tr=96):
    n, Hh, C = landed.shape
    tr = _tile(Hh, tr, 16)
    nblk = Hh // tr
    x, y, c, _ = _place()
    idx = jnp.stack([2 * x + y, c]).astype(jnp.int32)

    def slot_spec(k):
        return pl.BlockSpec((None, tr, C), lambda i, s: (jnp.where(s[0] == k, (k + 1) % n, k), i, 0))

    def body(s_ref, *refs):
        slots, own_ref, o_ref = refs[:n], refs[n], refs[n + 1]
        own = own_ref[...].astype(F32)
        acc = None
        for k in range(n):
            v = jnp.where(s_ref[0] == k, own, slots[k][...].astype(F32))
            acc = v if acc is None else acc + v
        o_ref[...] = acc

    return pl.pallas_call(
        body,
        grid_spec=pltpu.PrefetchScalarGridSpec(
            num_scalar_prefetch=1, grid=(nblk,),
            in_specs=[slot_spec(k) for k in range(n)] + [pl.BlockSpec((None, tr, C), lambda i, s: (s[0], i, 0))],
            out_specs=pl.BlockSpec((tr, C), lambda i, s: (s[1] * nblk + i, 0))),
        out_shape=jax.ShapeDtypeStruct((2 * Hh, C), F32), name=name,
        compiler_params=_params(("parallel",), 6 * tr * C * 4),
    )(idx, *([landed] * n), part)


def sum_leading(x3, *, name, tr=160, align=16):
    n, R, C = x3.shape
    tr = _tile(R, tr, align)

    def body(x_ref, o_ref):
        acc = x_ref[0].astype(F32)
        for k in range(1, n):
            acc = acc + x_ref[k].astype(F32)
        o_ref[...] = acc

    return pl.pallas_call(
        body, grid=(R // tr,), in_specs=[pl.BlockSpec((n, tr, C), lambda i: (0, i, 0))],
        out_specs=pl.BlockSpec((tr, C), lambda i: (i, 0)), out_shape=jax.ShapeDtypeStruct((R, C), F32), name=name,
        compiler_params=_params(("parallel",), n * tr * C * 4 + tr * C * 4),
    )(x3)


def join_halves(r, *, name):
    R, C = r.shape
    Hh = R // 2

    def body(r_ref, out_ref, send_sem, recv_sem):
        x, y, c, _ = _place()
        mine = out_ref.at[pl.ds(c * Hh, Hh), :]
        theirs = out_ref.at[pl.ds((1 - c) * Hh, Hh), :]
        cp = pltpu.make_async_remote_copy(src_ref=mine, dst_ref=mine, send_sem=send_sem, recv_sem=recv_sem,
                                          device_id=(x, y, 1 - c), device_id_type=MESH)
        cp.start()
        pltpu.make_async_remote_copy(src_ref=theirs, dst_ref=theirs, send_sem=send_sem, recv_sem=recv_sem,
                                     device_id=(x, y, 1 - c), device_id_type=MESH).wait_recv()
        cp.wait_send()

    return pl.pallas_call(
        body, in_specs=[ANY], out_specs=ANY, out_shape=jax.ShapeDtypeStruct((R, C), r.dtype),
        input_output_aliases={0: 0}, scratch_shapes=[pltpu.SemaphoreType.DMA, pltpu.SemaphoreType.DMA], name=name,
    )(r)


def gather_devices(v, *, name):
    m_per, n = v.shape

    def body(x_ref, out_ref, send_sems, recv_sems, local_sem):
        x, y, c, chips = _place()
        me, sibling = (x, y, c), (x, y, 1 - c)

        def rows(px, py, pc):
            return out_ref.at[pl.ds((4 * px + 2 * py + pc) * m_per, m_per), :]

        def copy(k, block, to, src=None):
            return pltpu.make_async_remote_copy(src_ref=rows(*block) if src is None else src, dst_ref=rows(*block),
                                                send_sem=send_sems.at[k], recv_sem=recv_sems.at[k], device_id=to,
                                                device_id_type=MESH)

        mine = pltpu.make_async_copy(x_ref, rows(*me), local_sem)
        mine.start()
        first = [copy(0, me, sibling, src=x_ref)]
        first += [copy(1 + j, me, (*chip, c), src=x_ref) for j, chip in enumerate(chips)]
        for cp in first:
            cp.start()
        passed = [copy(4 + j, (*chip, c), sibling) for j, chip in enumerate(chips)]
        for j, chip in enumerate(chips):
            copy(1 + j, (*chip, c), me).wait_recv()
            passed[j].start()
        copy(0, sibling, me).wait_recv()
        for j, chip in enumerate(chips):
            copy(4 + j, (*chip, 1 - c), me).wait_recv()
        for cp in first + passed:
            cp.wait_send()
        mine.wait()

    return pl.pallas_call(
        body, out_shape=jax.ShapeDtypeStruct((N_DEV * m_per, n), v.dtype),
        in_specs=[pl.BlockSpec(memory_space=pltpu.VMEM)], out_specs=pl.BlockSpec(memory_space=pltpu.VMEM),
        scratch_shapes=[pltpu.SemaphoreType.DMA((7,)), pltpu.SemaphoreType.DMA((7,)), pltpu.SemaphoreType.DMA], name=name,
        compiler_params=pltpu.CompilerParams(vmem_limit_bytes=_vmem_limit(9 * m_per * n * 4)),
    )(v)


def reduce_weight_grads(g4):
    other = swap_halves(g4, name="rs_swap_halves")
    part = add_half(g4, other, name="rs_add_half")
    landed = scatter_chips(part, name="rs_scatter_chips")
    mine = sum_chips(landed, part, name="rs_sum_chips")
    return join_halves(mine, name="rs_join_halves")


PACK_COLS = 1024
BIG = (("pool_w", 2), ("sb_w_qkv", 2), ("sb_w_o", 1), ("s5_w_glu", 2), ("xa_wq", 1), ("xa_wkv", 2), ("xa_wo", 1),
       ("ffn_w_up", 2), ("ffn_w_down", 1))
SMALL_SHARDED = (("pool_scale", 1), ("s5_d", 1), ("ffn_conv_w", 2))
REPLICATED = ("mix_norm_g", "s5_a_re", "s5_a_im", "s5_log_dt", "s5_b_re", "s5_b_im", "s5_c_re", "s5_c_im",
              "xa_norm_g", "mem_norm_g", "ffn_norm_g", "ffn_conv_b", "final_norm_g")
WEIGHTS = ("mix_norm_g", "pool_w", "pool_scale", "sb_w_qkv", "sb_w_o", "s5_a_re", "s5_a_im", "s5_log_dt", "s5_b_re",
           "s5_b_im", "s5_c_re", "s5_c_im", "s5_d", "s5_w_glu", "xa_norm_g", "mem_norm_g", "xa_wq", "xa_wkv", "xa_wo",
           "ffn_norm_g", "ffn_w_up", "ffn_conv_w", "ffn_conv_b", "ffn_w_down", "final_norm_g")


def _pack_rows(parts, row_align):
    flat = jnp.concatenate(parts, axis=-1)
    n = flat.shape[-1]
    per = PACK_COLS * row_align
    padded = -(-n // per) * per
    if padded != n:
        flat = jnp.pad(flat, [(0, 0)] * (flat.ndim - 1) + [(0, padded - n)])
    return flat.reshape(*flat.shape[:-1], padded // PACK_COLS, PACK_COLS)


def _to_natural(g, ax):
    g = jnp.moveaxis(g, 0, ax)
    sh = g.shape
    return g.reshape(*sh[:ax], sh[ax] * sh[ax + 1], *sh[ax + 2:])


def _to_chunks(a, ax, n=N_CHIPS):
    sh = a.shape
    a = a.reshape(*sh[:ax], n, sh[ax] // n, *sh[ax + 1:])
    return jnp.moveaxis(a, ax, 0).reshape(n, -1)


def _unpack(flat, shapes):
    out, off = [], 0
    for sh in shapes:
        n = math.prod(sh)
        out.append(flat[..., off:off + n].reshape(*flat.shape[:-1], *sh))
        off += n
    return out


def _mixer_kind(i):
    return i % 3, i // 3


def _s5_forward(hn, h, s5, tag):
    bu = bdmm(hn, s5["w_in"], out_dtype=F32, name=f"{tag}_s5_bu")
    xs = s5_scan(bu, s5["pw_fwd"], reverse=False, name=f"{tag}_s5_scan")
    ycx = bdmm(xs, s5["w_out"], out_dtype=F32, name=f"{tag}_s5_cx")
    D = hn.shape[1]

    def post(i, yv, uv, dv):
        return [_gelu(yv + dv * uv)]

    yg = ew(post, [(ycx, "tile"), (hn, "tile"), (s5["d"], "full")], [(D, BF16, "tile")], rows=hn.shape[0], tr=256,
            name=f"{tag}_s5_gelu")[0]
    vg = mm(yg, s5["w_glu"], out_dtype=F32, name=f"{tag}_s5_glu")

    def glu(i, vgv, hv):
        return [hv + vgv[:, :D] * _sigmoid(vgv[:, D:])]

    h1 = ew(glu, [(vg, "tile"), (h, "tile")], [(D, F32, "tile")], rows=hn.shape[0], tr=256, name=f"{tag}_s5_gate")[0]
    return h1, dict(xs=xs, ycx=ycx, yg=yg, vg=vg)


def _s5_backward(hn, dout, s5, sv, tag):
    S, D = hn.shape

    def dglu(i, vgv, dv):
        sg = _sigmoid(vgv[:, D:])
        return [jnp.concatenate([dv * sg, dv * vgv[:, :D] * sg * (1.0 - sg)], axis=1)]

    dvg = ew(dglu, [(sv["vg"], "tile"), (dout, "tile")], [(2 * D, BF16, "tile")], rows=S, tr=256, name=f"{tag}_s5_dgate")[0]
    dyg = mm(dvg, s5["w_glu"], tb=True, out_dtype=F32, name=f"{tag}_s5_dglu_x")
    dw_glu = mm(sv["yg"], dvg, ta=True, out_dtype=BF16, name=f"{tag}_s5_dglu_w")

    def dgelu(i, dygv, yv, uv, dv):
        dyp = dygv * _gelu_grad(yv + dv * uv)
        return [dyp, _colsum(dyp * uv)]

    dyp, dd = ew(dgelu, [(dyg, "tile"), (sv["ycx"], "tile"), (hn, "tile"), (s5["d"], "full")],
                 [(D, F32, "tile"), (D, F32, "acc")], rows=S, tr=256, name=f"{tag}_s5_dgelu")
    gx = bdmm(dyp, s5["w_out_t"], out_dtype=F32, name=f"{tag}_s5_dcx")
    dw_out = bdmm_tn(sv["xs"], dyp, ka=S5_BLOCK, kd=S5_GROUPS_PER_BLOCK * S5_GROUP, name=f"{tag}_s5_dwout")
    lam = s5_scan(gx, s5["pw_bwd"], reverse=True, name=f"{tag}_s5_scan_bwd")
    da = s5_da(lam, sv["xs"], name=f"{tag}_s5_da")
    dw_in = bdmm_tn(hn, lam, ka=S5_GROUPS_PER_BLOCK * S5_GROUP, kd=S5_BLOCK, name=f"{tag}_s5_dwin")
    du = bdmm(lam, s5["w_in_t"], out_dtype=F32, name=f"{tag}_s5_du")

    def dsum(i, duv, dypv, dv):
        return [duv + dypv * dv]

    dhn = ew(dsum, [(du, "tile"), (dyp, "tile"), (s5["d"], "full")], [(D, F32, "tile")], rows=S, tr=256,
             name=f"{tag}_s5_dhn")[0]
    return dhn, dict(dw_glu=dw_glu, dd=dd, dw_out=dw_out, dw_in=dw_in, da=da)


def _step(x, mem, target, w, m, v):
    S, D = x.shape
    depth = w["mix_norm_g"].shape[0]
    F = w["ffn_w_down"].shape[1] * N_CHIPS
    chip = 2 * lax.axis_index("x") + lax.axis_index("y")

    big_shapes = [w[n].shape for n, _ in BIG]
    packed = _pack_rows([w[n].astype(BF16).reshape(-1) for n, _ in BIG], 32)
    gathered = gather_chips(packed, name="ag_weights").reshape(N_CHIPS, -1)
    full = {n: _to_natural(p, ax) for (n, ax), p in zip(BIG, _unpack(gathered, big_shapes))}
    small_shapes = [w[n].shape for n, _ in SMALL_SHARDED]
    spacked = _pack_rows([w[n].reshape(-1) for n, _ in SMALL_SHARDED], SUBLANE)
    sgathered = gather_devices(spacked, name="ag_small").reshape(N_CHIPS, 2, -1)[:, 0]
    full.update({n: _to_natural(p, ax) for (n, ax), p in zip(SMALL_SHARDED, _unpack(sgathered, small_shapes))})

    n_s5 = w["s5_a_re"].shape[0]
    s5 = []
    for j in range(n_s5):
        G = w["s5_a_re"].shape[1]
        nb = G // S5_GROUPS_PER_BLOCK
        prm = (w["s5_a_re"][j], w["s5_a_im"][j], w["s5_log_dt"][j], w["s5_b_re"][j], w["s5_b_im"][j])
        (ab_re, ab_im, bb_re, bb_im), disc_vjp = jax.vjp(_s5_discretize, *prm)
        w_in = _s5_in_weights(bb_re, bb_im, nb)
        w_out = _s5_out_weights(w["s5_c_re"][j], w["s5_c_im"][j], nb)
        s5.append(dict(
            w_in=w_in, w_in_t=jnp.transpose(w_in, (0, 2, 1)), w_out=w_out, w_out_t=jnp.transpose(w_out, (0, 2, 1)),
            pw_fwd=_s5_powers(prm[0], prm[1], prm[2], nb, reverse=False),
            pw_bwd=_s5_powers(prm[0], prm[1], prm[2], nb, reverse=True),
            d=full["s5_d"][j][None], w_glu=full["s5_w_glu"][j], vjp=disc_vjp, nb=nb))

    h = x
    saved = []
    for i in range(depth):
        kind, j = _mixer_kind(i)
        tag = f"L{i}"
        sv = dict(h=h)
        g_mix = w["mix_norm_g"][i][None]
        if kind == 0:
            hn = rms_fwd(h, g_mix, out_dtype=F32, name=f"{tag}_mix_norm")
            h1 = pool_fwd(hn, h, full["pool_w"][j], full["pool_scale"][j][None], name=f"{tag}_pool")
        elif kind == 1:
            hn = rms_fwd(h, g_mix, out_dtype=BF16, name=f"{tag}_mix_norm")
            qkv = mm(hn, full["sb_w_qkv"][j], out_dtype=BF16, name=f"{tag}_sb_qkv")
            o = sb_fwd(qkv, name=f"{tag}_sb_attn")
            h1 = mm(o, full["sb_w_o"][j], res=h, name=f"{tag}_sb_out")
            sv.update(qkv=qkv, o=o)
        else:
            hn = rms_fwd(h, g_mix, out_dtype=F32, name=f"{tag}_mix_norm")
            h1, s5sv = _s5_forward(hn, h, s5[j], tag)
            sv.update(s5sv)
        sv.update(hn=hn, h1=h1)
        hq = rms_fwd(h1, w["xa_norm_g"][i][None], out_dtype=BF16, name=f"{tag}_xa_norm")
        memn = rms_fwd(mem, w["mem_norm_g"][i][None], out_dtype=BF16, name=f"{tag}_mem_norm", tr=mem.shape[0])
        q = mm(hq, full["xa_wq"][i], out_dtype=BF16, name=f"{tag}_xa_q")
        kv = mm(memn, full["xa_wkv"][i], out_dtype=BF16, name=f"{tag}_xa_kv")
        oa = xa_fwd(q, kv, name=f"{tag}_xa_attn")
        h2 = mm(oa, full["xa_wo"][i], res=h1, name=f"{tag}_xa_out")
        hf = rms_fwd(h2, w["ffn_norm_g"][i][None], out_dtype=BF16, name=f"{tag}_ffn_norm")
        uu = mm(hf, full["ffn_w_up"][i], out_dtype=BF16, tn=1408, name=f"{tag}_ffn_up")
        conv_w, conv_b = full["ffn_conv_w"][i], w["ffn_conv_b"][i][None]
        act = ffn_act_fwd(uu, conv_w, conv_b, name=f"{tag}_ffn_act")
        h3 = mm(act, full["ffn_w_down"][i], res=h2, tk=1408, name=f"{tag}_ffn_down")
        sv.update(hq=hq, memn=memn, q=q, kv=kv, oa=oa, h2=h2, hf=hf, uu=uu, act=act)
        saved.append(sv)
        h = h3

    dh, g_final, loss = loss_head(h, w["final_norm_g"][None], target, name="loss_head")

    gw = {n: [None] * w[n].shape[0] for n in WEIGHTS if n != "final_norm_g"}
    for i in reversed(range(depth)):
        kind, j = _mixer_kind(i)
        tag = f"L{i}b"
        sv = saved[i]
        conv_w, conv_b = full["ffn_conv_w"][i], w["ffn_conv_b"][i][None]
        dact = mm(dh, full["ffn_w_down"][i], tb=True, out_dtype=BF16, tn=1408, name=f"{tag}_ffn_down_x")
        gw["ffn_w_down"][i] = mm(sv["act"], dh, ta=True, out_dtype=BF16, tm=1408, name=f"{tag}_ffn_down_w")
        duv, dug, dcw, dcb = ffn_act_bwd(sv["uu"], dact, conv_w, conv_b, name=f"{tag}_ffn_act")
        gw["ffn_conv_w"][i], gw["ffn_conv_b"][i] = dcw, dcb[0]
        dhf = mm(duv, full["ffn_w_up"][i], tb=True, b_col0=0, tk=1408, name=f"{tag}_ffn_up_xv")
        dhf = mm(dug, full["ffn_w_up"][i], tb=True, b_col0=F, res=dhf, tk=1408, name=f"{tag}_ffn_up_xg")
        gw["ffn_w_up"][i] = (mm(sv["hf"], duv, ta=True, out_dtype=BF16, tn=1408, name=f"{tag}_ffn_up_wv"),
                             mm(sv["hf"], dug, ta=True, out_dtype=BF16, tn=1408, name=f"{tag}_ffn_up_wg"))
        dh2, dg = rms_bwd(sv["h2"], w["ffn_norm_g"][i][None], dhf, dh, name=f"{tag}_ffn_norm")
        gw["ffn_norm_g"][i] = dg[0]

        doa = mm(dh2, full["xa_wo"][i], tb=True, out_dtype=BF16, name=f"{tag}_xa_out_x")
        gw["xa_wo"][i] = mm(sv["oa"], dh2, ta=True, out_dtype=BF16, name=f"{tag}_xa_out_w")
        dq, dkv = xa_bwd(sv["q"], sv["kv"], doa, name=f"{tag}_xa_attn")
        dhq = mm(dq, full["xa_wq"][i], tb=True, name=f"{tag}_xa_q_x")
        gw["xa_wq"][i] = mm(sv["hq"], dq, ta=True, out_dtype=BF16, name=f"{tag}_xa_q_w")
        dmemn = mm(dkv, full["xa_wkv"][i], tb=True, name=f"{tag}_xa_kv_x")
        gw["xa_wkv"][i] = mm(sv["memn"], dkv, ta=True, out_dtype=BF16, name=f"{tag}_xa_kv_w")
        gw["mem_norm_g"][i] = rms_bwd_g(mem, dmemn, name=f"{tag}_mem_norm")[0]
        dh1, dg = rms_bwd(sv["h1"], w["xa_norm_g"][i][None], dhq, dh2, name=f"{tag}_xa_norm")
        gw["xa_norm_g"][i] = dg[0]

        g_mix = w["mix_norm_g"][i][None]
        if kind == 0:
            dp, dpw, dps = pool_bwd_w(sv["hn"], dh1, full["pool_w"][j], full["pool_scale"][j][None], name=f"{tag}_pool_w")
            gw["pool_w"][j], gw["pool_scale"][j] = dpw, dps[0]
            dhn = pool_bwd_x(dp, len(POOL_WINDOWS), name=f"{tag}_pool_x")
        elif kind == 1:
            do = mm(dh1, full["sb_w_o"][j], tb=True, name=f"{tag}_sb_out_x")
            gw["sb_w_o"][j] = mm(sv["o"], dh1, ta=True, out_dtype=BF16, name=f"{tag}_sb_out_w")
            dq3 = sb_bwd(sv["qkv"], sv["o"], do, name=f"{tag}_sb_attn")
            dqkv = jnp.concatenate([t.astype(BF16) for t in dq3], axis=1)
            dhn = mm(dqkv, full["sb_w_qkv"][j], tb=True, name=f"{tag}_sb_qkv_x")
            gw["sb_w_qkv"][j] = mm(sv["hn"], dqkv, ta=True, out_dtype=BF16, name=f"{tag}_sb_qkv_w")
        else:
            dhn, sg = _s5_backward(sv["hn"], dh1, s5[j], sv, tag)
            Cg = w["s5_b_re"].shape[-1]
            nb = s5[j]["nb"]
            gw["s5_w_glu"][j], gw["s5_d"][j] = sg["dw_glu"], sg["dd"][0]
            da = sg["da"].reshape(nb, 2, -1)
            gw["s5_a_re"][j], gw["s5_a_im"][j] = da[:, 0].reshape(-1, S5_STATE), da[:, 1].reshape(-1, S5_STATE)
            gw["s5_b_re"][j], gw["s5_b_im"][j] = _s5_in_weight_grads(sg["dw_in"], Cg)
            gw["s5_c_re"][j], gw["s5_c_im"][j] = _s5_out_weight_grads(sg["dw_out"], Cg)
        dh, dg = rms_bwd(sv["h"], g_mix, dhn, dh1, name=f"{tag}_mix_norm")
        gw["mix_norm_g"][i] = dg[0]
    grad_x = dh

    pieces = []
    for n, ax in BIG:
        for gl in gw[n]:
            if isinstance(gl, tuple):
                pieces.append(jnp.concatenate([_to_chunks(t.astype(BF16), ax - 1, N_CHIPS // len(gl)) for t in gl], axis=0))
            else:
                pieces.append(_to_chunks(gl.astype(BF16), ax - 1))
    g4 = _pack_rows(pieces, 32)
    reduced = reduce_weight_grads(g4).reshape(-1)
    grads = dict(zip([n for n, _ in BIG], _unpack(reduced, big_shapes)))

    s5_raw = ("s5_a_re", "s5_a_im", "s5_b_re", "s5_b_im")
    small_names = [n for n in REPLICATED if n not in ("final_norm_g", "s5_log_dt")] + [n for n, _ in SMALL_SHARDED]
    small_full = [jnp.stack(gw[n]).astype(F32) for n in small_names] + [g_final[0]]
    small_full_shapes = [t.shape for t in small_full]
    spk = _pack_rows([t.reshape(-1) for t in small_full], SUBLANE)
    everyone = gather_devices(spk, name="ar_small_gather").reshape(N_DEV, *spk.shape)
    ssum = sum_leading(everyone, name="ar_small_sum", align=SUBLANE).reshape(-1)
    small = dict(zip(small_names + ["final_norm_g"], _unpack(ssum, small_full_shapes)))
    per_layer = [[], [], [], [], []]
    for j in range(n_s5):
        ct = tuple(small[n][j] for n in s5_raw)
        for lst, gpart in zip(per_layer, s5[j]["vjp"]((ct[0], ct[1], ct[2], ct[3]))):
            lst.append(gpart)
    for n, lst in zip(("s5_a_re", "s5_a_im", "s5_log_dt", "s5_b_re", "s5_b_im"), per_layer):
        small[n] = jnp.stack(lst)
    for n, ax in SMALL_SHARDED:
        chunks = _to_chunks(small[n], ax)
        small[n] = lax.dynamic_index_in_dim(chunks, chip, 0, keepdims=False).reshape(w[n].shape)
    for n in REPLICATED:
        grads[n] = small[n].reshape(w[n].shape)
    for n, _ in SMALL_SHARDED:
        grads[n] = small[n]

    delta, new_m, new_v = {}, {}, {}
    for n in WEIGHTS:
        delta[n], new_m[n], new_v[n] = adamw(w[n], grads[n], m[n], v[n], name=f"adamw_{n}")
    total = lax.psum(loss[0, 0], ("x", "y", "c"))
    return (total, grad_x, *[grads[n] for n in WEIGHTS], *[delta[n] for n in WEIGHTS],
            *[new_m[n] for n in WEIGHTS], *[new_v[n] for n in WEIGHTS])


def kernel(x, mem, mix_norm_g, pool_w, pool_scale, sb_w_qkv, sb_w_o, s5_a_re, s5_a_im, s5_log_dt, s5_b_re, s5_b_im,
           s5_c_re, s5_c_im, s5_d, s5_w_glu, xa_norm_g, mem_norm_g, xa_wq, xa_wkv, xa_wo, ffn_norm_g, ffn_w_up,
           ffn_conv_w, ffn_conv_b, ffn_w_down, final_norm_g, loss_target, m_mix_norm_g, m_pool_w, m_pool_scale,
           m_sb_w_qkv, m_sb_w_o, m_s5_a_re, m_s5_a_im, m_s5_log_dt, m_s5_b_re, m_s5_b_im, m_s5_c_re, m_s5_c_im,
           m_s5_d, m_s5_w_glu, m_xa_norm_g, m_mem_norm_g, m_xa_wq, m_xa_wkv, m_xa_wo, m_ffn_norm_g, m_ffn_w_up,
           m_ffn_conv_w, m_ffn_conv_b, m_ffn_w_down, m_final_norm_g, v_mix_norm_g, v_pool_w, v_pool_scale,
           v_sb_w_qkv, v_sb_w_o, v_s5_a_re, v_s5_a_im, v_s5_log_dt, v_s5_b_re, v_s5_b_im, v_s5_c_re, v_s5_c_im,
           v_s5_d, v_s5_w_glu, v_xa_norm_g, v_mem_norm_g, v_xa_wq, v_xa_wkv, v_xa_wo, v_ffn_norm_g, v_ffn_w_up,
           v_ffn_conv_w, v_ffn_conv_b, v_ffn_w_down, v_final_norm_g):
    given = dict(locals())
    w = {n: given[n] for n in WEIGHTS}
    m = {n: given["m_" + n] for n in WEIGHTS}
    v = {n: given["v_" + n] for n in WEIGHTS}
    out = _step(x[0], mem[0], loss_target[0], w, m, v)
    return (out[0], out[1][None], *out[2:])
```

```python
import functools
import math

import jax
import jax.numpy as jnp
from jax import lax
from jax.experimental import pallas as pl
from jax.experimental.pallas import tpu as pltpu

F32 = jnp.float32
BF16 = jnp.bfloat16
MESH = pl.DeviceIdType.MESH

EPS = 1e-6
POOL_WINDOWS = (2, 4, 8, 16)
POOL_HALO = 128
POOL_TILE = 256
SB_HEAD_DIM = 64
SB_BLOCK = 128
S5_GROUP = 16
S5_STATE = 64
S5_GROUPS_PER_BLOCK = 8
XA_HEADS = 4
CONV_WIDTH = 3
ADAM_LR, ADAM_B1, ADAM_B2, ADAM_EPS, ADAM_WD, ADAM_STEP = 0.001, 0.9, 0.999, 1e-08, 0.01, 10

V7X_VMEM_BYTES = 64 * 1024 * 1024
LANE = 128
SUBLANE = 8


def _vmem_limit(block_bytes):
    want = 2 * block_bytes + 16 * 1024 * 1024
    return int(min(V7X_VMEM_BYTES - 6 * 1024 * 1024, max(32 * 1024 * 1024, want)))


def _params(sem, block_bytes):
    return pltpu.CompilerParams(dimension_semantics=sem, vmem_limit_bytes=_vmem_limit(block_bytes))


def _tile(n, cap, align=LANE):
    if n <= cap:
        return n
    t = (cap // align) * align
    while t >= align:
        if n % t == 0:
            return t
        t -= align
    return n


def _nbytes(shape, dtype):
    return math.prod(shape) * jnp.dtype(dtype).itemsize


def mm(a, b, *, ta=False, tb=False, out_dtype=F32, res=None, b_col0=None, name, tm=1024, tn=1024, tk=1024):
    if ta:
        K, M = a.shape
    else:
        M, K = a.shape
    if tb:
        N, Kb = b.shape
    else:
        Kb, N = b.shape
    if b_col0 is None:
        assert K == Kb, (a.shape, b.shape, ta, tb)
    tm, tn, tk = _tile(M, tm), _tile(N, tn), _tile(K, tk)
    nk = K // tk
    koff = 0
    if b_col0 is not None:
        assert tb and b_col0 % tk == 0 and b_col0 + K <= Kb
        koff = b_col0 // tk
    dims = (((0,) if ta else (1,), (1,) if tb else (0,)), ((), ()))

    def body(*refs):
        if res is None:
            a_ref, b_ref, o_ref, acc = refs
            r_ref = None
        else:
            a_ref, b_ref, r_ref, o_ref, acc = refs
        k = pl.program_id(2)

        @pl.when(k == 0)
        def _():
            acc[...] = jnp.zeros_like(acc)

        acc[...] += lax.dot_general(a_ref[...].astype(BF16), b_ref[...].astype(BF16), dims,
                                    preferred_element_type=F32)

        @pl.when(k == nk - 1)
        def _():
            r = acc[...]
            if r_ref is not None:
                r = r + r_ref[...].astype(F32)
            o_ref[...] = r.astype(out_dtype)

    a_spec = pl.BlockSpec((tk, tm), lambda i, j, k: (k, i)) if ta else pl.BlockSpec((tm, tk), lambda i, j, k: (i, k))
    b_spec = pl.BlockSpec((tn, tk), lambda i, j, k: (j, k + koff)) if tb else pl.BlockSpec((tk, tn), lambda i, j, k: (k, j))
    in_specs = [a_spec, b_spec]
    args = [a, b]
    blk = _nbytes((tm, tk), a.dtype) + _nbytes((tk, tn), b.dtype) + _nbytes((tm, tn), out_dtype)
    if res is not None:
        in_specs.append(pl.BlockSpec((tm, tn), lambda i, j, k: (i, j)))
        args.append(res)
        blk += _nbytes((tm, tn), res.dtype)
    return pl.pallas_call(
        body, grid=(M // tm, N // tn, nk), in_specs=in_specs,
        out_specs=pl.BlockSpec((tm, tn), lambda i, j, k: (i, j)),
        out_shape=jax.ShapeDtypeStruct((M, N), out_dtype),
        scratch_shapes=[pltpu.VMEM((tm, tn), F32)], name=name,
        compiler_params=_params(("parallel", "parallel", "arbitrary"), blk + _nbytes((tm, tn), F32)),
    )(*args)


def bdmm(a, w, *, out_dtype, name, tm=512):
    M = a.shape[0]
    nb, ka, kn = w.shape
    tm = _tile(M, tm)

    def body(a_ref, w_ref, o_ref):
        o_ref[...] = jnp.dot(a_ref[...].astype(BF16), w_ref[...].astype(BF16),
                             preferred_element_type=F32).astype(out_dtype)

    blk = _nbytes((tm, ka), a.dtype) + _nbytes((ka, kn), w.dtype) + _nbytes((tm, kn), out_dtype)
    return pl.pallas_call(
        body, grid=(M // tm, nb),
        in_specs=[pl.BlockSpec((tm, ka), lambda i, b: (i, b)), pl.BlockSpec((None, ka, kn), lambda i, b: (b, 0, 0))],
        out_specs=pl.BlockSpec((tm, kn), lambda i, b: (i, b)),
        out_shape=jax.ShapeDtypeStruct((M, nb * kn), out_dtype), name=name,
        compiler_params=_params(("parallel", "parallel"), blk),
    )(a, w)


def bdmm_tn(a, d, *, ka, kd, name, tm=512):
    M = a.shape[0]
    nb = a.shape[1] // ka
    assert d.shape[1] == nb * kd
    tm = _tile(M, tm)

    def body(a_ref, d_ref, o_ref):
        i = pl.program_id(1)
        v = lax.dot_general(a_ref[...].astype(BF16), d_ref[...].astype(BF16), (((0,), (0,)), ((), ())),
                            preferred_element_type=F32)

        @pl.when(i == 0)
        def _():
            o_ref[...] = v

        @pl.when(i > 0)
        def _():
            o_ref[...] += v

    blk = _nbytes((tm, ka), a.dtype) + _nbytes((tm, kd), d.dtype) + _nbytes((ka, kd), F32)
    return pl.pallas_call(
        body, grid=(nb, M // tm),
        in_specs=[pl.BlockSpec((tm, ka), lambda b, i: (i, b)), pl.BlockSpec((tm, kd), lambda b, i: (i, b))],
        out_specs=pl.BlockSpec((None, ka, kd), lambda b, i: (b, 0, 0)),
        out_shape=jax.ShapeDtypeStruct((nb, ka, kd), F32), name=name,
        compiler_params=_params(("parallel", "arbitrary"), blk),
    )(a, d)


def ew(fn, ins, outs, *, rows, tr, name):
    n = rows // tr
    assert n * tr == rows
    in_specs, args, blk = [], [], 0
    for a, kind in ins:
        if kind == "tile":
            assert a.shape[0] == rows, (name, a.shape, rows)
            in_specs.append(pl.BlockSpec((tr, a.shape[1]), lambda i: (i, 0)))
            blk += _nbytes((tr, a.shape[1]), a.dtype)
        else:
            in_specs.append(pl.BlockSpec(a.shape, lambda i, nd=a.ndim: (0,) * nd))
            blk += _nbytes(a.shape, a.dtype)
        args.append(a)
    out_shape, out_specs = [], []
    for c, dt, kind in outs:
        if kind == "tile":
            out_shape.append(jax.ShapeDtypeStruct((rows, c), dt))
            out_specs.append(pl.BlockSpec((tr, c), lambda i: (i, 0)))
            blk += _nbytes((tr, c), dt)
        else:
            out_shape.append(jax.ShapeDtypeStruct((1, c), dt))
            out_specs.append(pl.BlockSpec((1, c), lambda i: (0, 0)))
    nin = len(ins)

    def body(*refs):
        i = pl.program_id(0)
        vals = fn(i, *[r[...] for r in refs[:nin]])
        for (c, dt, kind), o, v in zip(outs, refs[nin:], vals):
            if kind == "tile":
                o[...] = v.astype(dt)
            else:
                @pl.when(i == 0)
                def _():
                    o[...] = v.astype(dt)

                @pl.when(i > 0)
                def _():
                    o[...] += v.astype(dt)

    has_acc = any(k == "acc" for _, _, k in outs)
    return pl.pallas_call(
        body, grid=(n,), in_specs=in_specs, out_specs=out_specs, out_shape=out_shape, name=name,
        compiler_params=_params(("arbitrary" if has_acc else "parallel",), 3 * blk),
    )(*args)


def _colsum(x):
    return jnp.sum(x, axis=0, keepdims=True)


def rms_fwd(x, g, *, out_dtype, name, tr=256):
    def fn(i, xv, gv):
        r = lax.rsqrt(jnp.mean(xv * xv, axis=-1, keepdims=True) + EPS)
        return [xv * r * gv]

    return ew(fn, [(x, "tile"), (g, "full")], [(x.shape[1], out_dtype, "tile")], rows=x.shape[0], tr=tr, name=name)[0]


def rms_bwd(x, g, dy, dres, *, name, tr=256):
    def fn(i, xv, gv, dyv, drv):
        dyv = dyv.astype(F32)
        r = lax.rsqrt(jnp.mean(xv * xv, axis=-1, keepdims=True) + EPS)
        xh = xv * r
        gy = dyv * gv
        dx = r * (gy - xh * jnp.mean(gy * xh, axis=-1, keepdims=True))
        return [dx + drv, _colsum(dyv * xh)]

    D = x.shape[1]
    return ew(fn, [(x, "tile"), (g, "full"), (dy, "tile"), (dres, "tile")], [(D, F32, "tile"), (D, F32, "acc")],
              rows=x.shape[0], tr=tr, name=name)


def rms_bwd_g(x, dy, *, name, tr=256):
    def fn(i, xv, dyv):
        r = lax.rsqrt(jnp.mean(xv * xv, axis=-1, keepdims=True) + EPS)
        return [_colsum(dyv.astype(F32) * xv * r)]

    return ew(fn, [(x, "tile"), (dy, "tile")], [(x.shape[1], F32, "acc")], rows=x.shape[0],
              tr=_tile(x.shape[0], tr, SUBLANE), name=name)[0]


def loss_head(h, g, target, *, name, tr=256):
    D = h.shape[1]

    def fn(i, xv, gv, tv):
        r = lax.rsqrt(jnp.mean(xv * xv, axis=-1, keepdims=True) + EPS)
        xh = xv * r
        err = xh * gv - tv
        dy = err * (1.0 / D)
        gy = dy * gv
        dx = r * (gy - xh * jnp.mean(gy * xh, axis=-1, keepdims=True))
        part = _colsum(err * err) * (0.5 / D)
        return [dx, _colsum(dy * xh), jnp.sum(part, axis=1, keepdims=True)]

    return ew(fn, [(h, "tile"), (g, "full"), (target, "tile")], [(D, F32, "tile"), (D, F32, "acc"), (1, F32, "acc")],
              rows=h.shape[0], tr=tr, name=name)


def _as2d(a):
    if a.ndim >= 2 and a.shape[-1] >= LANE:
        return a.reshape(-1, a.shape[-1])
    if a.size % (8 * LANE) == 0:
        return a.reshape(-1, 8 * LANE)
    return a.reshape(1, -1)


def adamw(w, g, m, v, *, name):
    shape = w.shape
    w2, g2, m2, v2 = (_as2d(t) for t in (w, g.astype(F32).reshape(shape), m, v))
    R, C = w2.shape
    tr = R
    if R * C * 4 > (1 << 20):
        tr = _tile(R, max(SUBLANE, (1 << 20) // (C * 4) // SUBLANE * SUBLANE), SUBLANE)
    c1 = 1.0 / (1.0 - ADAM_B1 ** ADAM_STEP)
    c2 = 1.0 / (1.0 - ADAM_B2 ** ADAM_STEP)

    def fn(i, wv, gv, mv, vv):
        mn = ADAM_B1 * mv + (1.0 - ADAM_B1) * gv
        vn = ADAM_B2 * vv + (1.0 - ADAM_B2) * (gv * gv)
        delta = -ADAM_LR * ((mn * c1) / (jnp.sqrt(vn * c2) + ADAM_EPS) + ADAM_WD * wv)
        return [delta, mn, vn]

    d, mn, vn = ew(fn, [(w2, "tile"), (g2, "tile"), (m2, "tile"), (v2, "tile")], [(C, F32, "tile")] * 3,
                   rows=R, tr=tr, name=name)
    return d.reshape(shape), mn.reshape(shape), vn.reshape(shape)


def _split_bf16(x):
    hi = x.astype(BF16)
    return hi, (x - hi.astype(F32)).astype(BF16)


def _dot2(band, x):
    hi, lo = _split_bf16(x)
    return jnp.dot(band, hi, preferred_element_type=F32) + jnp.dot(band, lo, preferred_element_type=F32)


def _pool_fwd_window(xm, xh, r0, win):
    T = xm.shape[0]
    t = r0 + lax.broadcasted_iota(jnp.int32, (T, 1), 0)
    s_main = r0 + lax.broadcasted_iota(jnp.int32, (1, T), 1)
    s_halo = r0 - POOL_HALO + lax.broadcasted_iota(jnp.int32, (1, POOL_HALO), 1)
    band_m = ((s_main <= t) & (s_main > t - win)).astype(BF16)
    band_h = ((s_halo > t - win) & (s_halo >= 0)).astype(BF16)
    ws = _dot2(band_m, xm) + _dot2(band_h, xh)
    cnt = jnp.minimum(t + 1, win).astype(F32)
    return ws / cnt - xm


def _pool_bwd_window(dm, dh, r0, win, S):
    T = dm.shape[0]
    s = r0 + lax.broadcasted_iota(jnp.int32, (T, 1), 0)
    t_main = r0 + lax.broadcasted_iota(jnp.int32, (1, T), 1)
    t_halo = r0 + T + lax.broadcasted_iota(jnp.int32, (1, POOL_HALO), 1)
    band_m = ((t_main >= s) & (t_main < s + win)).astype(BF16)
    band_h = ((t_halo < s + win) & (t_halo < S)).astype(BF16)
    tm_col = r0 + lax.broadcasted_iota(jnp.int32, (T, 1), 0)
    th_col = r0 + T + lax.broadcasted_iota(jnp.int32, (POOL_HALO, 1), 0)
    dmc = dm / jnp.minimum(tm_col + 1, win).astype(F32)
    dhc = dh / jnp.minimum(th_col + 1, win).astype(F32)
    return _dot2(band_m, dmc) + _dot2(band_h, dhc) - dm


def _pool_specs(T, Cg, order):
    per = T // POOL_HALO
    if order == "ig":
        return (pl.BlockSpec((T, Cg), lambda i, g: (i, g)),
                pl.BlockSpec((POOL_HALO, Cg), lambda i, g: (jnp.maximum(i * per - 1, 0), g)))
    return (pl.BlockSpec((T, Cg), lambda g, i: (i, g)),
            pl.BlockSpec((POOL_HALO, Cg), lambda g, i: (jnp.maximum(i * per - 1, 0), g)))


def pool_fwd(hn, h, w, scale, *, name):
    S, D = hn.shape
    G, Cg, _ = w.shape
    T = _tile(S, POOL_TILE)

    def body(xm_ref, xh_ref, h_ref, w_ref, sc_ref, o_ref):
        i, g = pl.program_id(0), pl.program_id(1)
        win = jnp.left_shift(2, g)
        p = _pool_fwd_window(xm_ref[...], xh_ref[...], i * T, win)
        y = jnp.dot(p.astype(BF16), w_ref[...], preferred_element_type=F32)
        o_ref[...] = h_ref[...] + y * sc_ref[...]

    main, halo = _pool_specs(T, Cg, "ig")
    return pl.pallas_call(
        body, grid=(S // T, G),
        in_specs=[main, halo, main, pl.BlockSpec((None, Cg, Cg), lambda i, g: (g, 0, 0)),
                  pl.BlockSpec((1, Cg), lambda i, g: (0, g))],
        out_specs=main, out_shape=jax.ShapeDtypeStruct((S, D), F32), name=name,
        compiler_params=_params(("parallel", "parallel"), 4 * T * Cg * 4),
    )(hn, hn, h, w, scale)


def pool_bwd_w(hn, dt, w, scale, *, name):
    S, D = hn.shape
    G, Cg, _ = w.shape
    T = _tile(S, POOL_TILE)

    def body(xm_ref, xh_ref, dt_ref, w_ref, sc_ref, dp_ref, dw_ref, ds_ref):
        g, i = pl.program_id(0), pl.program_id(1)
        win = jnp.left_shift(2, g)
        p = _pool_fwd_window(xm_ref[...], xh_ref[...], i * T, win).astype(BF16)
        dtv = dt_ref[...]
        ypre = jnp.dot(p, w_ref[...], preferred_element_type=F32)
        dy = (dtv * sc_ref[...]).astype(BF16)
        dp_ref[...] = lax.dot_general(dy, w_ref[...], (((1,), (1,)), ((), ())), preferred_element_type=F32)
        dwv = lax.dot_general(p, dy, (((0,), (0,)), ((), ())), preferred_element_type=F32)
        dsv = _colsum(dtv * ypre)

        @pl.when(i == 0)
        def _():
            dw_ref[...] = dwv
            ds_ref[...] = dsv

        @pl.when(i > 0)
        def _():
            dw_ref[...] += dwv
            ds_ref[...] += dsv

    main, halo = _pool_specs(T, Cg, "gi")
    return pl.pallas_call(
        body, grid=(G, S // T),
        in_specs=[main, halo, main, pl.BlockSpec((None, Cg, Cg), lambda g, i: (g, 0, 0)),
                  pl.BlockSpec((1, Cg), lambda g, i: (0, g))],
        out_specs=[main, pl.BlockSpec((None, Cg, Cg), lambda g, i: (g, 0, 0)), pl.BlockSpec((1, Cg), lambda g, i: (0, g))],
        out_shape=[jax.ShapeDtypeStruct((S, D), F32), jax.ShapeDtypeStruct((G, Cg, Cg), F32),
                   jax.ShapeDtypeStruct((1, D), F32)], name=name,
        compiler_params=_params(("parallel", "arbitrary"), 4 * T * Cg * 4),
    )(hn, hn, dt, w, scale)


def pool_bwd_x(dp, G, *, name):
    S, D = dp.shape
    Cg = D // G
    T = _tile(S, POOL_TILE)
    per = T // POOL_HALO
    last = S // POOL_HALO - 1

    def body(dm_ref, dh_ref, o_ref):
        i, g = pl.program_id(0), pl.program_id(1)
        o_ref[...] = _pool_bwd_window(dm_ref[...], dh_ref[...], i * T, jnp.left_shift(2, g), S)

    main = pl.BlockSpec((T, Cg), lambda i, g: (i, g))
    return pl.pallas_call(
        body, grid=(S // T, G),
        in_specs=[main, pl.BlockSpec((POOL_HALO, Cg), lambda i, g: (jnp.minimum((i + 1) * per, last), g))],
        out_specs=main, out_shape=jax.ShapeDtypeStruct((S, D), F32), name=name,
        compiler_params=_params(("parallel", "parallel"), 3 * T * Cg * 4),
    )(dp, dp)


def _sb_logs(z, mask):
    e = jnp.exp(-jnp.abs(z))
    sp = jnp.log(1.0 + e)
    ls = jnp.minimum(z, 0.0) - sp
    lsn = jnp.where(mask, jnp.minimum(-z, 0.0) - sp, 0.0)
    return ls, lsn, e


def _dot2r(x, band):
    hi, lo = _split_bf16(x)
    return jnp.dot(hi, band, preferred_element_type=F32) + jnp.dot(lo, band, preferred_element_type=F32)


def _dot3r(x, band):
    hi = x.astype(BF16)
    r1 = x - hi.astype(F32)
    mid = r1.astype(BF16)
    lo = (r1 - mid.astype(F32)).astype(BF16)
    return (jnp.dot(hi, band, preferred_element_type=F32) + jnp.dot(mid, band, preferred_element_type=F32)
            + jnp.dot(lo, band, preferred_element_type=F32))


def _head_masks(n_lanes):
    lane = lax.broadcasted_iota(jnp.int32, (1, n_lanes), 1)
    return [((lane >= h * SB_HEAD_DIM) & (lane < (h + 1) * SB_HEAD_DIM)) for h in range(n_lanes // SB_HEAD_DIM)]


_NT = (((1,), (1,)), ((), ()))
_TN = (((0,), (0,)), ((), ()))


SB_UNROLL = 4


def _sb_unroll(S):
    return SB_UNROLL if S % (SB_UNROLL * SB_BLOCK) == 0 else 1


SB_DEAD_LOG = -105.0


def _sb_loop(n_steps, step, init, *, rs_at):
    def cond(state):
        n, alive, _ = state
        return jnp.logical_and(n < n_steps, alive > 0)

    def body(state):
        n, _, carry = state
        carry = step(n, carry)
        top = functools.reduce(jnp.maximum, [jnp.max(r) for r in carry[rs_at]])
        return n + 1, (top > SB_DEAD_LOG).astype(jnp.int32), carry

    return lax.while_loop(cond, body, (jnp.int32(0), jnp.int32(1), init))[2]


def sb_fwd(qkv, *, name):
    S, D3 = qkv.shape
    D = D3 // 3
    B = SB_BLOCK
    npair = D // LANE
    scale = SB_HEAD_DIM ** -0.5

    U = _sb_unroll(S)

    def body(q_ref, k_ref, v_ref, o_ref):
        i = pl.program_id(1)
        masks = _head_masks(LANE)
        q = q_ref[...] * scale
        qh = [jnp.where(m, q, jnp.zeros_like(q)) for m in masks]
        row = lax.broadcasted_iota(jnp.int32, (B, B), 0)
        col = lax.broadcasted_iota(jnp.int32, (B, B), 1)
        upper = (row > col).astype(BF16)
        diag = col < row
        nsuper = i // U + 1

        def step(n, carry):
            os_, rs = list(carry[0]), list(carry[1])
            last = i + 1 - n * U
            first = jnp.maximum(last - U, 0)
            base = pl.multiple_of(first * B, B)
            kbig = k_ref[pl.ds(base, U * B), :]
            vbig = v_ref[pl.ds(base, U * B), :]
            tiles = [(c, hd) for c in reversed(range(U)) for hd in range(len(masks))]
            kb = {c: kbig[c * B:(c + 1) * B] for c in range(U)}
            vb = {c: vbig[c * B:(c + 1) * B] for c in range(U)}
            mask = {c: jnp.logical_and(first + c < last, jnp.logical_or(first + c < i, diag)) for c in range(U)}
            z = {t: lax.dot_general(qh[t[1]], kb[t[0]], _NT, preferred_element_type=F32) for t in tiles}
            ls, lsn = {}, {}
            for t in tiles:
                ls[t], lsn[t], _ = _sb_logs(z[t], mask[t[0]])
            local = {t: _dot2r(lsn[t], upper) for t in tiles}
            for c, hd in tiles:
                a = jnp.where(mask[c], jnp.exp(ls[(c, hd)] + local[(c, hd)] + rs[hd]), 0.0)
                os_[hd] = os_[hd] + jnp.dot(a.astype(BF16), vb[c], preferred_element_type=F32)
                rs[hd] = rs[hd] + jnp.sum(lsn[(c, hd)], axis=1, keepdims=True)
            return tuple(os_), tuple(rs)

        zero = jnp.zeros((B, 1), F32)
        zacc = jnp.zeros((B, LANE), F32)
        os_, _ = _sb_loop(nsuper, step, (tuple(zacc for _ in masks), tuple(zero for _ in masks)), rs_at=1)
        o = jnp.zeros((B, LANE), F32)
        for m, oh in zip(masks, os_):
            o = jnp.where(m, oh, o)
        o_ref[...] = o

    return pl.pallas_call(
        body, grid=(npair, S // B),
        in_specs=[pl.BlockSpec((B, LANE), lambda p, i: (i, p)),
                  pl.BlockSpec((S, LANE), lambda p, i: (0, npair + p)),
                  pl.BlockSpec((S, LANE), lambda p, i: (0, 2 * npair + p))],
        out_specs=pl.BlockSpec((B, LANE), lambda p, i: (i, p)),
        out_shape=jax.ShapeDtypeStruct((S, D), F32), name=name,
        compiler_params=_params(("parallel", "arbitrary"), 2 * S * LANE * 2),
    )(qkv, qkv, qkv)


def sb_bwd(qkv, o, do, *, name):
    S, D3 = qkv.shape
    D = D3 // 3
    B = SB_BLOCK
    npair = D // LANE
    scale = SB_HEAD_DIM ** -0.5

    U = _sb_unroll(S)

    def body(q_ref, k_ref, v_ref, o_ref, do_ref, dq_ref, dk_ref, dv_ref):
        i = pl.program_id(1)

        @pl.when(i == 0)
        def _():
            dk_ref[...] = jnp.zeros_like(dk_ref)
            dv_ref[...] = jnp.zeros_like(dv_ref)

        masks = _head_masks(LANE)
        q = q_ref[...] * scale
        dov = do_ref[...]
        ov = o_ref[...]
        qh = [jnp.where(m, q, jnp.zeros_like(q)) for m in masks]
        doh = [jnp.where(m, dov, 0.0).astype(BF16) for m in masks]
        gsum = [jnp.sum(dh_.astype(F32) * ov, axis=1, keepdims=True) for dh_ in doh]
        row = lax.broadcasted_iota(jnp.int32, (B, B), 0)
        col = lax.broadcasted_iota(jnp.int32, (B, B), 1)
        upper = (row > col).astype(BF16)
        upper_incl = (row >= col).astype(BF16)
        diag = col < row
        nsuper = i // U + 1

        def step(n, carry):
            dqs, rs, gs = list(carry[0]), list(carry[1]), list(carry[2])
            last = i + 1 - n * U
            first = jnp.maximum(last - U, 0)
            base = pl.multiple_of(first * B, B)
            kbig = k_ref[pl.ds(base, U * B), :]
            vbig = v_ref[pl.ds(base, U * B), :]
            nh = len(masks)
            tiles = [(c, hd) for c in reversed(range(U)) for hd in range(nh)]
            kb = {c: kbig[c * B:(c + 1) * B] for c in range(U)}
            vb = {c: vbig[c * B:(c + 1) * B] for c in range(U)}
            mask = {c: jnp.logical_and(first + c < last, jnp.logical_or(first + c < i, diag)) for c in range(U)}
            z = {t: lax.dot_general(qh[t[1]], kb[t[0]], _NT, preferred_element_type=F32) for t in tiles}
            da = {t: lax.dot_general(doh[t[1]], vb[t[0]], _NT, preferred_element_type=F32) for t in tiles}
            ls, lsn, sig = {}, {}, {}
            for t in tiles:
                ls[t], lsn[t], e = _sb_logs(z[t], mask[t[0]])
                sig[t] = jnp.exp(ls[t])
            local = {t: _dot2r(lsn[t], upper) for t in tiles}
            ab, g = {}, {}
            for c, hd in tiles:
                a = jnp.where(mask[c], jnp.exp(ls[(c, hd)] + local[(c, hd)] + rs[hd]), 0.0)
                ab[(c, hd)] = a.astype(BF16)
                g[(c, hd)] = ab[(c, hd)].astype(F32) * da[(c, hd)]
                rs[hd] = rs[hd] + jnp.sum(lsn[(c, hd)], axis=1, keepdims=True)
            glocal = {t: _dot3r(g[t], upper_incl) for t in tiles}
            dzb = {}
            for c, hd in tiles:
                t = (c, hd)
                sg = glocal[t] + gs[hd]
                dzb[t] = jnp.where(mask[c], g[t] * (1.0 - sig[t]) - (gsum[hd] - sg) * sig[t], 0.0).astype(BF16)
                gs[hd] = gs[hd] + jnp.sum(g[t], axis=1, keepdims=True)
            for c, hd in tiles:
                dqs[hd] = dqs[hd] + jnp.dot(dzb[(c, hd)], kb[c], preferred_element_type=F32)
            for c in reversed(range(U)):
                dkb = sum(lax.dot_general(dzb[(c, hd)], qh[hd], _TN, preferred_element_type=F32) for hd in range(nh))
                dvb = sum(lax.dot_general(ab[(c, hd)], doh[hd], _TN, preferred_element_type=F32) for hd in range(nh))
                off = pl.multiple_of(base + c * B, B)
                dk_ref[pl.ds(off, B), :] += dkb
                dv_ref[pl.ds(off, B), :] += dvb
            return tuple(dqs), tuple(rs), tuple(gs)

        zero = jnp.zeros((B, 1), F32)
        zs = tuple(zero for _ in masks)
        zacc = jnp.zeros((B, LANE), F32)
        dqs, _, _ = _sb_loop(nsuper, step, (tuple(zacc for _ in masks), zs, zs), rs_at=1)
        dq = jnp.zeros((B, LANE), F32)
        for m, dqh in zip(masks, dqs):
            dq = jnp.where(m, dqh, dq)
        dq_ref[...] = dq * scale

    tile = pl.BlockSpec((B, LANE), lambda p, i: (i, p))
    strip = pl.BlockSpec((S, LANE), lambda p, i: (0, p))
    return pl.pallas_call(
        body, grid=(npair, S // B),
        in_specs=[tile, pl.BlockSpec((S, LANE), lambda p, i: (0, npair + p)),
                  pl.BlockSpec((S, LANE), lambda p, i: (0, 2 * npair + p)), tile, tile],
        out_specs=[tile, strip, strip],
        out_shape=[jax.ShapeDtypeStruct((S, D), F32)] * 3, name=name,
        compiler_params=_params(("parallel", "arbitrary"), 2 * S * LANE * 2 + 2 * S * LANE * 4),
    )(qkv, qkv, qkv, o, do)


S5_HALF = S5_GROUPS_PER_BLOCK * S5_STATE
S5_BLOCK = 2 * S5_HALF


def s5_scan(bu, pw, *, reverse, name, tr=512):
    S, W = bu.shape
    nb = W // S5_BLOCK
    tr = _tile(S, tr, SUBLANE)
    nsub = tr // SUBLANE
    nt = S // tr
    H = S5_HALF

    def body(bu_ref, pw_ref, x_ref, st_re, st_im):
        i = pl.program_id(1)

        @pl.when(i == 0)
        def _():
            st_re[...] = jnp.zeros_like(st_re)
            st_im[...] = jnp.zeros_like(st_im)

        row = lax.broadcasted_iota(jnp.int32, (SUBLANE, H), 0)
        steps = []
        for k, sh in enumerate((1, 2, 4)):
            valid = (row < SUBLANE - sh) if reverse else (row >= sh)
            steps.append((sh, valid, pw_ref[SUBLANE + k:SUBLANE + k + 1, 0:H], pw_ref[SUBLANE + k:SUBLANE + k + 1, H:2 * H]))
        ap_re = pw_ref[0:SUBLANE, 0:H]
        ap_im = pw_ref[0:SUBLANE, H:2 * H]
        edge = (row == 0) if reverse else (row == SUBLANE - 1)

        def sub(n, carry):
            s_re, s_im = carry
            j = (nsub - 1 - n) if reverse else n
            off = pl.multiple_of(j * SUBLANE, SUBLANE)
            r = bu_ref[pl.ds(off, SUBLANE), 0:H]
            m = bu_ref[pl.ds(off, SUBLANE), H:2 * H]
            for sh, valid, a_re, a_im in steps:
                amt = (SUBLANE - sh) if reverse else sh
                rs = jnp.where(valid, pltpu.roll(r, amt, 0), 0.0)
                ms = jnp.where(valid, pltpu.roll(m, amt, 0), 0.0)
                r, m = r + a_re * rs - a_im * ms, m + a_re * ms + a_im * rs
            r, m = r + ap_re * s_re - ap_im * s_im, m + ap_re * s_im + ap_im * s_re
            x_ref[pl.ds(off, SUBLANE), 0:H] = r
            x_ref[pl.ds(off, SUBLANE), H:2 * H] = m
            return (jnp.sum(jnp.where(edge, r, 0.0), axis=0, keepdims=True),
                    jnp.sum(jnp.where(edge, m, 0.0), axis=0, keepdims=True))

        s_re, s_im = lax.fori_loop(0, nsub, sub, (st_re[...], st_im[...]), unroll=4)
        st_re[...] = s_re
        st_im[...] = s_im

    if reverse:
        tile = pl.BlockSpec((tr, S5_BLOCK), lambda b, i: (nt - 1 - i, b))
    else:
        tile = pl.BlockSpec((tr, S5_BLOCK), lambda b, i: (i, b))
    return pl.pallas_call(
        body, grid=(nb, nt),
        in_specs=[tile, pl.BlockSpec((2 * SUBLANE, S5_BLOCK), lambda b, i: (0, b))],
        out_specs=tile, out_shape=jax.ShapeDtypeStruct((S, W), F32),
        scratch_shapes=[pltpu.VMEM((1, H), F32), pltpu.VMEM((1, H), F32)], name=name,
        compiler_params=_params(("parallel", "arbitrary"), 2 * tr * S5_BLOCK * 4),
    )(bu, pw)


def s5_da(lam, x, *, name, tr=512):
    S, W = lam.shape
    nb = W // S5_BLOCK
    tr = _tile(S, tr, SUBLANE)
    nsub = tr // SUBLANE
    nt = S // tr
    H = S5_HALF

    def body(l_ref, x_ref, xh_ref, o_ref, acc_re, acc_im):
        i = pl.program_id(1)

        @pl.when(i == 0)
        def _():
            acc_re[...] = jnp.zeros_like(acc_re)
            acc_im[...] = jnp.zeros_like(acc_im)

        row = lax.broadcasted_iota(jnp.int32, (SUBLANE, H), 0)
        first = row == 0

        def sub(n, carry):
            a_re, a_im = carry
            off = pl.multiple_of(n * SUBLANE, SUBLANE)
            poff = pl.multiple_of(jnp.maximum(n - 1, 0) * SUBLANE, SUBLANE)
            inside = n > 0
            start = jnp.logical_and(i == 0, n == 0)
            out = []
            for lo in (0, H):
                cur = x_ref[pl.ds(off, SUBLANE), lo:lo + H]
                prv = jnp.where(inside, x_ref[pl.ds(poff, SUBLANE), lo:lo + H], xh_ref[:, lo:lo + H])
                xs = jnp.where(first, pltpu.roll(prv, 1, 0), pltpu.roll(cur, 1, 0))
                out.append(jnp.where(jnp.logical_and(start, first), 0.0, xs))
            xs_re, xs_im = out
            l_re = l_ref[pl.ds(off, SUBLANE), 0:H]
            l_im = l_ref[pl.ds(off, SUBLANE), H:2 * H]
            return a_re + l_re * xs_re + l_im * xs_im, a_im + l_im * xs_re - l_re * xs_im

        a_re, a_im = lax.fori_loop(0, nsub, sub, (acc_re[...], acc_im[...]), unroll=4)
        acc_re[...] = a_re
        acc_im[...] = a_im

        @pl.when(i == nt - 1)
        def _():
            o_ref[:, 0:H] = jnp.sum(a_re, axis=0, keepdims=True)
            o_ref[:, H:2 * H] = jnp.sum(a_im, axis=0, keepdims=True)

    per = tr // SUBLANE
    tile = pl.BlockSpec((tr, S5_BLOCK), lambda b, i: (i, b))
    return pl.pallas_call(
        body, grid=(nb, nt),
        in_specs=[tile, tile, pl.BlockSpec((SUBLANE, S5_BLOCK), lambda b, i: (jnp.maximum(i * per - 1, 0), b))],
        out_specs=pl.BlockSpec((1, S5_BLOCK), lambda b, i: (0, b)),
        out_shape=jax.ShapeDtypeStruct((1, W), F32),
        scratch_shapes=[pltpu.VMEM((SUBLANE, H), F32), pltpu.VMEM((SUBLANE, H), F32)], name=name,
        compiler_params=_params(("parallel", "arbitrary"), 2 * tr * S5_BLOCK * 4),
    )(lam, x, x)


def _gelu(y):
    c = math.sqrt(2.0 / math.pi)
    return 0.5 * y * (1.0 + jnp.tanh(c * (y + 0.044715 * y * y * y)))


def _gelu_grad(y):
    c = math.sqrt(2.0 / math.pi)
    th = jnp.tanh(c * (y + 0.044715 * y * y * y))
    return 0.5 * (1.0 + th) + 0.5 * y * (1.0 - th * th) * c * (1.0 + 3.0 * 0.044715 * y * y)


def _sigmoid(x):
    return 0.5 + 0.5 * jnp.tanh(0.5 * x)


def _xa_probs(qh, kh, scale):
    s = lax.dot_general(qh, kh, _NT, preferred_element_type=F32) * scale
    p = jnp.exp(s - jnp.max(s, axis=-1, keepdims=True))
    return p / jnp.sum(p, axis=-1, keepdims=True)


def xa_fwd(q, kv, *, name, tm=512):
    S, D = q.shape
    M = kv.shape[0]
    dh = D // XA_HEADS
    scale = dh ** -0.5
    tm = _tile(S, tm)

    def body(q_ref, kv_ref, o_ref):
        for h in range(XA_HEADS):
            p = _xa_probs(q_ref[:, h * dh:(h + 1) * dh], kv_ref[:, h * dh:(h + 1) * dh], scale)
            o_ref[:, h * dh:(h + 1) * dh] = jnp.dot(p.astype(BF16), kv_ref[:, D + h * dh:D + (h + 1) * dh],
                                                   preferred_element_type=F32).astype(BF16)

    return pl.pallas_call(
        body, grid=(S // tm,),
        in_specs=[pl.BlockSpec((tm, D), lambda i: (i, 0)), pl.BlockSpec((M, 2 * D), lambda i: (0, 0))],
        out_specs=pl.BlockSpec((tm, D), lambda i: (i, 0)),
        out_shape=jax.ShapeDtypeStruct((S, D), BF16), name=name,
        compiler_params=_params(("parallel",), 2 * tm * D * 2 + M * 2 * D * 2),
    )(q, kv)


def xa_bwd(q, kv, do, *, name, tm=512):
    S, D = q.shape
    M = kv.shape[0]
    dh = D // XA_HEADS
    scale = dh ** -0.5
    tm = _tile(S, tm)

    def body(q_ref, kv_ref, do_ref, dq_ref, dkv_ref):
        i = pl.program_id(0)

        @pl.when(i == 0)
        def _():
            dkv_ref[...] = jnp.zeros_like(dkv_ref)

        for h in range(XA_HEADS):
            sl = slice(h * dh, (h + 1) * dh)
            vsl = slice(D + h * dh, D + (h + 1) * dh)
            qh, kh, vh = q_ref[:, sl], kv_ref[:, sl], kv_ref[:, vsl]
            doh = do_ref[:, sl].astype(BF16)
            p = _xa_probs(qh, kh, scale)
            dp = lax.dot_general(doh, vh, _NT, preferred_element_type=F32)
            ds = (p * (dp - jnp.sum(dp * p, axis=-1, keepdims=True)) * scale).astype(BF16)
            dq_ref[:, sl] = jnp.dot(ds, kh, preferred_element_type=F32).astype(BF16)
            dkv_ref[:, sl] += lax.dot_general(ds, qh, _TN, preferred_element_type=F32)
            dkv_ref[:, vsl] += lax.dot_general(p.astype(BF16), doh, _TN, preferred_element_type=F32)

    return pl.pallas_call(
        body, grid=(S // tm,),
        in_specs=[pl.BlockSpec((tm, D), lambda i: (i, 0)), pl.BlockSpec((M, 2 * D), lambda i: (0, 0)),
                  pl.BlockSpec((tm, D), lambda i: (i, 0))],
        out_specs=[pl.BlockSpec((tm, D), lambda i: (i, 0)), pl.BlockSpec((M, 2 * D), lambda i: (0, 0))],
        out_shape=[jax.ShapeDtypeStruct((S, D), BF16), jax.ShapeDtypeStruct((M, 2 * D), F32)], name=name,
        compiler_params=_params(("arbitrary",), 3 * tm * D * 2 + M * 2 * D * 6),
    )(q, kv, do)


FFN_STRIP = 256
FFN_ROWS = 512


def _shift_down(x, k):
    return pltpu.roll(x, k, 0)


def _shift_up(x, k):
    return pltpu.roll(x, x.shape[0] - k, 0)


FFN_HALO = 2 * SUBLANE


def _rows_with_prev(u_ref, r0, R):
    cur = u_ref[pl.ds(r0, R), :].astype(F32)
    p0 = pl.multiple_of(jnp.maximum(r0 - FFN_HALO, 0), FFN_HALO)
    prev = jnp.where(r0 > 0, u_ref[pl.ds(p0, FFN_HALO), :].astype(F32), 0.0)
    return jnp.concatenate([prev, cur], axis=0), cur


def _conv_rows(u_ref, r0, R, w_ref, b_ref):
    ext, _ = _rows_with_prev(u_ref, r0, R)
    out = w_ref[2:3, :] * ext + w_ref[1:2, :] * _shift_down(ext, 1) + w_ref[0:1, :] * _shift_down(ext, 2) + b_ref[...]
    return out[FFN_HALO:, :]


def ffn_act_fwd(u, conv_w, conv_b, *, name):
    S, F2 = u.shape
    F = F2 // 2
    tc = _tile(F, FFN_STRIP)
    nc = F // tc
    R = _tile(S, FFN_ROWS, 16)

    def body(uv_ref, ug_ref, wv_ref, wg_ref, bv_ref, bg_ref, o_ref):
        def rows(n, _):
            r0 = pl.multiple_of(n * R, R)
            val = _conv_rows(uv_ref, r0, R, wv_ref, bv_ref)
            gate = _conv_rows(ug_ref, r0, R, wg_ref, bg_ref)
            o_ref[pl.ds(r0, R), :] = (gate * _sigmoid(gate) * val).astype(BF16)
            return 0

        lax.fori_loop(0, S // R, rows, 0)

    return pl.pallas_call(
        body, grid=(nc,),
        in_specs=[pl.BlockSpec((S, tc), lambda c: (0, c)), pl.BlockSpec((S, tc), lambda c: (0, nc + c)),
                  pl.BlockSpec((CONV_WIDTH, tc), lambda c: (0, c)), pl.BlockSpec((CONV_WIDTH, tc), lambda c: (0, nc + c)),
                  pl.BlockSpec((1, tc), lambda c: (0, c)), pl.BlockSpec((1, tc), lambda c: (0, nc + c))],
        out_specs=pl.BlockSpec((S, tc), lambda c: (0, c)),
        out_shape=jax.ShapeDtypeStruct((S, F), BF16), name=name,
        compiler_params=_params(("parallel",), 3 * S * tc * 2),
    )(u, u, conv_w, conv_w, conv_b, conv_b)


def ffn_act_bwd(u, dact, conv_w, conv_b, *, name):
    S, F2 = u.shape
    F = F2 // 2
    tc = _tile(F, FFN_STRIP)
    nc = F // tc
    R = _tile(S, FFN_ROWS, 16)
    nr = S // R
    HALO = FFN_HALO

    def body(uv_ref, ug_ref, da_ref, wv_ref, wg_ref, bv_ref, bg_ref,
             duv_ref, dug_ref, dwv_ref, dwg_ref, dbv_ref, dbg_ref, dcv, dcg):
        def p1(n, _):
            r0 = pl.multiple_of(n * R, R)
            val = _conv_rows(uv_ref, r0, R, wv_ref, bv_ref)
            gate = _conv_rows(ug_ref, r0, R, wg_ref, bg_ref)
            d = da_ref[pl.ds(r0, R), :].astype(F32)
            sg = _sigmoid(gate)
            dcv[pl.ds(r0, R), :] = d * gate * sg
            dcg[pl.ds(r0, R), :] = d * val * (sg + gate * sg * (1.0 - sg))
            return 0

        lax.fori_loop(0, nr, p1, 0)

        def p2(n, carry):
            r0 = pl.multiple_of(n * R, R)
            nxt = pl.multiple_of(jnp.minimum(r0 + R, S - HALO), HALO)
            new = []
            for u_ref, dc, w_ref, du_ref, acc in ((uv_ref, dcv, wv_ref, duv_ref, carry[0]),
                                                  (ug_ref, dcg, wg_ref, dug_ref, carry[1])):
                d = dc[pl.ds(r0, R), :]
                after = jnp.where(r0 + R < S, dc[pl.ds(nxt, HALO), :], 0.0)
                ext = jnp.concatenate([d, after], axis=0)
                du = w_ref[2:3, :] * ext + w_ref[1:2, :] * _shift_up(ext, 1) + w_ref[0:1, :] * _shift_up(ext, 2)
                du_ref[pl.ds(r0, R), :] = du[:R, :].astype(BF16)
                uext, cur = _rows_with_prev(u_ref, r0, R)
                u1 = _shift_down(uext, 1)[HALO:, :]
                u2 = _shift_down(uext, 2)[HALO:, :]
                dw2, dw1, dw0, db = acc
                new.append((dw2 + _colsum(d * cur), dw1 + _colsum(d * u1), dw0 + _colsum(d * u2), db + _colsum(d)))
            return tuple(new)

        z = jnp.zeros((1, tc), F32)
        accs = lax.fori_loop(0, nr, p2, ((z, z, z, z), (z, z, z, z)))
        for (dw2, dw1, dw0, db), dw_ref, db_ref in ((accs[0], dwv_ref, dbv_ref), (accs[1], dwg_ref, dbg_ref)):
            dw_ref[0:1, :] = dw0
            dw_ref[1:2, :] = dw1
            dw_ref[2:3, :] = dw2
            db_ref[...] = db

    strip_v = pl.BlockSpec((S, tc), lambda c: (0, c))
    strip_g = pl.BlockSpec((S, tc), lambda c: (0, nc + c))
    w_v = pl.BlockSpec((CONV_WIDTH, tc), lambda c: (0, c))
    w_g = pl.BlockSpec((CONV_WIDTH, tc), lambda c: (0, nc + c))
    b_v = pl.BlockSpec((1, tc), lambda c: (0, c))
    b_g = pl.BlockSpec((1, tc), lambda c: (0, nc + c))
    outs = pl.pallas_call(
        body, grid=(nc,),
        in_specs=[strip_v, strip_g, strip_v, w_v, w_g, b_v, b_g],
        out_specs=[strip_v, strip_v, w_v, w_v, b_v, b_v],
        out_shape=[jax.ShapeDtypeStruct((S, F), BF16), jax.ShapeDtypeStruct((S, F), BF16),
                   jax.ShapeDtypeStruct((CONV_WIDTH, F), F32), jax.ShapeDtypeStruct((CONV_WIDTH, F), F32),
                   jax.ShapeDtypeStruct((1, F), F32), jax.ShapeDtypeStruct((1, F), F32)],
        scratch_shapes=[pltpu.VMEM((S, tc), F32), pltpu.VMEM((S, tc), F32)], name=name,
        compiler_params=_params(("parallel",), 5 * S * tc * 2 + S * tc * 4),
    )(u, u, dact, conv_w, conv_w, conv_b, conv_b)
    duv, dug, dwv, dwg, dbv, dbg = outs
    return duv, dug, jnp.concatenate([dwv, dwg], axis=1), jnp.concatenate([dbv, dbg], axis=1)


def _s5_discretize(a_re, a_im, log_dt, b_re, b_im):
    lam = lax.complex(a_re, a_im)
    dt_lam = lam * jnp.exp(log_dt)[:, None]
    a_bar = jnp.exp(dt_lam)
    b_bar = ((a_bar - 1.0) / lam)[..., None] * lax.complex(b_re, b_im)
    return jnp.real(a_bar), jnp.imag(a_bar), jnp.real(b_bar), jnp.imag(b_bar)


def _s5_cols(z_re, z_im, nb):
    lead = z_re.shape[:-2]
    re = z_re.reshape(*lead, nb, S5_HALF)
    im = z_im.reshape(*lead, nb, S5_HALF)
    return jnp.concatenate([re, im], axis=-1).reshape(*lead, nb * S5_BLOCK)


def _s5_powers(a_re, a_im, log_dt, nb, *, reverse):
    dt_lam = lax.complex(a_re, a_im) * jnp.exp(log_dt)[:, None]
    if reverse:
        dt_lam = jnp.conj(dt_lam)
        carry = jnp.arange(SUBLANE, 0, -1, dtype=F32)
    else:
        carry = jnp.arange(1, SUBLANE + 1, dtype=F32)
    ks = jnp.concatenate([carry, jnp.array([1.0, 2.0, 4.0], F32), jnp.zeros((SUBLANE - 3,), F32)])
    pw = jnp.exp(ks[:, None, None] * dt_lam[None])
    return _s5_cols(jnp.real(pw), jnp.imag(pw), nb)


def _s5_in_weights(bb_re, bb_im, nb):
    eye = jnp.eye(S5_GROUPS_PER_BLOCK, dtype=F32)
    G, P, Cg = bb_re.shape

    def one(bb):
        t = jnp.einsum("bgpi,gh->bgihp", bb.reshape(nb, S5_GROUPS_PER_BLOCK, P, Cg), eye)
        return t.reshape(nb, S5_GROUPS_PER_BLOCK * Cg, S5_HALF)

    return jnp.concatenate([one(bb_re), one(bb_im)], axis=2)


def _s5_out_weights(c_re, c_im, nb):
    eye = jnp.eye(S5_GROUPS_PER_BLOCK, dtype=F32)
    G, Cg, P = c_re.shape

    def one(c):
        t = jnp.einsum("bgip,gh->bgphi", c.reshape(nb, S5_GROUPS_PER_BLOCK, Cg, P), eye)
        return t.reshape(nb, S5_HALF, S5_GROUPS_PER_BLOCK * Cg)

    return jnp.concatenate([one(c_re), -one(c_im)], axis=1)


def _s5_in_weight_grads(dwb, Cg):
    nb = dwb.shape[0]
    eye = jnp.eye(S5_GROUPS_PER_BLOCK, dtype=F32)
    t = dwb.reshape(nb, S5_GROUPS_PER_BLOCK, Cg, 2, S5_GROUPS_PER_BLOCK, S5_STATE)
    out = jnp.einsum("bgirhp,gh->rbgpi", t, eye)
    return out[0].reshape(-1, S5_STATE, Cg), out[1].reshape(-1, S5_STATE, Cg)


def _s5_out_weight_grads(dwc, Cg):
    nb = dwc.shape[0]
    eye = jnp.eye(S5_GROUPS_PER_BLOCK, dtype=F32)
    t = dwc.reshape(nb, 2, S5_GROUPS_PER_BLOCK, S5_STATE, S5_GROUPS_PER_BLOCK, Cg)
    out = jnp.einsum("brgphi,gh->rbgip", t, eye)
    return out[0].reshape(-1, Cg, S5_STATE), -out[1].reshape(-1, Cg, S5_STATE)


ANY = pl.BlockSpec(memory_space=pl.ANY)
N_CHIPS = 4
N_DEV = 8


def _place():
    x, y, c = lax.axis_index("x"), lax.axis_index("y"), lax.axis_index("c")
    chips = [(1 - x, y), (x, 1 - y), (1 - x, 1 - y)]
    return x, y, c, chips


def gather_chips(w, *, name):
    R, C = w.shape
    Hh = R // 2
    assert 2 * Hh == R

    def body(w_ref, out_ref, send_sems, recv_sems):
        x, y, c, chips = _place()
        me = 2 * x + y
        sibling = (x, y, 1 - c)

        def half(chip, hc):
            return out_ref.at[chip, pl.ds(hc * Hh, Hh), :]

        def copy(k, src, dst, to):
            return pltpu.make_async_remote_copy(src_ref=src, dst_ref=dst, send_sem=send_sems.at[k],
                                                recv_sem=recv_sems.at[k], device_id=to, device_id_type=MESH)

        first = [copy(j, w_ref.at[pl.ds(c * Hh, Hh), :], half(me, c), (px, py, c)) for j, (px, py) in enumerate(chips)]
        for cp in first:
            cp.start()
        passed = []
        for j, (px, py) in enumerate(chips):
            landed = half(2 * px + py, c)
            copy(j, landed, landed, (px, py, c)).wait_recv()
            fwd = copy(3 + j, landed, landed, sibling)
            fwd.start()
            passed.append(fwd)
        for j, (px, py) in enumerate(chips):
            theirs = half(2 * px + py, 1 - c)
            copy(3 + j, theirs, theirs, sibling).wait_recv()
        for cp in first + passed:
            cp.wait_send()

    others = pl.pallas_call(
        body, in_specs=[ANY], out_specs=ANY, out_shape=jax.ShapeDtypeStruct((N_CHIPS, R, C), w.dtype),
        scratch_shapes=[pltpu.SemaphoreType.DMA((6,)), pltpu.SemaphoreType.DMA((6,))], name=name,
    )(w)
    return _place_rows(others, w, 2 * lax.axis_index("x") + lax.axis_index("y"), name=name + "_own")


def _place_rows(buf, rows, slot, *, name, tr=592):
    n, R, C = buf.shape
    tr = _tile(R, tr, 16)
    idx = jnp.asarray(slot, jnp.int32).reshape(1)

    def body(s_ref, r_ref, b_ref, o_ref):
        o_ref[...] = r_ref[...]

    return pl.pallas_call(
        body,
        grid_spec=pltpu.PrefetchScalarGridSpec(
            num_scalar_prefetch=1, grid=(R // tr,),
            in_specs=[pl.BlockSpec((tr, C), lambda i, s: (i, 0)), ANY],
            out_specs=pl.BlockSpec((None, tr, C), lambda i, s: (s[0], i, 0))),
        out_shape=jax.ShapeDtypeStruct(buf.shape, buf.dtype), input_output_aliases={2: 0}, name=name,
        compiler_params=_params(("parallel",), 2 * tr * C * 4),
    )(idx, rows, buf)


def swap_halves(g4, *, name):
    n, R, C = g4.shape
    Hh = R // 2

    def body(g_ref, out_ref, send_sem, recv_sem):
        x, y, c, _ = _place()
        cp = pltpu.make_async_remote_copy(
            src_ref=g_ref.at[pl.ds(0, n), pl.ds((1 - c) * Hh, Hh), :], dst_ref=out_ref, send_sem=send_sem,
            recv_sem=recv_sem, device_id=(x, y, 1 - c), device_id_type=MESH)
        cp.start()
        cp.wait()

    return pl.pallas_call(
        body, in_specs=[ANY], out_specs=ANY, out_shape=jax.ShapeDtypeStruct((n, Hh, C), g4.dtype),
        scratch_shapes=[pltpu.SemaphoreType.DMA, pltpu.SemaphoreType.DMA], name=name,
    )(g4)


def add_half(g4, other, *, name, tr=160):
    n, R, C = g4.shape
    Hh = R // 2
    tr = _tile(Hh, tr, 16)
    nblk = Hh // tr
    cidx = lax.axis_index("c").astype(jnp.int32).reshape(1)

    def body(c_ref, g_ref, o_ref, out_ref):
        out_ref[...] = (g_ref[...].astype(F32) + o_ref[...].astype(F32)).astype(out_ref.dtype)

    return pl.pallas_call(
        body,
        grid_spec=pltpu.PrefetchScalarGridSpec(
            num_scalar_prefetch=1, grid=(nblk,),
            in_specs=[pl.BlockSpec((n, tr, C), lambda i, c_ref: (0, c_ref[0] * nblk + i, 0)),
                      pl.BlockSpec((n, tr, C), lambda i, c_ref: (0, i, 0))],
            out_specs=pl.BlockSpec((n, tr, C), lambda i, c_ref: (0, i, 0))),
        out_shape=jax.ShapeDtypeStruct((n, Hh, C), g4.dtype), name=name,
        compiler_params=_params(("parallel",), 3 * n * tr * C * 2),
    )(cidx, g4, other)


def scatter_chips(p4, *, name):
    n, Hh, C = p4.shape

    def body(p_ref, out_ref, send_sems, recv_sems):
        x, y, c, chips = _place()
        me = 2 * x + y
        sends = []
        for j, (px, py) in enumerate(chips):
            cp = pltpu.make_async_remote_copy(src_ref=p_ref.at[2 * px + py], dst_ref=out_ref.at[me], send_sem=send_sems.at[j],
                                              recv_sem=recv_sems.at[j], device_id=(px, py, c), device_id_type=MESH)
            cp.start()
            sends.append(cp)
        for j, (px, py) in enumerate(chips):
            slot = out_ref.at[2 * px + py]
            pltpu.make_async_remote_copy(src_ref=slot, dst_ref=slot, send_sem=send_sems.at[j], recv_sem=recv_sems.at[j],
                                         device_id=(px, py, c), device_id_type=MESH).wait_recv()
        for cp in sends:
            cp.wait_send()

    return pl.pallas_call(
        body, in_specs=[ANY], out_specs=ANY, out_shape=jax.ShapeDtypeStruct((n, Hh, C), p4.dtype),
        scratch_shapes=[pltpu.SemaphoreType.DMA((3,)), pltpu.SemaphoreType.DMA((3,))], name=name,
    )(p4)


def sum_chips(landed, part, *, name, tr=96):
    n, Hh, C = landed.shape
    tr = _tile(Hh, tr, 16)
    nblk = Hh // tr
    x, y, c, _ = _place()
    idx = jnp.stack([2 * x + y, c]).astype(jnp.int32)

    def slot_spec(k):
        return pl.BlockSpec((None, tr, C), lambda i, s: (jnp.where(s[0] == k, (k + 1) % n, k), i, 0))

    def body(s_ref, *refs):
        slots, own_ref, o_ref = refs[:n], refs[n], refs[n + 1]
        own = own_ref[...].astype(F32)
        acc = None
        for k in range(n):
            v = jnp.where(s_ref[0] == k, own, slots[k][...].astype(F32))
            acc = v if acc is None else acc + v
        o_ref[...] = acc

    return pl.pallas_call(
        body,
        grid_spec=pltpu.PrefetchScalarGridSpec(
            num_scalar_prefetch=1, grid=(nblk,),
            in_specs=[slot_spec(k) for k in range(n)] + [pl.BlockSpec((None, tr, C), lambda i, s: (s[0], i, 0))],
            out_specs=pl.BlockSpec((tr, C), lambda i, s: (s[1] * nblk + i, 0))),
        out_shape=jax.ShapeDtypeStruct((2 * Hh, C), F32), name=name,
        compiler_params=_params(("parallel",), 6 * tr * C * 4),
    )(idx, *([landed] * n), part)


def sum_leading(x3, *, name, tr=160, align=16):
    n, R, C = x3.shape
    tr = _tile(R, tr, align)

    def body(x_ref, o_ref):
        acc = x_ref[0].astype(F32)
        for k in range(1, n):
            acc = acc + x_ref[k].astype(F32)
        o_ref[...] = acc

    return pl.pallas_call(
        body, grid=(R // tr,), in_specs=[pl.BlockSpec((n, tr, C), lambda i: (0, i, 0))],
        out_specs=pl.BlockSpec((tr, C), lambda i: (i, 0)), out_shape=jax.ShapeDtypeStruct((R, C), F32), name=name,
        compiler_params=_params(("parallel",), n * tr * C * 4 + tr * C * 4),
    )(x3)


def join_halves(r, *, name):
    R, C = r.shape
    Hh = R // 2

    def body(r_ref, out_ref, send_sem, recv_sem):
        x, y, c, _ = _place()
        mine = out_ref.at[pl.ds(c * Hh, Hh), :]
        theirs = out_ref.at[pl.ds((1 - c) * Hh, Hh), :]
        cp = pltpu.make_async_remote_copy(src_ref=mine, dst_ref=mine, send_sem=send_sem, recv_sem=recv_sem,
                                          device_id=(x, y, 1 - c), device_id_type=MESH)
        cp.start()
        pltpu.make_async_remote_copy(src_ref=theirs, dst_ref=theirs, send_sem=send_sem, recv_sem=recv_sem,
                                     device_id=(x, y, 1 - c), device_id_type=MESH).wait_recv()
        cp.wait_send()

    return pl.pallas_call(
        body, in_specs=[ANY], out_specs=ANY, out_shape=jax.ShapeDtypeStruct((R, C), r.dtype),
        input_output_aliases={0: 0}, scratch_shapes=[pltpu.SemaphoreType.DMA, pltpu.SemaphoreType.DMA], name=name,
    )(r)


def gather_devices(v, *, name):
    m_per, n = v.shape

    def body(x_ref, out_ref, send_sems, recv_sems, local_sem):
        x, y, c, chips = _place()
        me, sibling = (x, y, c), (x, y, 1 - c)

        def rows(px, py, pc):
            return out_ref.at[pl.ds((4 * px + 2 * py + pc) * m_per, m_per), :]

        def copy(k, block, to, src=None):
            return pltpu.make_async_remote_copy(src_ref=rows(*block) if src is None else src, dst_ref=rows(*block),
                                                send_sem=send_sems.at[k], recv_sem=recv_sems.at[k], device_id=to,
                                                device_id_type=MESH)

        mine = pltpu.make_async_copy(x_ref, rows(*me), local_sem)
        mine.start()
        first = [copy(0, me, sibling, src=x_ref)]
        first += [copy(1 + j, me, (*chip, c), src=x_ref) for j, chip in enumerate(chips)]
        for cp in first:
            cp.start()
        passed = [copy(4 + j, (*chip, c), sibling) for j, chip in enumerate(chips)]
        for j, chip in enumerate(chips):
            copy(1 + j, (*chip, c), me).wait_recv()
            passed[j].start()
        copy(0, sibling, me).wait_recv()
        for j, chip in enumerate(chips):
            copy(4 + j, (*chip, 1 - c), me).wait_recv()
        for cp in first + passed:
            cp.wait_send()
        mine.wait()

    return pl.pallas_call(
        body, out_shape=jax.ShapeDtypeStruct((N_DEV * m_per, n), v.dtype),
        in_specs=[pl.BlockSpec(memory_space=pltpu.VMEM)], out_specs=pl.BlockSpec(memory_space=pltpu.VMEM),
        scratch_shapes=[pltpu.SemaphoreType.DMA((7,)), pltpu.SemaphoreType.DMA((7,)), pltpu.SemaphoreType.DMA], name=name,
        compiler_params=pltpu.CompilerParams(vmem_limit_bytes=_vmem_limit(9 * m_per * n * 4)),
    )(v)


def reduce_weight_grads(g4):
    other = swap_halves(g4, name="rs_swap_halves")
    part = add_half(g4, other, name="rs_add_half")
    landed = scatter_chips(part, name="rs_scatter_chips")
    mine = sum_chips(landed, part, name="rs_sum_chips")
    return join_halves(mine, name="rs_join_halves")


PACK_COLS = 1024
BIG = (("pool_w", 2), ("sb_w_qkv", 2), ("sb_w_o", 1), ("s5_w_glu", 2), ("xa_wq", 1), ("xa_wkv", 2), ("xa_wo", 1),
       ("ffn_w_up", 2), ("ffn_w_down", 1))
SMALL_SHARDED = (("pool_scale", 1), ("s5_d", 1), ("ffn_conv_w", 2))
REPLICATED = ("mix_norm_g", "s5_a_re", "s5_a_im", "s5_log_dt", "s5_b_re", "s5_b_im", "s5_c_re", "s5_c_im",
              "xa_norm_g", "mem_norm_g", "ffn_norm_g", "ffn_conv_b", "final_norm_g")
WEIGHTS = ("mix_norm_g", "pool_w", "pool_scale", "sb_w_qkv", "sb_w_o", "s5_a_re", "s5_a_im", "s5_log_dt", "s5_b_re",
           "s5_b_im", "s5_c_re", "s5_c_im", "s5_d", "s5_w_glu", "xa_norm_g", "mem_norm_g", "xa_wq", "xa_wkv", "xa_wo",
           "ffn_norm_g", "ffn_w_up", "ffn_conv_w", "ffn_conv_b", "ffn_w_down", "final_norm_g")


def _pack_rows(parts, row_align):
    flat = jnp.concatenate(parts, axis=-1)
    n = flat.shape[-1]
    per = PACK_COLS * row_align
    padded = -(-n // per) * per
    if padded != n:
        flat = jnp.pad(flat, [(0, 0)] * (flat.ndim - 1) + [(0, padded - n)])
    return flat.reshape(*flat.shape[:-1], padded // PACK_COLS, PACK_COLS)


def _to_natural(g, ax):
    g = jnp.moveaxis(g, 0, ax)
    sh = g.shape
    return g.reshape(*sh[:ax], sh[ax] * sh[ax + 1], *sh[ax + 2:])


def _to_chunks(a, ax, n=N_CHIPS):
    sh = a.shape
    a = a.reshape(*sh[:ax], n, sh[ax] // n, *sh[ax + 1:])
    return jnp.moveaxis(a, ax, 0).reshape(n, -1)


def _unpack(flat, shapes):
    out, off = [], 0
    for sh in shapes:
        n = math.prod(sh)
        out.append(flat[..., off:off + n].reshape(*flat.shape[:-1], *sh))
        off += n
    return out


def _mixer_kind(i):
    return i % 3, i // 3


def _s5_forward(hn, h, s5, tag):
    bu = bdmm(hn, s5["w_in"], out_dtype=F32, name=f"{tag}_s5_bu")
    xs = s5_scan(bu, s5["pw_fwd"], reverse=False, name=f"{tag}_s5_scan")
    ycx = bdmm(xs, s5["w_out"], out_dtype=F32, name=f"{tag}_s5_cx")
    D = hn.shape[1]

    def post(i, yv, uv, dv):
        return [_gelu(yv + dv * uv)]

    yg = ew(post, [(ycx, "tile"), (hn, "tile"), (s5["d"], "full")], [(D, BF16, "tile")], rows=hn.shape[0], tr=256,
            name=f"{tag}_s5_gelu")[0]
    vg = mm(yg, s5["w_glu"], out_dtype=F32, name=f"{tag}_s5_glu")

    def glu(i, vgv, hv):
        return [hv + vgv[:, :D] * _sigmoid(vgv[:, D:])]

    h1 = ew(glu, [(vg, "tile"), (h, "tile")], [(D, F32, "tile")], rows=hn.shape[0], tr=256, name=f"{tag}_s5_gate")[0]
    return h1, dict(xs=xs, ycx=ycx, yg=yg, vg=vg)


def _s5_backward(hn, dout, s5, sv, tag):
    S, D = hn.shape

    def dglu(i, vgv, dv):
        sg = _sigmoid(vgv[:, D:])
        return [jnp.concatenate([dv * sg, dv * vgv[:, :D] * sg * (1.0 - sg)], axis=1)]

    dvg = ew(dglu, [(sv["vg"], "tile"), (dout, "tile")], [(2 * D, BF16, "tile")], rows=S, tr=256, name=f"{tag}_s5_dgate")[0]
    dyg = mm(dvg, s5["w_glu"], tb=True, out_dtype=F32, name=f"{tag}_s5_dglu_x")
    dw_glu = mm(sv["yg"], dvg, ta=True, out_dtype=BF16, name=f"{tag}_s5_dglu_w")

    def dgelu(i, dygv, yv, uv, dv):
        dyp = dygv * _gelu_grad(yv + dv * uv)
        return [dyp, _colsum(dyp * uv)]

    dyp, dd = ew(dgelu, [(dyg, "tile"), (sv["ycx"], "tile"), (hn, "tile"), (s5["d"], "full")],
                 [(D, F32, "tile"), (D, F32, "acc")], rows=S, tr=256, name=f"{tag}_s5_dgelu")
    gx = bdmm(dyp, s5["w_out_t"], out_dtype=F32, name=f"{tag}_s5_dcx")
    dw_out = bdmm_tn(sv["xs"], dyp, ka=S5_BLOCK, kd=S5_GROUPS_PER_BLOCK * S5_GROUP, name=f"{tag}_s5_dwout")
    lam = s5_scan(gx, s5["pw_bwd"], reverse=True, name=f"{tag}_s5_scan_bwd")
    da = s5_da(lam, sv["xs"], name=f"{tag}_s5_da")
    dw_in = bdmm_tn(hn, lam, ka=S5_GROUPS_PER_BLOCK * S5_GROUP, kd=S5_BLOCK, name=f"{tag}_s5_dwin")
    du = bdmm(lam, s5["w_in_t"], out_dtype=F32, name=f"{tag}_s5_du")

    def dsum(i, duv, dypv, dv):
        return [duv + dypv * dv]

    dhn = ew(dsum, [(du, "tile"), (dyp, "tile"), (s5["d"], "full")], [(D, F32, "tile")], rows=S, tr=256,
             name=f"{tag}_s5_dhn")[0]
    return dhn, dict(dw_glu=dw_glu, dd=dd, dw_out=dw_out, dw_in=dw_in, da=da)


def _step(x, mem, target, w, m, v):
    S, D = x.shape
    depth = w["mix_norm_g"].shape[0]
    F = w["ffn_w_down"].shape[1] * N_CHIPS
    chip = 2 * lax.axis_index("x") + lax.axis_index("y")

    big_shapes = [w[n].shape for n, _ in BIG]
    packed = _pack_rows([w[n].astype(BF16).reshape(-1) for n, _ in BIG], 32)
    gathered = gather_chips(packed, name="ag_weights").reshape(N_CHIPS, -1)
    full = {n: _to_natural(p, ax) for (n, ax), p in zip(BIG, _unpack(gathered, big_shapes))}
    small_shapes = [w[n].shape for n, _ in SMALL_SHARDED]
    spacked = _pack_rows([w[n].reshape(-1) for n, _ in SMALL_SHARDED], SUBLANE)
    sgathered = gather_devices(spacked, name="ag_small").reshape(N_CHIPS, 2, -1)[:, 0]
    full.update({n: _to_natural(p, ax) for (n, ax), p in zip(SMALL_SHARDED, _unpack(sgathered, small_shapes))})

    n_s5 = w["s5_a_re"].shape[0]
    s5 = []
    for j in range(n_s5):
        G = w["s5_a_re"].shape[1]
        nb = G // S5_GROUPS_PER_BLOCK
        prm = (w["s5_a_re"][j], w["s5_a_im"][j], w["s5_log_dt"][j], w["s5_b_re"][j], w["s5_b_im"][j])
        (ab_re, ab_im, bb_re, bb_im), disc_vjp = jax.vjp(_s5_discretize, *prm)
        w_in = _s5_in_weights(bb_re, bb_im, nb)
        w_out = _s5_out_weights(w["s5_c_re"][j], w["s5_c_im"][j], nb)
        s5.append(dict(
            w_in=w_in, w_in_t=jnp.transpose(w_in, (0, 2, 1)), w_out=w_out, w_out_t=jnp.transpose(w_out, (0, 2, 1)),
            pw_fwd=_s5_powers(prm[0], prm[1], prm[2], nb, reverse=False),
            pw_bwd=_s5_powers(prm[0], prm[1], prm[2], nb, reverse=True),
            d=full["s5_d"][j][None], w_glu=full["s5_w_glu"][j], vjp=disc_vjp, nb=nb))

    h = x
    saved = []
    for i in range(depth):
        kind, j = _mixer_kind(i)
        tag = f"L{i}"
        sv = dict(h=h)
        g_mix = w["mix_norm_g"][i][None]
        if kind == 0:
            hn = rms_fwd(h, g_mix, out_dtype=F32, name=f"{tag}_mix_norm")
            h1 = pool_fwd(hn, h, full["pool_w"][j], full["pool_scale"][j][None], name=f"{tag}_pool")
        elif kind == 1:
            hn = rms_fwd(h, g_mix, out_dtype=BF16, name=f"{tag}_mix_norm")
            qkv = mm(hn, full["sb_w_qkv"][j], out_dtype=BF16, name=f"{tag}_sb_qkv")
            o = sb_fwd(qkv, name=f"{tag}_sb_attn")
            h1 = mm(o, full["sb_w_o"][j], res=h, name=f"{tag}_sb_out")
            sv.update(qkv=qkv, o=o)
        else:
            hn = rms_fwd(h, g_mix, out_dtype=F32, name=f"{tag}_mix_norm")
            h1, s5sv = _s5_forward(hn, h, s5[j], tag)
            sv.update(s5sv)
        sv.update(hn=hn, h1=h1)
        hq = rms_fwd(h1, w["xa_norm_g"][i][None], out_dtype=BF16, name=f"{tag}_xa_norm")
        memn = rms_fwd(mem, w["mem_norm_g"][i][None], out_dtype=BF16, name=f"{tag}_mem_norm", tr=mem.shape[0])
        q = mm(hq, full["xa_wq"][i], out_dtype=BF16, name=f"{tag}_xa_q")
        kv = mm(memn, full["xa_wkv"][i], out_dtype=BF16, name=f"{tag}_xa_kv")
        oa = xa_fwd(q, kv, name=f"{tag}_xa_attn")
        h2 = mm(oa, full["xa_wo"][i], res=h1, name=f"{tag}_xa_out")
        hf = rms_fwd(h2, w["ffn_norm_g"][i][None], out_dtype=BF16, name=f"{tag}_ffn_norm")
        uu = mm(hf, full["ffn_w_up"][i], out_dtype=BF16, tn=1408, name=f"{tag}_ffn_up")
        conv_w, conv_b = full["ffn_conv_w"][i], w["ffn_conv_b"][i][None]
        act = ffn_act_fwd(uu, conv_w, conv_b, name=f"{tag}_ffn_act")
        h3 = mm(act, full["ffn_w_down"][i], res=h2, tk=1408, name=f"{tag}_ffn_down")
        sv.update(hq=hq, memn=memn, q=q, kv=kv, oa=oa, h2=h2, hf=hf, uu=uu, act=act)
        saved.append(sv)
        h = h3

    dh, g_final, loss = loss_head(h, w["final_norm_g"][None], target, name="loss_head")

    gw = {n: [None] * w[n].shape[0] for n in WEIGHTS if n != "final_norm_g"}
    for i in reversed(range(depth)):
        kind, j = _mixer_kind(i)
        tag = f"L{i}b"
        sv = saved[i]
        conv_w, conv_b = full["ffn_conv_w"][i], w["ffn_conv_b"][i][None]
        dact = mm(dh, full["ffn_w_down"][i], tb=True, out_dtype=BF16, tn=1408, name=f"{tag}_ffn_down_x")
        gw["ffn_w_down"][i] = mm(sv["act"], dh, ta=True, out_dtype=BF16, tm=1408, name=f"{tag}_ffn_down_w")
        duv, dug, dcw, dcb = ffn_act_bwd(sv["uu"], dact, conv_w, conv_b, name=f"{tag}_ffn_act")
        gw["ffn_conv_w"][i], gw["ffn_conv_b"][i] = dcw, dcb[0]
        dhf = mm(duv, full["ffn_w_up"][i], tb=True, b_col0=0, tk=1408, name=f"{tag}_ffn_up_xv")
        dhf = mm(dug, full["ffn_w_up"][i], tb=True, b_col0=F, res=dhf, tk=1408, name=f"{tag}_ffn_up_xg")
        gw["ffn_w_up"][i] = jnp.concatenate(
            [mm(sv["hf"], duv, ta=True, out_dtype=BF16, tn=1408, name=f"{tag}_ffn_up_wv"),
             mm(sv["hf"], dug, ta=True, out_dtype=BF16, tn=1408, name=f"{tag}_ffn_up_wg")], axis=1)
        dh2, dg = rms_bwd(sv["h2"], w["ffn_norm_g"][i][None], dhf, dh, name=f"{tag}_ffn_norm")
        gw["ffn_norm_g"][i] = dg[0]

        doa = mm(dh2, full["xa_wo"][i], tb=True, out_dtype=BF16, name=f"{tag}_xa_out_x")
        gw["xa_wo"][i] = mm(sv["oa"], dh2, ta=True, out_dtype=BF16, name=f"{tag}_xa_out_w")
        dq, dkv = xa_bwd(sv["q"], sv["kv"], doa, name=f"{tag}_xa_attn")
        dhq = mm(dq, full["xa_wq"][i], tb=True, name=f"{tag}_xa_q_x")
        gw["xa_wq"][i] = mm(sv["hq"], dq, ta=True, out_dtype=BF16, name=f"{tag}_xa_q_w")
        dmemn = mm(dkv, full["xa_wkv"][i], tb=True, name=f"{tag}_xa_kv_x")
        gw["xa_wkv"][i] = mm(sv["memn"], dkv, ta=True, out_dtype=BF16, name=f"{tag}_xa_kv_w")
        gw["mem_norm_g"][i] = rms_bwd_g(mem, dmemn, name=f"{tag}_mem_norm")[0]
        dh1, dg = rms_bwd(sv["h1"], w["xa_norm_g"][i][None], dhq, dh2, name=f"{tag}_xa_norm")
        gw["xa_norm_g"][i] = dg[0]

        g_mix = w["mix_norm_g"][i][None]
        if kind == 0:
            dp, dpw, dps = pool_bwd_w(sv["hn"], dh1, full["pool_w"][j], full["pool_scale"][j][None], name=f"{tag}_pool_w")
            gw["pool_w"][j], gw["pool_scale"][j] = dpw, dps[0]
            dhn = pool_bwd_x(dp, len(POOL_WINDOWS), name=f"{tag}_pool_x")
        elif kind == 1:
            do = mm(dh1, full["sb_w_o"][j], tb=True, name=f"{tag}_sb_out_x")
            gw["sb_w_o"][j] = mm(sv["o"], dh1, ta=True, out_dtype=BF16, name=f"{tag}_sb_out_w")
            dq3 = sb_bwd(sv["qkv"], sv["o"], do, name=f"{tag}_sb_attn")
            dqkv = jnp.concatenate([t.astype(BF16) for t in dq3], axis=1)
            dhn = mm(dqkv, full["sb_w_qkv"][j], tb=True, name=f"{tag}_sb_qkv_x")
            gw["sb_w_qkv"][j] = mm(sv["hn"], dqkv, ta=True, out_dtype=BF16, name=f"{tag}_sb_qkv_w")
        else:
            dhn, sg = _s5_backward(sv["hn"], dh1, s5[j], sv, tag)
            Cg = w["s5_b_re"].shape[-1]
            nb = s5[j]["nb"]
            gw["s5_w_glu"][j], gw["s5_d"][j] = sg["dw_glu"], sg["dd"][0]
            da = sg["da"].reshape(nb, 2, -1)
            gw["s5_a_re"][j], gw["s5_a_im"][j] = da[:, 0].reshape(-1, S5_STATE), da[:, 1].reshape(-1, S5_STATE)
            gw["s5_b_re"][j], gw["s5_b_im"][j] = _s5_in_weight_grads(sg["dw_in"], Cg)
            gw["s5_c_re"][j], gw["s5_c_im"][j] = _s5_out_weight_grads(sg["dw_out"], Cg)
        dh, dg = rms_bwd(sv["h"], g_mix, dhn, dh1, name=f"{tag}_mix_norm")
        gw["mix_norm_g"][i] = dg[0]
    grad_x = dh

    g4 = _pack_rows([_to_chunks(jnp.stack(gw[n]).astype(BF16), ax) for n, ax in BIG], 32)
    reduced = reduce_weight_grads(g4).reshape(-1)
    grads = dict(zip([n for n, _ in BIG], _unpack(reduced, big_shapes)))

    s5_raw = ("s5_a_re", "s5_a_im", "s5_b_re", "s5_b_im")
    small_names = [n for n in REPLICATED if n not in ("final_norm_g", "s5_log_dt")] + [n for n, _ in SMALL_SHARDED]
    small_full = [jnp.stack(gw[n]).astype(F32) for n in small_names] + [g_final[0]]
    small_full_shapes = [t.shape for t in small_full]
    spk = _pack_rows([t.reshape(-1) for t in small_full], SUBLANE)
    everyone = gather_devices(spk, name="ar_small_gather").reshape(N_DEV, *spk.shape)
    ssum = sum_leading(everyone, name="ar_small_sum", align=SUBLANE).reshape(-1)
    small = dict(zip(small_names + ["final_norm_g"], _unpack(ssum, small_full_shapes)))
    per_layer = [[], [], [], [], []]
    for j in range(n_s5):
        ct = tuple(small[n][j] for n in s5_raw)
        for lst, gpart in zip(per_layer, s5[j]["vjp"]((ct[0], ct[1], ct[2], ct[3]))):
            lst.append(gpart)
    for n, lst in zip(("s5_a_re", "s5_a_im", "s5_log_dt", "s5_b_re", "s5_b_im"), per_layer):
        small[n] = jnp.stack(lst)
    for n, ax in SMALL_SHARDED:
        chunks = _to_chunks(small[n], ax)
        small[n] = lax.dynamic_index_in_dim(chunks, chip, 0, keepdims=False).reshape(w[n].shape)
    for n in REPLICATED:
        grads[n] = small[n].reshape(w[n].shape)
    for n, _ in SMALL_SHARDED:
        grads[n] = small[n]

    delta, new_m, new_v = {}, {}, {}
    for n in WEIGHTS:
        delta[n], new_m[n], new_v[n] = adamw(w[n], grads[n], m[n], v[n], name=f"adamw_{n}")
    total = lax.psum(loss[0, 0], ("x", "y", "c"))
    return (total, grad_x, *[grads[n] for n in WEIGHTS], *[delta[n] for n in WEIGHTS],
            *[new_m[n] for n in WEIGHTS], *[new_v[n] for n in WEIGHTS])


def kernel(x, mem, mix_norm_g, pool_w, pool_scale, sb_w_qkv, sb_w_o, s5_a_re, s5_a_im, s5_log_dt, s5_b_re, s5_b_im,
           s5_c_re, s5_c_im, s5_d, s5_w_glu, xa_norm_g, mem_norm_g, xa_wq, xa_wkv, xa_wo, ffn_norm_g, ffn_w_up,
           ffn_conv_w, ffn_conv_b, ffn_w_down, final_norm_g, loss_target, m_mix_norm_g, m_pool_w, m_pool_scale,
           m_sb_w_qkv, m_sb_w_o, m_s5_a_re, m_s5_a_im, m_s5_log_dt, m_s5_b_re, m_s5_b_im, m_s5_c_re, m_s5_c_im,
           m_s5_d, m_s5_w_glu, m_xa_norm_g, m_mem_norm_g, m_xa_wq, m_xa_wkv, m_xa_wo, m_ffn_norm_g, m_ffn_w_up,
           m_ffn_conv_w, m_ffn_conv_b, m_ffn_w_down, m_final_norm_g, v_mix_norm_g, v_pool_w, v_pool_scale,
           v_sb_w_qkv, v_sb_w_o, v_s5_a_re, v_s5_a_im, v_s5_log_dt, v_s5_b_re, v_s5_b_im, v_s5_c_re, v_s5_c_im,
           v_s5_d, v_s5_w_glu, v_xa_norm_g, v_mem_norm_g, v_xa_wq, v_xa_wkv, v_xa_wo, v_ffn_norm_g, v_ffn_w_up,
           v_ffn_conv_w, v_ffn_conv_b, v_ffn_w_down, v_final_norm_g):
    given = dict(locals())
    w = {n: given[n] for n in WEIGHTS}
    m = {n: given["m_" + n] for n in WEIGHTS}
    v = {n: given["v_" + n] for n in WEIGHTS}
    out = _step(x[0], mem[0], loss_target[0], w, m, v)
    return (out[0], out[1][None], *out[2:])
```

```python
import functools
import math

import jax
import jax.numpy as jnp
from jax import lax
from jax.experimental import pallas as pl
from jax.experimental.pallas import tpu as pltpu

F32 = jnp.float32
BF16 = jnp.bfloat16
MESH = pl.DeviceIdType.MESH

EPS = 1e-6
POOL_WINDOWS = (2, 4, 8, 16)
POOL_HALO = 128
POOL_TILE = 256
SB_HEAD_DIM = 64
SB_BLOCK = 128
S5_GROUP = 16
S5_STATE = 64
S5_GROUPS_PER_BLOCK = 8
XA_HEADS = 4
CONV_WIDTH = 3
ADAM_LR, ADAM_B1, ADAM_B2, ADAM_EPS, ADAM_WD, ADAM_STEP = 0.001, 0.9, 0.999, 1e-08, 0.01, 10

V7X_VMEM_BYTES = 64 * 1024 * 1024
LANE = 128
SUBLANE = 8


def _vmem_limit(block_bytes):
    want = 2 * block_bytes + 16 * 1024 * 1024
    return int(min(V7X_VMEM_BYTES - 6 * 1024 * 1024, max(32 * 1024 * 1024, want)))


def _params(sem, block_bytes):
    return pltpu.CompilerParams(dimension_semantics=sem, vmem_limit_bytes=_vmem_limit(block_bytes))


def _tile(n, cap, align=LANE):
    if n <= cap:
        return n
    t = (cap // align) * align
    while t >= align:
        if n % t == 0:
            return t
        t -= align
    return n


def _nbytes(shape, dtype):
    return math.prod(shape) * jnp.dtype(dtype).itemsize


def mm(a, b, *, ta=False, tb=False, out_dtype=F32, res=None, b_col0=None, name, tm=1024, tn=1024, tk=1024):
    if ta:
        K, M = a.shape
    else:
        M, K = a.shape
    if tb:
        N, Kb = b.shape
    else:
        Kb, N = b.shape
    if b_col0 is None:
        assert K == Kb, (a.shape, b.shape, ta, tb)
    tm, tn, tk = _tile(M, tm), _tile(N, tn), _tile(K, tk)
    nk = K // tk
    koff = 0
    if b_col0 is not None:
        assert tb and b_col0 % tk == 0 and b_col0 + K <= Kb
        koff = b_col0 // tk
    dims = (((0,) if ta else (1,), (1,) if tb else (0,)), ((), ()))

    def body(*refs):
        if res is None:
            a_ref, b_ref, o_ref, acc = refs
            r_ref = None
        else:
            a_ref, b_ref, r_ref, o_ref, acc = refs
        k = pl.program_id(2)

        @pl.when(k == 0)
        def _():
            acc[...] = jnp.zeros_like(acc)

        acc[...] += lax.dot_general(a_ref[...].astype(BF16), b_ref[...].astype(BF16), dims,
                                    preferred_element_type=F32)

        @pl.when(k == nk - 1)
        def _():
            r = acc[...]
            if r_ref is not None:
                r = r + r_ref[...].astype(F32)
            o_ref[...] = r.astype(out_dtype)

    a_spec = pl.BlockSpec((tk, tm), lambda i, j, k: (k, i)) if ta else pl.BlockSpec((tm, tk), lambda i, j, k: (i, k))
    b_spec = pl.BlockSpec((tn, tk), lambda i, j, k: (j, k + koff)) if tb else pl.BlockSpec((tk, tn), lambda i, j, k: (k, j))
    in_specs = [a_spec, b_spec]
    args = [a, b]
    blk = _nbytes((tm, tk), a.dtype) + _nbytes((tk, tn), b.dtype) + _nbytes((tm, tn), out_dtype)
    if res is not None:
        in_specs.append(pl.BlockSpec((tm, tn), lambda i, j, k: (i, j)))
        args.append(res)
        blk += _nbytes((tm, tn), res.dtype)
    return pl.pallas_call(
        body, grid=(M // tm, N // tn, nk), in_specs=in_specs,
        out_specs=pl.BlockSpec((tm, tn), lambda i, j, k: (i, j)),
        out_shape=jax.ShapeDtypeStruct((M, N), out_dtype),
        scratch_shapes=[pltpu.VMEM((tm, tn), F32)], name=name,
        compiler_params=_params(("parallel", "parallel", "arbitrary"), blk + _nbytes((tm, tn), F32)),
    )(*args)


def bdmm(a, w, *, out_dtype, name, tm=512):
    M = a.shape[0]
    nb, ka, kn = w.shape
    tm = _tile(M, tm)

    def body(a_ref, w_ref, o_ref):
        o_ref[...] = jnp.dot(a_ref[...].astype(BF16), w_ref[...].astype(BF16),
                             preferred_element_type=F32).astype(out_dtype)

    blk = _nbytes((tm, ka), a.dtype) + _nbytes((ka, kn), w.dtype) + _nbytes((tm, kn), out_dtype)
    return pl.pallas_call(
        body, grid=(M // tm, nb),
        in_specs=[pl.BlockSpec((tm, ka), lambda i, b: (i, b)), pl.BlockSpec((None, ka, kn), lambda i, b: (b, 0, 0))],
        out_specs=pl.BlockSpec((tm, kn), lambda i, b: (i, b)),
        out_shape=jax.ShapeDtypeStruct((M, nb * kn), out_dtype), name=name,
        compiler_params=_params(("parallel", "parallel"), blk),
    )(a, w)


def bdmm_tn(a, d, *, ka, kd, name, tm=512):
    M = a.shape[0]
    nb = a.shape[1] // ka
    assert d.shape[1] == nb * kd
    tm = _tile(M, tm)

    def body(a_ref, d_ref, o_ref):
        i = pl.program_id(1)
        v = lax.dot_general(a_ref[...].astype(BF16), d_ref[...].astype(BF16), (((0,), (0,)), ((), ())),
                            preferred_element_type=F32)

        @pl.when(i == 0)
        def _():
            o_ref[...] = v

        @pl.when(i > 0)
        def _():
            o_ref[...] += v

    blk = _nbytes((tm, ka), a.dtype) + _nbytes((tm, kd), d.dtype) + _nbytes((ka, kd), F32)
    return pl.pallas_call(
        body, grid=(nb, M // tm),
        in_specs=[pl.BlockSpec((tm, ka), lambda b, i: (i, b)), pl.BlockSpec((tm, kd), lambda b, i: (i, b))],
        out_specs=pl.BlockSpec((None, ka, kd), lambda b, i: (b, 0, 0)),
        out_shape=jax.ShapeDtypeStruct((nb, ka, kd), F32), name=name,
        compiler_params=_params(("parallel", "arbitrary"), blk),
    )(a, d)


def ew(fn, ins, outs, *, rows, tr, name):
    n = rows // tr
    assert n * tr == rows
    in_specs, args, blk = [], [], 0
    for a, kind in ins:
        if kind == "tile":
            assert a.shape[0] == rows, (name, a.shape, rows)
            in_specs.append(pl.BlockSpec((tr, a.shape[1]), lambda i: (i, 0)))
            blk += _nbytes((tr, a.shape[1]), a.dtype)
        else:
            in_specs.append(pl.BlockSpec(a.shape, lambda i, nd=a.ndim: (0,) * nd))
            blk += _nbytes(a.shape, a.dtype)
        args.append(a)
    out_shape, out_specs = [], []
    for c, dt, kind in outs:
        if kind == "tile":
            out_shape.append(jax.ShapeDtypeStruct((rows, c), dt))
            out_specs.append(pl.BlockSpec((tr, c), lambda i: (i, 0)))
            blk += _nbytes((tr, c), dt)
        else:
            out_shape.append(jax.ShapeDtypeStruct((1, c), dt))
            out_specs.append(pl.BlockSpec((1, c), lambda i: (0, 0)))
    nin = len(ins)

    def body(*refs):
        i = pl.program_id(0)
        vals = fn(i, *[r[...] for r in refs[:nin]])
        for (c, dt, kind), o, v in zip(outs, refs[nin:], vals):
            if kind == "tile":
                o[...] = v.astype(dt)
            else:
                @pl.when(i == 0)
                def _():
                    o[...] = v.astype(dt)

                @pl.when(i > 0)
                def _():
                    o[...] += v.astype(dt)

    has_acc = any(k == "acc" for _, _, k in outs)
    return pl.pallas_call(
        body, grid=(n,), in_specs=in_specs, out_specs=out_specs, out_shape=out_shape, name=name,
        compiler_params=_params(("arbitrary" if has_acc else "parallel",), 3 * blk),
    )(*args)


def _colsum(x):
    return jnp.sum(x, axis=0, keepdims=True)


def rms_fwd(x, g, *, out_dtype, name, tr=256):
    def fn(i, xv, gv):
        r = lax.rsqrt(jnp.mean(xv * xv, axis=-1, keepdims=True) + EPS)
        return [xv * r * gv]

    return ew(fn, [(x, "tile"), (g, "full")], [(x.shape[1], out_dtype, "tile")], rows=x.shape[0], tr=tr, name=name)[0]


def rms_bwd(x, g, dy, dres, *, name, tr=256):
    def fn(i, xv, gv, dyv, drv):
        dyv = dyv.astype(F32)
        r = lax.rsqrt(jnp.mean(xv * xv, axis=-1, keepdims=True) + EPS)
        xh = xv * r
        gy = dyv * gv
        dx = r * (gy - xh * jnp.mean(gy * xh, axis=-1, keepdims=True))
        return [dx + drv, _colsum(dyv * xh)]

    D = x.shape[1]
    return ew(fn, [(x, "tile"), (g, "full"), (dy, "tile"), (dres, "tile")], [(D, F32, "tile"), (D, F32, "acc")],
              rows=x.shape[0], tr=tr, name=name)


def rms_bwd_g(x, dy, *, name, tr=256):
    def fn(i, xv, dyv):
        r = lax.rsqrt(jnp.mean(xv * xv, axis=-1, keepdims=True) + EPS)
        return [_colsum(dyv.astype(F32) * xv * r)]

    return ew(fn, [(x, "tile"), (dy, "tile")], [(x.shape[1], F32, "acc")], rows=x.shape[0],
              tr=_tile(x.shape[0], tr, SUBLANE), name=name)[0]


def loss_head(h, g, target, *, name, tr=256):
    D = h.shape[1]

    def fn(i, xv, gv, tv):
        r = lax.rsqrt(jnp.mean(xv * xv, axis=-1, keepdims=True) + EPS)
        xh = xv * r
        err = xh * gv - tv
        dy = err * (1.0 / D)
        gy = dy * gv
        dx = r * (gy - xh * jnp.mean(gy * xh, axis=-1, keepdims=True))
        part = _colsum(err * err) * (0.5 / D)
        return [dx, _colsum(dy * xh), jnp.sum(part, axis=1, keepdims=True)]

    return ew(fn, [(h, "tile"), (g, "full"), (target, "tile")], [(D, F32, "tile"), (D, F32, "acc"), (1, F32, "acc")],
              rows=h.shape[0], tr=tr, name=name)


def _as2d(a):
    if a.ndim >= 2 and a.shape[-1] >= LANE:
        return a.reshape(-1, a.shape[-1])
    if a.size % (8 * LANE) == 0:
        return a.reshape(-1, 8 * LANE)
    return a.reshape(1, -1)


def adamw(w, g, m, v, *, name):
    shape = w.shape
    w2, g2, m2, v2 = (_as2d(t) for t in (w, g.astype(F32).reshape(shape), m, v))
    R, C = w2.shape
    tr = R
    if R * C * 4 > (1 << 20):
        tr = _tile(R, max(SUBLANE, (1 << 20) // (C * 4) // SUBLANE * SUBLANE), SUBLANE)
    c1 = 1.0 / (1.0 - ADAM_B1 ** ADAM_STEP)
    c2 = 1.0 / (1.0 - ADAM_B2 ** ADAM_STEP)

    def fn(i, wv, gv, mv, vv):
        mn = ADAM_B1 * mv + (1.0 - ADAM_B1) * gv
        vn = ADAM_B2 * vv + (1.0 - ADAM_B2) * (gv * gv)
        delta = -ADAM_LR * ((mn * c1) / (jnp.sqrt(vn * c2) + ADAM_EPS) + ADAM_WD * wv)
        return [delta, mn, vn]

    d, mn, vn = ew(fn, [(w2, "tile"), (g2, "tile"), (m2, "tile"), (v2, "tile")], [(C, F32, "tile")] * 3,
                   rows=R, tr=tr, name=name)
    return d.reshape(shape), mn.reshape(shape), vn.reshape(shape)


def _split_bf16(x):
    hi = x.astype(BF16)
    return hi, (x - hi.astype(F32)).astype(BF16)


def _dot2(band, x):
    hi, lo = _split_bf16(x)
    return jnp.dot(band, hi, preferred_element_type=F32) + jnp.dot(band, lo, preferred_element_type=F32)


def _pool_fwd_window(xm, xh, r0, win):
    T = xm.shape[0]
    t = r0 + lax.broadcasted_iota(jnp.int32, (T, 1), 0)
    s_main = r0 + lax.broadcasted_iota(jnp.int32, (1, T), 1)
    s_halo = r0 - POOL_HALO + lax.broadcasted_iota(jnp.int32, (1, POOL_HALO), 1)
    band_m = ((s_main <= t) & (s_main > t - win)).astype(BF16)
    band_h = ((s_halo > t - win) & (s_halo >= 0)).astype(BF16)
    ws = _dot2(band_m, xm) + _dot2(band_h, xh)
    cnt = jnp.minimum(t + 1, win).astype(F32)
    return ws / cnt - xm


def _pool_bwd_window(dm, dh, r0, win, S):
    T = dm.shape[0]
    s = r0 + lax.broadcasted_iota(jnp.int32, (T, 1), 0)
    t_main = r0 + lax.broadcasted_iota(jnp.int32, (1, T), 1)
    t_halo = r0 + T + lax.broadcasted_iota(jnp.int32, (1, POOL_HALO), 1)
    band_m = ((t_main >= s) & (t_main < s + win)).astype(BF16)
    band_h = ((t_halo < s + win) & (t_halo < S)).astype(BF16)
    tm_col = r0 + lax.broadcasted_iota(jnp.int32, (T, 1), 0)
    th_col = r0 + T + lax.broadcasted_iota(jnp.int32, (POOL_HALO, 1), 0)
    dmc = dm / jnp.minimum(tm_col + 1, win).astype(F32)
    dhc = dh / jnp.minimum(th_col + 1, win).astype(F32)
    return _dot2(band_m, dmc) + _dot2(band_h, dhc) - dm


def _pool_specs(T, Cg, order):
    per = T // POOL_HALO
    if order == "ig":
        return (pl.BlockSpec((T, Cg), lambda i, g: (i, g)),
                pl.BlockSpec((POOL_HALO, Cg), lambda i, g: (jnp.maximum(i * per - 1, 0), g)))
    return (pl.BlockSpec((T, Cg), lambda g, i: (i, g)),
            pl.BlockSpec((POOL_HALO, Cg), lambda g, i: (jnp.maximum(i * per - 1, 0), g)))


def pool_fwd(hn, h, w, scale, *, name):
    S, D = hn.shape
    G, Cg, _ = w.shape
    T = _tile(S, POOL_TILE)

    def body(xm_ref, xh_ref, h_ref, w_ref, sc_ref, o_ref):
        i, g = pl.program_id(0), pl.program_id(1)
        win = jnp.left_shift(2, g)
        p = _pool_fwd_window(xm_ref[...], xh_ref[...], i * T, win)
        y = jnp.dot(p.astype(BF16), w_ref[...], preferred_element_type=F32)
        o_ref[...] = h_ref[...] + y * sc_ref[...]

    main, halo = _pool_specs(T, Cg, "ig")
    return pl.pallas_call(
        body, grid=(S // T, G),
        in_specs=[main, halo, main, pl.BlockSpec((None, Cg, Cg), lambda i, g: (g, 0, 0)),
                  pl.BlockSpec((1, Cg), lambda i, g: (0, g))],
        out_specs=main, out_shape=jax.ShapeDtypeStruct((S, D), F32), name=name,
        compiler_params=_params(("parallel", "parallel"), 4 * T * Cg * 4),
    )(hn, hn, h, w, scale)


def pool_bwd_w(hn, dt, w, scale, *, name):
    S, D = hn.shape
    G, Cg, _ = w.shape
    T = _tile(S, POOL_TILE)

    def body(xm_ref, xh_ref, dt_ref, w_ref, sc_ref, dp_ref, dw_ref, ds_ref):
        g, i = pl.program_id(0), pl.program_id(1)
        win = jnp.left_shift(2, g)
        p = _pool_fwd_window(xm_ref[...], xh_ref[...], i * T, win).astype(BF16)
        dtv = dt_ref[...]
        ypre = jnp.dot(p, w_ref[...], preferred_element_type=F32)
        dy = (dtv * sc_ref[...]).astype(BF16)
        dp_ref[...] = lax.dot_general(dy, w_ref[...], (((1,), (1,)), ((), ())), preferred_element_type=F32)
        dwv = lax.dot_general(p, dy, (((0,), (0,)), ((), ())), preferred_element_type=F32)
        dsv = _colsum(dtv * ypre)

        @pl.when(i == 0)
        def _():
            dw_ref[...] = dwv
            ds_ref[...] = dsv

        @pl.when(i > 0)
        def _():
            dw_ref[...] += dwv
            ds_ref[...] += dsv

    main, halo = _pool_specs(T, Cg, "gi")
    return pl.pallas_call(
        body, grid=(G, S // T),
        in_specs=[main, halo, main, pl.BlockSpec((None, Cg, Cg), lambda g, i: (g, 0, 0)),
                  pl.BlockSpec((1, Cg), lambda g, i: (0, g))],
        out_specs=[main, pl.BlockSpec((None, Cg, Cg), lambda g, i: (g, 0, 0)), pl.BlockSpec((1, Cg), lambda g, i: (0, g))],
        out_shape=[jax.ShapeDtypeStruct((S, D), F32), jax.ShapeDtypeStruct((G, Cg, Cg), F32),
                   jax.ShapeDtypeStruct((1, D), F32)], name=name,
        compiler_params=_params(("parallel", "arbitrary"), 4 * T * Cg * 4),
    )(hn, hn, dt, w, scale)


def pool_bwd_x(dp, G, *, name):
    S, D = dp.shape
    Cg = D // G
    T = _tile(S, POOL_TILE)
    per = T // POOL_HALO
    last = S // POOL_HALO - 1

    def body(dm_ref, dh_ref, o_ref):
        i, g = pl.program_id(0), pl.program_id(1)
        o_ref[...] = _pool_bwd_window(dm_ref[...], dh_ref[...], i * T, jnp.left_shift(2, g), S)

    main = pl.BlockSpec((T, Cg), lambda i, g: (i, g))
    return pl.pallas_call(
        body, grid=(S // T, G),
        in_specs=[main, pl.BlockSpec((POOL_HALO, Cg), lambda i, g: (jnp.minimum((i + 1) * per, last), g))],
        out_specs=main, out_shape=jax.ShapeDtypeStruct((S, D), F32), name=name,
        compiler_params=_params(("parallel", "parallel"), 3 * T * Cg * 4),
    )(dp, dp)


def _sb_logs(z, mask):
    e = jnp.exp(-jnp.abs(z))
    sp = jnp.log(1.0 + e)
    ls = jnp.minimum(z, 0.0) - sp
    lsn = jnp.where(mask, jnp.minimum(-z, 0.0) - sp, 0.0)
    return ls, lsn, e


def _dot2r(x, band):
    hi, lo = _split_bf16(x)
    return jnp.dot(hi, band, preferred_element_type=F32) + jnp.dot(lo, band, preferred_element_type=F32)


def _dot3r(x, band):
    hi = x.astype(BF16)
    r1 = x - hi.astype(F32)
    mid = r1.astype(BF16)
    lo = (r1 - mid.astype(F32)).astype(BF16)
    return (jnp.dot(hi, band, preferred_element_type=F32) + jnp.dot(mid, band, preferred_element_type=F32)
            + jnp.dot(lo, band, preferred_element_type=F32))


def _head_masks(n_lanes):
    lane = lax.broadcasted_iota(jnp.int32, (1, n_lanes), 1)
    return [((lane >= h * SB_HEAD_DIM) & (lane < (h + 1) * SB_HEAD_DIM)) for h in range(n_lanes // SB_HEAD_DIM)]


_NT = (((1,), (1,)), ((), ()))
_TN = (((0,), (0,)), ((), ()))


SB_UNROLL = 4


def _sb_unroll(S):
    return SB_UNROLL if S % (SB_UNROLL * SB_BLOCK) == 0 else 1


SB_DEAD_LOG = -105.0


def _sb_loop(n_steps, step, init, *, rs_at):
    def cond(state):
        n, alive, _ = state
        return jnp.logical_and(n < n_steps, alive > 0)

    def body(state):
        n, _, carry = state
        carry = step(n, carry)
        top = functools.reduce(jnp.maximum, [jnp.max(r) for r in carry[rs_at]])
        return n + 1, (top > SB_DEAD_LOG).astype(jnp.int32), carry

    return lax.while_loop(cond, body, (jnp.int32(0), jnp.int32(1), init))[2]


def sb_fwd(qkv, *, name):
    S, D3 = qkv.shape
    D = D3 // 3
    B = SB_BLOCK
    npair = D // LANE
    scale = SB_HEAD_DIM ** -0.5

    U = _sb_unroll(S)

    def body(q_ref, k_ref, v_ref, o_ref):
        i = pl.program_id(1)
        masks = _head_masks(LANE)
        q = q_ref[...] * scale
        qh = [jnp.where(m, q, jnp.zeros_like(q)) for m in masks]
        row = lax.broadcasted_iota(jnp.int32, (B, B), 0)
        col = lax.broadcasted_iota(jnp.int32, (B, B), 1)
        upper = (row > col).astype(BF16)
        diag = col < row
        nsuper = i // U + 1

        def step(n, carry):
            os_, rs = list(carry[0]), list(carry[1])
            last = i + 1 - n * U
            first = jnp.maximum(last - U, 0)
            base = pl.multiple_of(first * B, B)
            kbig = k_ref[pl.ds(base, U * B), :]
            vbig = v_ref[pl.ds(base, U * B), :]
            tiles = [(c, hd) for c in reversed(range(U)) for hd in range(len(masks))]
            kb = {c: kbig[c * B:(c + 1) * B] for c in range(U)}
            vb = {c: vbig[c * B:(c + 1) * B] for c in range(U)}
            mask = {c: jnp.logical_and(first + c < last, jnp.logical_or(first + c < i, diag)) for c in range(U)}
            z = {t: lax.dot_general(qh[t[1]], kb[t[0]], _NT, preferred_element_type=F32) for t in tiles}
            ls, lsn = {}, {}
            for t in tiles:
                ls[t], lsn[t], _ = _sb_logs(z[t], mask[t[0]])
            local = {t: _dot2r(lsn[t], upper) for t in tiles}
            for c, hd in tiles:
                a = jnp.where(mask[c], jnp.exp(ls[(c, hd)] + local[(c, hd)] + rs[hd]), 0.0)
                os_[hd] = os_[hd] + jnp.dot(a.astype(BF16), vb[c], preferred_element_type=F32)
                rs[hd] = rs[hd] + jnp.sum(lsn[(c, hd)], axis=1, keepdims=True)
            return tuple(os_), tuple(rs)

        zero = jnp.zeros((B, 1), F32)
        zacc = jnp.zeros((B, LANE), F32)
        os_, _ = _sb_loop(nsuper, step, (tuple(zacc for _ in masks), tuple(zero for _ in masks)), rs_at=1)
        o = jnp.zeros((B, LANE), F32)
        for m, oh in zip(masks, os_):
            o = jnp.where(m, oh, o)
        o_ref[...] = o

    return pl.pallas_call(
        body, grid=(npair, S // B),
        in_specs=[pl.BlockSpec((B, LANE), lambda p, i: (i, p)),
                  pl.BlockSpec((S, LANE), lambda p, i: (0, npair + p)),
                  pl.BlockSpec((S, LANE), lambda p, i: (0, 2 * npair + p))],
        out_specs=pl.BlockSpec((B, LANE), lambda p, i: (i, p)),
        out_shape=jax.ShapeDtypeStruct((S, D), F32), name=name,
        compiler_params=_params(("parallel", "arbitrary"), 2 * S * LANE * 2),
    )(qkv, qkv, qkv)


def sb_bwd(qkv, o, do, *, name):
    S, D3 = qkv.shape
    D = D3 // 3
    B = SB_BLOCK
    npair = D // LANE
    scale = SB_HEAD_DIM ** -0.5

    U = _sb_unroll(S)

    def body(q_ref, k_ref, v_ref, o_ref, do_ref, dq_ref, dk_ref, dv_ref):
        i = pl.program_id(1)

        @pl.when(i == 0)
        def _():
            dk_ref[...] = jnp.zeros_like(dk_ref)
            dv_ref[...] = jnp.zeros_like(dv_ref)

        masks = _head_masks(LANE)
        q = q_ref[...] * scale
        dov = do_ref[...]
        ov = o_ref[...]
        qh = [jnp.where(m, q, jnp.zeros_like(q)) for m in masks]
        doh = [jnp.where(m, dov, 0.0).astype(BF16) for m in masks]
        gsum = [jnp.sum(dh_.astype(F32) * ov, axis=1, keepdims=True) for dh_ in doh]
        row = lax.broadcasted_iota(jnp.int32, (B, B), 0)
        col = lax.broadcasted_iota(jnp.int32, (B, B), 1)
        upper = (row > col).astype(BF16)
        upper_incl = (row >= col).astype(BF16)
        diag = col < row
        nsuper = i // U + 1

        def step(n, carry):
            dqs, rs, gs = list(carry[0]), list(carry[1]), list(carry[2])
            last = i + 1 - n * U
            first = jnp.maximum(last - U, 0)
            base = pl.multiple_of(first * B, B)
            kbig = k_ref[pl.ds(base, U * B), :]
            vbig = v_ref[pl.ds(base, U * B), :]
            nh = len(masks)
            tiles = [(c, hd) for c in reversed(range(U)) for hd in range(nh)]
            kb = {c: kbig[c * B:(c + 1) * B] for c in range(U)}
            vb = {c: vbig[c * B:(c + 1) * B] for c in range(U)}
            mask = {c: jnp.logical_and(first + c < last, jnp.logical_or(first + c < i, diag)) for c in range(U)}
            z = {t: lax.dot_general(qh[t[1]], kb[t[0]], _NT, preferred_element_type=F32) for t in tiles}
            da = {t: lax.dot_general(doh[t[1]], vb[t[0]], _NT, preferred_element_type=F32) for t in tiles}
            ls, lsn, sig = {}, {}, {}
            for t in tiles:
                ls[t], lsn[t], e = _sb_logs(z[t], mask[t[0]])
                sig[t] = jnp.exp(ls[t])
            local = {t: _dot2r(lsn[t], upper) for t in tiles}
            ab, g = {}, {}
            for c, hd in tiles:
                a = jnp.where(mask[c], jnp.exp(ls[(c, hd)] + local[(c, hd)] + rs[hd]), 0.0)
                ab[(c, hd)] = a.astype(BF16)
                g[(c, hd)] = ab[(c, hd)].astype(F32) * da[(c, hd)]
                rs[hd] = rs[hd] + jnp.sum(lsn[(c, hd)], axis=1, keepdims=True)
            glocal = {t: _dot3r(g[t], upper_incl) for t in tiles}
            dzb = {}
            for c, hd in tiles:
                t = (c, hd)
                sg = glocal[t] + gs[hd]
                dzb[t] = jnp.where(mask[c], g[t] * (1.0 - sig[t]) - (gsum[hd] - sg) * sig[t], 0.0).astype(BF16)
                gs[hd] = gs[hd] + jnp.sum(g[t], axis=1, keepdims=True)
            for c, hd in tiles:
                dqs[hd] = dqs[hd] + jnp.dot(dzb[(c, hd)], kb[c], preferred_element_type=F32)
            for c in reversed(range(U)):
                dkb = sum(lax.dot_general(dzb[(c, hd)], qh[hd], _TN, preferred_element_type=F32) for hd in range(nh))
                dvb = sum(lax.dot_general(ab[(c, hd)], doh[hd], _TN, preferred_element_type=F32) for hd in range(nh))
                off = pl.multiple_of(base + c * B, B)
                dk_ref[pl.ds(off, B), :] += dkb
                dv_ref[pl.ds(off, B), :] += dvb
            return tuple(dqs), tuple(rs), tuple(gs)

        zero = jnp.zeros((B, 1), F32)
        zs = tuple(zero for _ in masks)
        zacc = jnp.zeros((B, LANE), F32)
        dqs, _, _ = _sb_loop(nsuper, step, (tuple(zacc for _ in masks), zs, zs), rs_at=1)
        dq = jnp.zeros((B, LANE), F32)
        for m, dqh in zip(masks, dqs):
            dq = jnp.where(m, dqh, dq)
        dq_ref[...] = dq * scale

    tile = pl.BlockSpec((B, LANE), lambda p, i: (i, p))
    strip = pl.BlockSpec((S, LANE), lambda p, i: (0, p))
    return pl.pallas_call(
        body, grid=(npair, S // B),
        in_specs=[tile, pl.BlockSpec((S, LANE), lambda p, i: (0, npair + p)),
                  pl.BlockSpec((S, LANE), lambda p, i: (0, 2 * npair + p)), tile, tile],
        out_specs=[tile, strip, strip],
        out_shape=[jax.ShapeDtypeStruct((S, D), F32)] * 3, name=name,
        compiler_params=_params(("parallel", "arbitrary"), 2 * S * LANE * 2 + 2 * S * LANE * 4),
    )(qkv, qkv, qkv, o, do)


S5_HALF = S5_GROUPS_PER_BLOCK * S5_STATE
S5_BLOCK = 2 * S5_HALF


def s5_scan(bu, pw, *, reverse, name, tr=512):
    S, W = bu.shape
    nb = W // S5_BLOCK
    tr = _tile(S, tr, SUBLANE)
    nsub = tr // SUBLANE
    nt = S // tr
    H = S5_HALF

    def body(bu_ref, pw_ref, x_ref, st_re, st_im):
        i = pl.program_id(1)

        @pl.when(i == 0)
        def _():
            st_re[...] = jnp.zeros_like(st_re)
            st_im[...] = jnp.zeros_like(st_im)

        row = lax.broadcasted_iota(jnp.int32, (SUBLANE, H), 0)
        steps = []
        for k, sh in enumerate((1, 2, 4)):
            lo = SUBLANE * (k + 1)
            steps.append((sh, pw_ref[lo:lo + SUBLANE, 0:H], pw_ref[lo:lo + SUBLANE, H:2 * H]))
        ap_re = pw_ref[0:SUBLANE, 0:H]
        ap_im = pw_ref[0:SUBLANE, H:2 * H]
        edge = (row == 0) if reverse else (row == SUBLANE - 1)

        def sub(n, carry):
            s_re, s_im = carry
            j = (nsub - 1 - n) if reverse else n
            off = pl.multiple_of(j * SUBLANE, SUBLANE)
            r = bu_ref[pl.ds(off, SUBLANE), 0:H]
            m = bu_ref[pl.ds(off, SUBLANE), H:2 * H]
            for sh, a_re, a_im in steps:
                amt = (SUBLANE - sh) if reverse else sh
                rs = pltpu.roll(r, amt, 0)
                ms = pltpu.roll(m, amt, 0)
                r, m = r + a_re * rs - a_im * ms, m + a_re * ms + a_im * rs
            r, m = r + ap_re * s_re - ap_im * s_im, m + ap_re * s_im + ap_im * s_re
            x_ref[pl.ds(off, SUBLANE), 0:H] = r
            x_ref[pl.ds(off, SUBLANE), H:2 * H] = m
            return (jnp.sum(jnp.where(edge, r, 0.0), axis=0, keepdims=True),
                    jnp.sum(jnp.where(edge, m, 0.0), axis=0, keepdims=True))

        s_re, s_im = lax.fori_loop(0, nsub, sub, (st_re[...], st_im[...]), unroll=4)
        st_re[...] = s_re
        st_im[...] = s_im

    if reverse:
        tile = pl.BlockSpec((tr, S5_BLOCK), lambda b, i: (nt - 1 - i, b))
    else:
        tile = pl.BlockSpec((tr, S5_BLOCK), lambda b, i: (i, b))
    return pl.pallas_call(
        body, grid=(nb, nt),
        in_specs=[tile, pl.BlockSpec((4 * SUBLANE, S5_BLOCK), lambda b, i: (0, b))],
        out_specs=tile, out_shape=jax.ShapeDtypeStruct((S, W), F32),
        scratch_shapes=[pltpu.VMEM((1, H), F32), pltpu.VMEM((1, H), F32)], name=name,
        compiler_params=_params(("parallel", "arbitrary"), 2 * tr * S5_BLOCK * 4),
    )(bu, pw)


def s5_da(lam, x, *, name, tr=512):
    S, W = lam.shape
    nb = W // S5_BLOCK
    tr = _tile(S, tr, SUBLANE)
    nsub = tr // SUBLANE
    nt = S // tr
    H = S5_HALF

    def body(l_ref, x_ref, xh_ref, o_ref, acc_re, acc_im):
        i = pl.program_id(1)

        @pl.when(i == 0)
        def _():
            acc_re[...] = jnp.zeros_like(acc_re)
            acc_im[...] = jnp.zeros_like(acc_im)

        row = lax.broadcasted_iota(jnp.int32, (SUBLANE, H), 0)
        first = row == 0

        def sub(n, carry):
            a_re, a_im = carry
            off = pl.multiple_of(n * SUBLANE, SUBLANE)
            poff = pl.multiple_of(jnp.maximum(n - 1, 0) * SUBLANE, SUBLANE)
            inside = n > 0
            start = jnp.logical_and(i == 0, n == 0)
            out = []
            for lo in (0, H):
                cur = x_ref[pl.ds(off, SUBLANE), lo:lo + H]
                prv = jnp.where(inside, x_ref[pl.ds(poff, SUBLANE), lo:lo + H], xh_ref[:, lo:lo + H])
                xs = jnp.where(first, pltpu.roll(prv, 1, 0), pltpu.roll(cur, 1, 0))
                out.append(jnp.where(jnp.logical_and(start, first), 0.0, xs))
            xs_re, xs_im = out
            l_re = l_ref[pl.ds(off, SUBLANE), 0:H]
            l_im = l_ref[pl.ds(off, SUBLANE), H:2 * H]
            return a_re + l_re * xs_re + l_im * xs_im, a_im + l_im * xs_re - l_re * xs_im

        a_re, a_im = lax.fori_loop(0, nsub, sub, (acc_re[...], acc_im[...]), unroll=4)
        acc_re[...] = a_re
        acc_im[...] = a_im

        @pl.when(i == nt - 1)
        def _():
            o_ref[:, 0:H] = jnp.sum(a_re, axis=0, keepdims=True)
            o_ref[:, H:2 * H] = jnp.sum(a_im, axis=0, keepdims=True)

    per = tr // SUBLANE
    tile = pl.BlockSpec((tr, S5_BLOCK), lambda b, i: (i, b))
    return pl.pallas_call(
        body, grid=(nb, nt),
        in_specs=[tile, tile, pl.BlockSpec((SUBLANE, S5_BLOCK), lambda b, i: (jnp.maximum(i * per - 1, 0), b))],
        out_specs=pl.BlockSpec((1, S5_BLOCK), lambda b, i: (0, b)),
        out_shape=jax.ShapeDtypeStruct((1, W), F32),
        scratch_shapes=[pltpu.VMEM((SUBLANE, H), F32), pltpu.VMEM((SUBLANE, H), F32)], name=name,
        compiler_params=_params(("parallel", "arbitrary"), 2 * tr * S5_BLOCK * 4),
    )(lam, x, x)


def _gelu(y):
    c = math.sqrt(2.0 / math.pi)
    return 0.5 * y * (1.0 + jnp.tanh(c * (y + 0.044715 * y * y * y)))


def _gelu_grad(y):
    c = math.sqrt(2.0 / math.pi)
    th = jnp.tanh(c * (y + 0.044715 * y * y * y))
    return 0.5 * (1.0 + th) + 0.5 * y * (1.0 - th * th) * c * (1.0 + 3.0 * 0.044715 * y * y)


def _sigmoid(x):
    return 0.5 + 0.5 * jnp.tanh(0.5 * x)


def _xa_probs(qh, kh, scale):
    s = lax.dot_general(qh, kh, _NT, preferred_element_type=F32) * scale
    p = jnp.exp(s - jnp.max(s, axis=-1, keepdims=True))
    return p / jnp.sum(p, axis=-1, keepdims=True)


def xa_fwd(q, kv, *, name, tm=512):
    S, D = q.shape
    M = kv.shape[0]
    dh = D // XA_HEADS
    scale = dh ** -0.5
    tm = _tile(S, tm)

    def body(q_ref, kv_ref, o_ref):
        for h in range(XA_HEADS):
            p = _xa_probs(q_ref[:, h * dh:(h + 1) * dh], kv_ref[:, h * dh:(h + 1) * dh], scale)
            o_ref[:, h * dh:(h + 1) * dh] = jnp.dot(p.astype(BF16), kv_ref[:, D + h * dh:D + (h + 1) * dh],
                                                   preferred_element_type=F32).astype(BF16)

    return pl.pallas_call(
        body, grid=(S // tm,),
        in_specs=[pl.BlockSpec((tm, D), lambda i: (i, 0)), pl.BlockSpec((M, 2 * D), lambda i: (0, 0))],
        out_specs=pl.BlockSpec((tm, D), lambda i: (i, 0)),
        out_shape=jax.ShapeDtypeStruct((S, D), BF16), name=name,
        compiler_params=_params(("parallel",), 2 * tm * D * 2 + M * 2 * D * 2),
    )(q, kv)


def xa_bwd(q, kv, do, *, name, tm=512):
    S, D = q.shape
    M = kv.shape[0]
    dh = D // XA_HEADS
    scale = dh ** -0.5
    tm = _tile(S, tm)

    def body(q_ref, kv_ref, do_ref, dq_ref, dkv_ref):
        i = pl.program_id(0)

        @pl.when(i == 0)
        def _():
            dkv_ref[...] = jnp.zeros_like(dkv_ref)

        for h in range(XA_HEADS):
            sl = slice(h * dh, (h + 1) * dh)
            vsl = slice(D + h * dh, D + (h + 1) * dh)
            qh, kh, vh = q_ref[:, sl], kv_ref[:, sl], kv_ref[:, vsl]
            doh = do_ref[:, sl].astype(BF16)
            p = _xa_probs(qh, kh, scale)
            dp = lax.dot_general(doh, vh, _NT, preferred_element_type=F32)
            ds = (p * (dp - jnp.sum(dp * p, axis=-1, keepdims=True)) * scale).astype(BF16)
            dq_ref[:, sl] = jnp.dot(ds, kh, preferred_element_type=F32).astype(BF16)
            dkv_ref[:, sl] += lax.dot_general(ds, qh, _TN, preferred_element_type=F32)
            dkv_ref[:, vsl] += lax.dot_general(p.astype(BF16), doh, _TN, preferred_element_type=F32)

    return pl.pallas_call(
        body, grid=(S // tm,),
        in_specs=[pl.BlockSpec((tm, D), lambda i: (i, 0)), pl.BlockSpec((M, 2 * D), lambda i: (0, 0)),
                  pl.BlockSpec((tm, D), lambda i: (i, 0))],
        out_specs=[pl.BlockSpec((tm, D), lambda i: (i, 0)), pl.BlockSpec((M, 2 * D), lambda i: (0, 0))],
        out_shape=[jax.ShapeDtypeStruct((S, D), BF16), jax.ShapeDtypeStruct((M, 2 * D), F32)], name=name,
        compiler_params=_params(("arbitrary",), 3 * tm * D * 2 + M * 2 * D * 6),
    )(q, kv, do)


FFN_STRIP = 256
FFN_ROWS = 512


def _shift_down(x, k):
    return pltpu.roll(x, k, 0)


def _shift_up(x, k):
    return pltpu.roll(x, x.shape[0] - k, 0)


FFN_HALO = 2 * SUBLANE


def _rows_with_prev(u_ref, r0, R):
    cur = u_ref[pl.ds(r0, R), :].astype(F32)
    p0 = pl.multiple_of(jnp.maximum(r0 - FFN_HALO, 0), FFN_HALO)
    prev = jnp.where(r0 > 0, u_ref[pl.ds(p0, FFN_HALO), :].astype(F32), 0.0)
    return jnp.concatenate([prev, cur], axis=0), cur


def _conv_rows(u_ref, r0, R, w_ref, b_ref):
    ext, _ = _rows_with_prev(u_ref, r0, R)
    out = w_ref[2:3, :] * ext + w_ref[1:2, :] * _shift_down(ext, 1) + w_ref[0:1, :] * _shift_down(ext, 2) + b_ref[...]
    return out[FFN_HALO:, :]


def ffn_act_fwd(u, conv_w, conv_b, *, name):
    S, F2 = u.shape
    F = F2 // 2
    tc = _tile(F, FFN_STRIP)
    nc = F // tc
    R = _tile(S, FFN_ROWS, 16)

    def body(uv_ref, ug_ref, wv_ref, wg_ref, bv_ref, bg_ref, o_ref):
        def rows(n, _):
            r0 = pl.multiple_of(n * R, R)
            val = _conv_rows(uv_ref, r0, R, wv_ref, bv_ref)
            gate = _conv_rows(ug_ref, r0, R, wg_ref, bg_ref)
            o_ref[pl.ds(r0, R), :] = (gate * _sigmoid(gate) * val).astype(BF16)
            return 0

        lax.fori_loop(0, S // R, rows, 0)

    return pl.pallas_call(
        body, grid=(nc,),
        in_specs=[pl.BlockSpec((S, tc), lambda c: (0, c)), pl.BlockSpec((S, tc), lambda c: (0, nc + c)),
                  pl.BlockSpec((CONV_WIDTH, tc), lambda c: (0, c)), pl.BlockSpec((CONV_WIDTH, tc), lambda c: (0, nc + c)),
                  pl.BlockSpec((1, tc), lambda c: (0, c)), pl.BlockSpec((1, tc), lambda c: (0, nc + c))],
        out_specs=pl.BlockSpec((S, tc), lambda c: (0, c)),
        out_shape=jax.ShapeDtypeStruct((S, F), BF16), name=name,
        compiler_params=_params(("parallel",), 3 * S * tc * 2),
    )(u, u, conv_w, conv_w, conv_b, conv_b)


def ffn_act_bwd(u, dact, conv_w, conv_b, *, name):
    S, F2 = u.shape
    F = F2 // 2
    tc = _tile(F, FFN_STRIP)
    nc = F // tc
    R = _tile(S, FFN_ROWS, 16)
    nr = S // R
    HALO = FFN_HALO

    def body(uv_ref, ug_ref, da_ref, wv_ref, wg_ref, bv_ref, bg_ref,
             duv_ref, dug_ref, dwv_ref, dwg_ref, dbv_ref, dbg_ref, dcv, dcg):
        def p1(n, _):
            r0 = pl.multiple_of(n * R, R)
            val = _conv_rows(uv_ref, r0, R, wv_ref, bv_ref)
            gate = _conv_rows(ug_ref, r0, R, wg_ref, bg_ref)
            d = da_ref[pl.ds(r0, R), :].astype(F32)
            sg = _sigmoid(gate)
            dcv[pl.ds(r0, R), :] = d * gate * sg
            dcg[pl.ds(r0, R), :] = d * val * (sg + gate * sg * (1.0 - sg))
            return 0

        lax.fori_loop(0, nr, p1, 0)

        def p2(n, carry):
            r0 = pl.multiple_of(n * R, R)
            nxt = pl.multiple_of(jnp.minimum(r0 + R, S - HALO), HALO)
            new = []
            for u_ref, dc, w_ref, du_ref, acc in ((uv_ref, dcv, wv_ref, duv_ref, carry[0]),
                                                  (ug_ref, dcg, wg_ref, dug_ref, carry[1])):
                d = dc[pl.ds(r0, R), :]
                after = jnp.where(r0 + R < S, dc[pl.ds(nxt, HALO), :], 0.0)
                ext = jnp.concatenate([d, after], axis=0)
                du = w_ref[2:3, :] * ext + w_ref[1:2, :] * _shift_up(ext, 1) + w_ref[0:1, :] * _shift_up(ext, 2)
                du_ref[pl.ds(r0, R), :] = du[:R, :].astype(BF16)
                uext, cur = _rows_with_prev(u_ref, r0, R)
                u1 = _shift_down(uext, 1)[HALO:, :]
                u2 = _shift_down(uext, 2)[HALO:, :]
                dw2, dw1, dw0, db = acc
                new.append((dw2 + _colsum(d * cur), dw1 + _colsum(d * u1), dw0 + _colsum(d * u2), db + _colsum(d)))
            return tuple(new)

        z = jnp.zeros((1, tc), F32)
        accs = lax.fori_loop(0, nr, p2, ((z, z, z, z), (z, z, z, z)))
        for (dw2, dw1, dw0, db), dw_ref, db_ref in ((accs[0], dwv_ref, dbv_ref), (accs[1], dwg_ref, dbg_ref)):
            dw_ref[0:1, :] = dw0
            dw_ref[1:2, :] = dw1
            dw_ref[2:3, :] = dw2
            db_ref[...] = db

    strip_v = pl.BlockSpec((S, tc), lambda c: (0, c))
    strip_g = pl.BlockSpec((S, tc), lambda c: (0, nc + c))
    w_v = pl.BlockSpec((CONV_WIDTH, tc), lambda c: (0, c))
    w_g = pl.BlockSpec((CONV_WIDTH, tc), lambda c: (0, nc + c))
    b_v = pl.BlockSpec((1, tc), lambda c: (0, c))
    b_g = pl.BlockSpec((1, tc), lambda c: (0, nc + c))
    outs = pl.pallas_call(
        body, grid=(nc,),
        in_specs=[strip_v, strip_g, strip_v, w_v, w_g, b_v, b_g],
        out_specs=[strip_v, strip_v, w_v, w_v, b_v, b_v],
        out_shape=[jax.ShapeDtypeStruct((S, F), BF16), jax.ShapeDtypeStruct((S, F), BF16),
                   jax.ShapeDtypeStruct((CONV_WIDTH, F), F32), jax.ShapeDtypeStruct((CONV_WIDTH, F), F32),
                   jax.ShapeDtypeStruct((1, F), F32), jax.ShapeDtypeStruct((1, F), F32)],
        scratch_shapes=[pltpu.VMEM((S, tc), F32), pltpu.VMEM((S, tc), F32)], name=name,
        compiler_params=_params(("parallel",), 5 * S * tc * 2 + S * tc * 4),
    )(u, u, dact, conv_w, conv_w, conv_b, conv_b)
    duv, dug, dwv, dwg, dbv, dbg = outs
    return duv, dug, jnp.concatenate([dwv, dwg], axis=1), jnp.concatenate([dbv, dbg], axis=1)


def _s5_discretize(a_re, a_im, log_dt, b_re, b_im):
    lam = lax.complex(a_re, a_im)
    dt_lam = lam * jnp.exp(log_dt)[:, None]
    a_bar = jnp.exp(dt_lam)
    b_bar = ((a_bar - 1.0) / lam)[..., None] * lax.complex(b_re, b_im)
    return jnp.real(a_bar), jnp.imag(a_bar), jnp.real(b_bar), jnp.imag(b_bar)


def _s5_cols(z_re, z_im, nb):
    lead = z_re.shape[:-2]
    re = z_re.reshape(*lead, nb, S5_HALF)
    im = z_im.reshape(*lead, nb, S5_HALF)
    return jnp.concatenate([re, im], axis=-1).reshape(*lead, nb * S5_BLOCK)


def _s5_powers(a_re, a_im, log_dt, nb, *, reverse):
    dt_lam = lax.complex(a_re, a_im) * jnp.exp(log_dt)[:, None]
    if reverse:
        dt_lam = jnp.conj(dt_lam)
        carry = jnp.arange(SUBLANE, 0, -1, dtype=F32)
    else:
        carry = jnp.arange(1, SUBLANE + 1, dtype=F32)
    row = jnp.arange(SUBLANE)
    ks, keep = [carry], [jnp.ones((SUBLANE,), F32)]
    for sh in (1, 2, 4):
        ks.append(jnp.full((SUBLANE,), float(sh), F32))
        keep.append(((row < SUBLANE - sh) if reverse else (row >= sh)).astype(F32))
    ks, keep = jnp.concatenate(ks), jnp.concatenate(keep)
    pw = jnp.exp(ks[:, None, None] * dt_lam[None]) * keep[:, None, None]
    return _s5_cols(jnp.real(pw), jnp.imag(pw), nb)


def _s5_in_weights(bb_re, bb_im, nb):
    eye = jnp.eye(S5_GROUPS_PER_BLOCK, dtype=F32)
    G, P, Cg = bb_re.shape

    def one(bb):
        t = jnp.einsum("bgpi,gh->bgihp", bb.reshape(nb, S5_GROUPS_PER_BLOCK, P, Cg), eye)
        return t.reshape(nb, S5_GROUPS_PER_BLOCK * Cg, S5_HALF)

    return jnp.concatenate([one(bb_re), one(bb_im)], axis=2)


def _s5_out_weights(c_re, c_im, nb):
    eye = jnp.eye(S5_GROUPS_PER_BLOCK, dtype=F32)
    G, Cg, P = c_re.shape

    def one(c):
        t = jnp.einsum("bgip,gh->bgphi", c.reshape(nb, S5_GROUPS_PER_BLOCK, Cg, P), eye)
        return t.reshape(nb, S5_HALF, S5_GROUPS_PER_BLOCK * Cg)

    return jnp.concatenate([one(c_re), -one(c_im)], axis=1)


def _s5_in_weight_grads(dwb, Cg):
    nb = dwb.shape[0]
    eye = jnp.eye(S5_GROUPS_PER_BLOCK, dtype=F32)
    t = dwb.reshape(nb, S5_GROUPS_PER_BLOCK, Cg, 2, S5_GROUPS_PER_BLOCK, S5_STATE)
    out = jnp.einsum("bgirhp,gh->rbgpi", t, eye)
    return out[0].reshape(-1, S5_STATE, Cg), out[1].reshape(-1, S5_STATE, Cg)


def _s5_out_weight_grads(dwc, Cg):
    nb = dwc.shape[0]
    eye = jnp.eye(S5_GROUPS_PER_BLOCK, dtype=F32)
    t = dwc.reshape(nb, 2, S5_GROUPS_PER_BLOCK, S5_STATE, S5_GROUPS_PER_BLOCK, Cg)
    out = jnp.einsum("brgphi,gh->rbgip", t, eye)
    return out[0].reshape(-1, Cg, S5_STATE), -out[1].reshape(-1, Cg, S5_STATE)


ANY = pl.BlockSpec(memory_space=pl.ANY)
N_CHIPS = 4
N_DEV = 8


def _place():
    x, y, c = lax.axis_index("x"), lax.axis_index("y"), lax.axis_index("c")
    chips = [(1 - x, y), (x, 1 - y), (1 - x, 1 - y)]
    return x, y, c, chips


def gather_chips(w, *, name):
    R, C = w.shape
    Hh = R // 2
    assert 2 * Hh == R

    def body(w_ref, out_ref, send_sems, recv_sems):
        x, y, c, chips = _place()
        me = 2 * x + y
        sibling = (x, y, 1 - c)

        def half(chip, hc):
            return out_ref.at[chip, pl.ds(hc * Hh, Hh), :]

        def copy(k, src, dst, to):
            return pltpu.make_async_remote_copy(src_ref=src, dst_ref=dst, send_sem=send_sems.at[k],
                                                recv_sem=recv_sems.at[k], device_id=to, device_id_type=MESH)

        first = [copy(j, w_ref.at[pl.ds(c * Hh, Hh), :], half(me, c), (px, py, c)) for j, (px, py) in enumerate(chips)]
        for cp in first:
            cp.start()
        passed = []
        for j, (px, py) in enumerate(chips):
            landed = half(2 * px + py, c)
            copy(j, landed, landed, (px, py, c)).wait_recv()
            fwd = copy(3 + j, landed, landed, sibling)
            fwd.start()
            passed.append(fwd)
        for j, (px, py) in enumerate(chips):
            theirs = half(2 * px + py, 1 - c)
            copy(3 + j, theirs, theirs, sibling).wait_recv()
        for cp in first + passed:
            cp.wait_send()

    others = pl.pallas_call(
        body, in_specs=[ANY], out_specs=ANY, out_shape=jax.ShapeDtypeStruct((N_CHIPS, R, C), w.dtype),
        scratch_shapes=[pltpu.SemaphoreType.DMA((6,)), pltpu.SemaphoreType.DMA((6,))], name=name,
    )(w)
    return _place_rows(others, w, 2 * lax.axis_index("x") + lax.axis_index("y"), name=name + "_own")


def _place_rows(buf, rows, slot, *, name, tr=592):
    n, R, C = buf.shape
    tr = _tile(R, tr, 16)
    idx = jnp.asarray(slot, jnp.int32).reshape(1)

    def body(s_ref, r_ref, b_ref, o_ref):
        o_ref[...] = r_ref[...]

    return pl.pallas_call(
        body,
        grid_spec=pltpu.PrefetchScalarGridSpec(
            num_scalar_prefetch=1, grid=(R // tr,),
            in_specs=[pl.BlockSpec((tr, C), lambda i, s: (i, 0)), ANY],
            out_specs=pl.BlockSpec((None, tr, C), lambda i, s: (s[0], i, 0))),
        out_shape=jax.ShapeDtypeStruct(buf.shape, buf.dtype), input_output_aliases={2: 0}, name=name,
        compiler_params=_params(("parallel",), 2 * tr * C * 4),
    )(idx, rows, buf)


def swap_halves(g4, *, name):
    n, R, C = g4.shape
    Hh = R // 2

    def body(g_ref, out_ref, send_sem, recv_sem):
        x, y, c, _ = _place()
        cp = pltpu.make_async_remote_copy(
            src_ref=g_ref.at[pl.ds(0, n), pl.ds((1 - c) * Hh, Hh), :], dst_ref=out_ref, send_sem=send_sem,
            recv_sem=recv_sem, device_id=(x, y, 1 - c), device_id_type=MESH)
        cp.start()
        cp.wait()

    return pl.pallas_call(
        body, in_specs=[ANY], out_specs=ANY, out_shape=jax.ShapeDtypeStruct((n, Hh, C), g4.dtype),
        scratch_shapes=[pltpu.SemaphoreType.DMA, pltpu.SemaphoreType.DMA], name=name,
    )(g4)


def add_half(g4, other, *, name, tr=160):
    n, R, C = g4.shape
    Hh = R // 2
    tr = _tile(Hh, tr, 16)
    nblk = Hh // tr
    cidx = lax.axis_index("c").astype(jnp.int32).reshape(1)

    def body(c_ref, g_ref, o_ref, out_ref):
        out_ref[...] = (g_ref[...].astype(F32) + o_ref[...].astype(F32)).astype(out_ref.dtype)

    return pl.pallas_call(
        body,
        grid_spec=pltpu.PrefetchScalarGridSpec(
            num_scalar_prefetch=1, grid=(nblk,),
            in_specs=[pl.BlockSpec((n, tr, C), lambda i, c_ref: (0, c_ref[0] * nblk + i, 0)),
                      pl.BlockSpec((n, tr, C), lambda i, c_ref: (0, i, 0))],
            out_specs=pl.BlockSpec((n, tr, C), lambda i, c_ref: (0, i, 0))),
        out_shape=jax.ShapeDtypeStruct((n, Hh, C), g4.dtype), name=name,
        compiler_params=_params(("parallel",), 3 * n * tr * C * 2),
    )(cidx, g4, other)


def scatter_chips(p4, *, name):
    n, Hh, C = p4.shape

    def body(p_ref, out_ref, send_sems, recv_sems):
        x, y, c, chips = _place()
        me = 2 * x + y
        sends = []
        for j, (px, py) in enumerate(chips):
            cp = pltpu.make_async_remote_copy(src_ref=p_ref.at[2 * px + py], dst_ref=out_ref.at[me], send_sem=send_sems.at[j],
                                              recv_sem=recv_sems.at[j], device_id=(px, py, c), device_id_type=MESH)
            cp.start()
            sends.append(cp)
        for j, (px, py) in enumerate(chips):
            slot = out_ref.at[2 * px + py]
            pltpu.make_async_remote_copy(src_ref=slot, dst_ref=slot, send_sem=send_sems.at[j], recv_sem=recv_sems.at[j],
                                         device_id=(px, py, c), device_id_type=MESH).wait_recv()
        for cp in sends:
            cp.wait_send()

    return pl.pallas_call(
        body, in_specs=[ANY], out_specs=ANY, out_shape=jax.ShapeDtypeStruct((n, Hh, C), p4.dtype),
        scratch_shapes=[pltpu.SemaphoreType.DMA((3,)), pltpu.SemaphoreType.DMA((3,))], name=name,
    )(p4)


def sum_chips(landed, part, *, name, tr=96):
    n, Hh, C = landed.shape
    tr = _tile(Hh, tr, 16)
    nblk = Hh // tr
    x, y, c, _ = _place()
    idx = jnp.stack([2 * x + y, c]).astype(jnp.int32)

    def slot_spec(k):
        return pl.BlockSpec((None, tr, C), lambda i, s: (jnp.where(s[0] == k, (k + 1) % n, k), i, 0))

    def body(s_ref, *refs):
        slots, own_ref, o_ref = refs[:n], refs[n], refs[n + 1]
        own = own_ref[...].astype(F32)
        acc = None
        for k in range(n):
            v = jnp.where(s_ref[0] == k, own, slots[k][...].astype(F32))
            acc = v if acc is None else acc + v
        o_ref[...] = acc

    return pl.pallas_call(
        body,
        grid_spec=pltpu.PrefetchScalarGridSpec(
            num_scalar_prefetch=1, grid=(nblk,),
            in_specs=[slot_spec(k) for k in range(n)] + [pl.BlockSpec((None, tr, C), lambda i, s: (s[0], i, 0))],
            out_specs=pl.BlockSpec((tr, C), lambda i, s: (s[1] * nblk + i, 0))),
        out_shape=jax.ShapeDtypeStruct((2 * Hh, C), F32), name=name,
        compiler_params=_params(("parallel",), 6 * tr * C * 4),
    )(idx, *([landed] * n), part)


def sum_leading(x3, *, name, tr=160, align=16):
    n, R, C = x3.shape
    tr = _tile(R, tr, align)

    def body(x_ref, o_ref):
        acc = x_ref[0].astype(F32)
        for k in range(1, n):
            acc = acc + x_ref[k].astype(F32)
        o_ref[...] = acc

    return pl.pallas_call(
        body, grid=(R // tr,), in_specs=[pl.BlockSpec((n, tr, C), lambda i: (0, i, 0))],
        out_specs=pl.BlockSpec((tr, C), lambda i: (i, 0)), out_shape=jax.ShapeDtypeStruct((R, C), F32), name=name,
        compiler_params=_params(("parallel",), n * tr * C * 4 + tr * C * 4),
    )(x3)


def join_halves(r, *, name):
    R, C = r.shape
    Hh = R // 2

    def body(r_ref, out_ref, send_sem, recv_sem):
        x, y, c, _ = _place()
        mine = out_ref.at[pl.ds(c * Hh, Hh), :]
        theirs = out_ref.at[pl.ds((1 - c) * Hh, Hh), :]
        cp = pltpu.make_async_remote_copy(src_ref=mine, dst_ref=mine, send_sem=send_sem, recv_sem=recv_sem,
                                          device_id=(x, y, 1 - c), device_id_type=MESH)
        cp.start()
        pltpu.make_async_remote_copy(src_ref=theirs, dst_ref=theirs, send_sem=send_sem, recv_sem=recv_sem,
                                     device_id=(x, y, 1 - c), device_id_type=MESH).wait_recv()
        cp.wait_send()

    return pl.pallas_call(
        body, in_specs=[ANY], out_specs=ANY, out_shape=jax.ShapeDtypeStruct((R, C), r.dtype),
        input_output_aliases={0: 0}, scratch_shapes=[pltpu.SemaphoreType.DMA, pltpu.SemaphoreType.DMA], name=name,
    )(r)


def gather_devices(v, *, name):
    m_per, n = v.shape

    def body(x_ref, out_ref, send_sems, recv_sems, local_sem):
        x, y, c, chips = _place()
        me, sibling = (x, y, c), (x, y, 1 - c)

        def rows(px, py, pc):
            return out_ref.at[pl.ds((4 * px + 2 * py + pc) * m_per, m_per), :]

        def copy(k, block, to, src=None):
            return pltpu.make_async_remote_copy(src_ref=rows(*block) if src is None else src, dst_ref=rows(*block),
                                                send_sem=send_sems.at[k], recv_sem=recv_sems.at[k], device_id=to,
                                                device_id_type=MESH)

        mine = pltpu.make_async_copy(x_ref, rows(*me), local_sem)
        mine.start()
        first = [copy(0, me, sibling, src=x_ref)]
        first += [copy(1 + j, me, (*chip, c), src=x_ref) for j, chip in enumerate(chips)]
        for cp in first:
            cp.start()
        passed = [copy(4 + j, (*chip, c), sibling) for j, chip in enumerate(chips)]
        for j, chip in enumerate(chips):
            copy(1 + j, (*chip, c), me).wait_recv()
            passed[j].start()
        copy(0, sibling, me).wait_recv()
        for j, chip in enumerate(chips):
            copy(4 + j, (*chip, 1 - c), me).wait_recv()
        for cp in first + passed:
            cp.wait_send()
        mine.wait()

    return pl.pallas_call(
        body, out_shape=jax.ShapeDtypeStruct((N_DEV * m_per, n), v.dtype),
        in_specs=[pl.BlockSpec(memory_space=pltpu.VMEM)], out_specs=pl.BlockSpec(memory_space=pltpu.VMEM),
        scratch_shapes=[pltpu.SemaphoreType.DMA((7,)), pltpu.SemaphoreType.DMA((7,)), pltpu.SemaphoreType.DMA], name=name,
        compiler_params=pltpu.CompilerParams(vmem_limit_bytes=_vmem_limit(9 * m_per * n * 4)),
    )(v)


def reduce_weight_grads(g4):
    other = swap_halves(g4, name="rs_swap_halves")
    part = add_half(g4, other, name="rs_add_half")
    landed = scatter_chips(part, name="rs_scatter_chips")
    mine = sum_chips(landed, part, name="rs_sum_chips")
    return join_halves(mine, name="rs_join_halves")


PACK_COLS = 1024
BIG = (("pool_w", 2), ("sb_w_qkv", 2), ("sb_w_o", 1), ("s5_w_glu", 2), ("xa_wq", 1), ("xa_wkv", 2), ("xa_wo", 1),
       ("ffn_w_up", 2), ("ffn_w_down", 1))
SMALL_SHARDED = (("pool_scale", 1), ("s5_d", 1), ("ffn_conv_w", 2))
REPLICATED = ("mix_norm_g", "s5_a_re", "s5_a_im", "s5_log_dt", "s5_b_re", "s5_b_im", "s5_c_re", "s5_c_im",
              "xa_norm_g", "mem_norm_g", "ffn_norm_g", "ffn_conv_b", "final_norm_g")
WEIGHTS = ("mix_norm_g", "pool_w", "pool_scale", "sb_w_qkv", "sb_w_o", "s5_a_re", "s5_a_im", "s5_log_dt", "s5_b_re",
           "s5_b_im", "s5_c_re", "s5_c_im", "s5_d", "s5_w_glu", "xa_norm_g", "mem_norm_g", "xa_wq", "xa_wkv", "xa_wo",
           "ffn_norm_g", "ffn_w_up", "ffn_conv_w", "ffn_conv_b", "ffn_w_down", "final_norm_g")


def _pack_rows(parts, row_align):
    flat = jnp.concatenate(parts, axis=-1)
    n = flat.shape[-1]
    per = PACK_COLS * row_align
    padded = -(-n // per) * per
    if padded != n:
        flat = jnp.pad(flat, [(0, 0)] * (flat.ndim - 1) + [(0, padded - n)])
    return flat.reshape(*flat.shape[:-1], padded // PACK_COLS, PACK_COLS)


def _to_natural(g, ax):
    g = jnp.moveaxis(g, 0, ax)
    sh = g.shape
    return g.reshape(*sh[:ax], sh[ax] * sh[ax + 1], *sh[ax + 2:])


def _to_chunks(a, ax, n=N_CHIPS):
    sh = a.shape
    a = a.reshape(*sh[:ax], n, sh[ax] // n, *sh[ax + 1:])
    return jnp.moveaxis(a, ax, 0).reshape(n, -1)


def _unpack(flat, shapes):
    out, off = [], 0
    for sh in shapes:
        n = math.prod(sh)
        out.append(flat[..., off:off + n].reshape(*flat.shape[:-1], *sh))
        off += n
    return out


def _mixer_kind(i):
    return i % 3, i // 3


def _s5_forward(hn, h, s5, tag):
    bu = bdmm(hn, s5["w_in"], out_dtype=F32, name=f"{tag}_s5_bu")
    xs = s5_scan(bu, s5["pw_fwd"], reverse=False, name=f"{tag}_s5_scan")
    ycx = bdmm(xs, s5["w_out"], out_dtype=F32, name=f"{tag}_s5_cx")
    D = hn.shape[1]

    def post(i, yv, uv, dv):
        return [_gelu(yv + dv * uv)]

    yg = ew(post, [(ycx, "tile"), (hn, "tile"), (s5["d"], "full")], [(D, BF16, "tile")], rows=hn.shape[0], tr=256,
            name=f"{tag}_s5_gelu")[0]
    vg = mm(yg, s5["w_glu"], out_dtype=F32, name=f"{tag}_s5_glu")

    def glu(i, vgv, hv):
        return [hv + vgv[:, :D] * _sigmoid(vgv[:, D:])]

    h1 = ew(glu, [(vg, "tile"), (h, "tile")], [(D, F32, "tile")], rows=hn.shape[0], tr=256, name=f"{tag}_s5_gate")[0]
    return h1, dict(xs=xs, ycx=ycx, yg=yg, vg=vg)


def _s5_backward(hn, dout, s5, sv, tag):
    S, D = hn.shape

    def dglu(i, vgv, dv):
        sg = _sigmoid(vgv[:, D:])
        return [jnp.concatenate([dv * sg, dv * vgv[:, :D] * sg * (1.0 - sg)], axis=1)]

    dvg = ew(dglu, [(sv["vg"], "tile"), (dout, "tile")], [(2 * D, BF16, "tile")], rows=S, tr=256, name=f"{tag}_s5_dgate")[0]
    dyg = mm(dvg, s5["w_glu"], tb=True, out_dtype=F32, name=f"{tag}_s5_dglu_x")
    dw_glu = mm(sv["yg"], dvg, ta=True, out_dtype=BF16, name=f"{tag}_s5_dglu_w")

    def dgelu(i, dygv, yv, uv, dv):
        dyp = dygv * _gelu_grad(yv + dv * uv)
        return [dyp, _colsum(dyp * uv)]

    dyp, dd = ew(dgelu, [(dyg, "tile"), (sv["ycx"], "tile"), (hn, "tile"), (s5["d"], "full")],
                 [(D, F32, "tile"), (D, F32, "acc")], rows=S, tr=256, name=f"{tag}_s5_dgelu")
    gx = bdmm(dyp, s5["w_out_t"], out_dtype=F32, name=f"{tag}_s5_dcx")
    dw_out = bdmm_tn(sv["xs"], dyp, ka=S5_BLOCK, kd=S5_GROUPS_PER_BLOCK * S5_GROUP, name=f"{tag}_s5_dwout")
    lam = s5_scan(gx, s5["pw_bwd"], reverse=True, name=f"{tag}_s5_scan_bwd")
    da = s5_da(lam, sv["xs"], name=f"{tag}_s5_da")
    dw_in = bdmm_tn(hn, lam, ka=S5_GROUPS_PER_BLOCK * S5_GROUP, kd=S5_BLOCK, name=f"{tag}_s5_dwin")
    du = bdmm(lam, s5["w_in_t"], out_dtype=F32, name=f"{tag}_s5_du")

    def dsum(i, duv, dypv, dv):
        return [duv + dypv * dv]

    dhn = ew(dsum, [(du, "tile"), (dyp, "tile"), (s5["d"], "full")], [(D, F32, "tile")], rows=S, tr=256,
             name=f"{tag}_s5_dhn")[0]
    return dhn, dict(dw_glu=dw_glu, dd=dd, dw_out=dw_out, dw_in=dw_in, da=da)


def _step(x, mem, target, w, m, v):
    S, D = x.shape
    depth = w["mix_norm_g"].shape[0]
    F = w["ffn_w_down"].shape[1] * N_CHIPS
    chip = 2 * lax.axis_index("x") + lax.axis_index("y")

    big_shapes = [w[n].shape for n, _ in BIG]
    packed = _pack_rows([w[n].astype(BF16).reshape(-1) for n, _ in BIG], 32)
    gathered = gather_chips(packed, name="ag_weights").reshape(N_CHIPS, -1)
    full = {n: _to_natural(p, ax) for (n, ax), p in zip(BIG, _unpack(gathered, big_shapes))}
    small_shapes = [w[n].shape for n, _ in SMALL_SHARDED]
    spacked = _pack_rows([w[n].reshape(-1) for n, _ in SMALL_SHARDED], SUBLANE)
    sgathered = gather_devices(spacked, name="ag_small").reshape(N_CHIPS, 2, -1)[:, 0]
    full.update({n: _to_natural(p, ax) for (n, ax), p in zip(SMALL_SHARDED, _unpack(sgathered, small_shapes))})

    n_s5 = w["s5_a_re"].shape[0]
    s5 = []
    for j in range(n_s5):
        G = w["s5_a_re"].shape[1]
        nb = G // S5_GROUPS_PER_BLOCK
        prm = (w["s5_a_re"][j], w["s5_a_im"][j], w["s5_log_dt"][j], w["s5_b_re"][j], w["s5_b_im"][j])
        (ab_re, ab_im, bb_re, bb_im), disc_vjp = jax.vjp(_s5_discretize, *prm)
        w_in = _s5_in_weights(bb_re, bb_im, nb)
        w_out = _s5_out_weights(w["s5_c_re"][j], w["s5_c_im"][j], nb)
        s5.append(dict(
            w_in=w_in, w_in_t=jnp.transpose(w_in, (0, 2, 1)), w_out=w_out, w_out_t=jnp.transpose(w_out, (0, 2, 1)),
            pw_fwd=_s5_powers(prm[0], prm[1], prm[2], nb, reverse=False),
            pw_bwd=_s5_powers(prm[0], prm[1], prm[2], nb, reverse=True),
            d=full["s5_d"][j][None], w_glu=full["s5_w_glu"][j], vjp=disc_vjp, nb=nb))

    h = x
    saved = []
    for i in range(depth):
        kind, j = _mixer_kind(i)
        tag = f"L{i}"
        sv = dict(h=h)
        g_mix = w["mix_norm_g"][i][None]
        if kind == 0:
            hn = rms_fwd(h, g_mix, out_dtype=F32, name=f"{tag}_mix_norm")
            h1 = pool_fwd(hn, h, full["pool_w"][j], full["pool_scale"][j][None], name=f"{tag}_pool")
        elif kind == 1:
            hn = rms_fwd(h, g_mix, out_dtype=BF16, name=f"{tag}_mix_norm")
            qkv = mm(hn, full["sb_w_qkv"][j], out_dtype=BF16, name=f"{tag}_sb_qkv")
            o = sb_fwd(qkv, name=f"{tag}_sb_attn")
            h1 = mm(o, full["sb_w_o"][j], res=h, name=f"{tag}_sb_out")
            sv.update(qkv=qkv, o=o)
        else:
            hn = rms_fwd(h, g_mix, out_dtype=F32, name=f"{tag}_mix_norm")
            h1, s5sv = _s5_forward(hn, h, s5[j], tag)
            sv.update(s5sv)
        sv.update(hn=hn, h1=h1)
        hq = rms_fwd(h1, w["xa_norm_g"][i][None], out_dtype=BF16, name=f"{tag}_xa_norm")
        memn = rms_fwd(mem, w["mem_norm_g"][i][None], out_dtype=BF16, name=f"{tag}_mem_norm", tr=mem.shape[0])
        q = mm(hq, full["xa_wq"][i], out_dtype=BF16, name=f"{tag}_xa_q")
        kv = mm(memn, full["xa_wkv"][i], out_dtype=BF16, name=f"{tag}_xa_kv")
        oa = xa_fwd(q, kv, name=f"{tag}_xa_attn")
        h2 = mm(oa, full["xa_wo"][i], res=h1, name=f"{tag}_xa_out")
        hf = rms_fwd(h2, w["ffn_norm_g"][i][None], out_dtype=BF16, name=f"{tag}_ffn_norm")
        uu = mm(hf, full["ffn_w_up"][i], out_dtype=BF16, tn=1408, name=f"{tag}_ffn_up")
        conv_w, conv_b = full["ffn_conv_w"][i], w["ffn_conv_b"][i][None]
        act = ffn_act_fwd(uu, conv_w, conv_b, name=f"{tag}_ffn_act")
        h3 = mm(act, full["ffn_w_down"][i], res=h2, tk=1408, name=f"{tag}_ffn_down")
        sv.update(hq=hq, memn=memn, q=q, kv=kv, oa=oa, h2=h2, hf=hf, uu=uu, act=act)
        saved.append(sv)
        h = h3

    dh, g_final, loss = loss_head(h, w["final_norm_g"][None], target, name="loss_head")

    gw = {n: [None] * w[n].shape[0] for n in WEIGHTS if n != "final_norm_g"}
    for i in reversed(range(depth)):
        kind, j = _mixer_kind(i)
        tag = f"L{i}b"
        sv = saved[i]
        conv_w, conv_b = full["ffn_conv_w"][i], w["ffn_conv_b"][i][None]
        dact = mm(dh, full["ffn_w_down"][i], tb=True, out_dtype=BF16, tn=1408, name=f"{tag}_ffn_down_x")
        gw["ffn_w_down"][i] = mm(sv["act"], dh, ta=True, out_dtype=BF16, tm=1408, name=f"{tag}_ffn_down_w")
        duv, dug, dcw, dcb = ffn_act_bwd(sv["uu"], dact, conv_w, conv_b, name=f"{tag}_ffn_act")
        gw["ffn_conv_w"][i], gw["ffn_conv_b"][i] = dcw, dcb[0]
        dhf = mm(duv, full["ffn_w_up"][i], tb=True, b_col0=0, tk=1408, name=f"{tag}_ffn_up_xv")
        dhf = mm(dug, full["ffn_w_up"][i], tb=True, b_col0=F, res=dhf, tk=1408, name=f"{tag}_ffn_up_xg")
        gw["ffn_w_up"][i] = jnp.concatenate(
            [mm(sv["hf"], duv, ta=True, out_dtype=BF16, tn=1408, name=f"{tag}_ffn_up_wv"),
             mm(sv["hf"], dug, ta=True, out_dtype=BF16, tn=1408, name=f"{tag}_ffn_up_wg")], axis=1)
        dh2, dg = rms_bwd(sv["h2"], w["ffn_norm_g"][i][None], dhf, dh, name=f"{tag}_ffn_norm")
        gw["ffn_norm_g"][i] = dg[0]

        doa = mm(dh2, full["xa_wo"][i], tb=True, out_dtype=BF16, name=f"{tag}_xa_out_x")
        gw["xa_wo"][i] = mm(sv["oa"], dh2, ta=True, out_dtype=BF16, name=f"{tag}_xa_out_w")
        dq, dkv = xa_bwd(sv["q"], sv["kv"], doa, name=f"{tag}_xa_attn")
        dhq = mm(dq, full["xa_wq"][i], tb=True, name=f"{tag}_xa_q_x")
        gw["xa_wq"][i] = mm(sv["hq"], dq, ta=True, out_dtype=BF16, name=f"{tag}_xa_q_w")
        dmemn = mm(dkv, full["xa_wkv"][i], tb=True, name=f"{tag}_xa_kv_x")
        gw["xa_wkv"][i] = mm(sv["memn"], dkv, ta=True, out_dtype=BF16, name=f"{tag}_xa_kv_w")
        gw["mem_norm_g"][i] = rms_bwd_g(mem, dmemn, name=f"{tag}_mem_norm")[0]
        dh1, dg = rms_bwd(sv["h1"], w["xa_norm_g"][i][None], dhq, dh2, name=f"{tag}_xa_norm")
        gw["xa_norm_g"][i] = dg[0]

        g_mix = w["mix_norm_g"][i][None]
        if kind == 0:
            dp, dpw, dps = pool_bwd_w(sv["hn"], dh1, full["pool_w"][j], full["pool_scale"][j][None], name=f"{tag}_pool_w")
            gw["pool_w"][j], gw["pool_scale"][j] = dpw, dps[0]
            dhn = pool_bwd_x(dp, len(POOL_WINDOWS), name=f"{tag}_pool_x")
        elif kind == 1:
            do = mm(dh1, full["sb_w_o"][j], tb=True, name=f"{tag}_sb_out_x")
            gw["sb_w_o"][j] = mm(sv["o"], dh1, ta=True, out_dtype=BF16, name=f"{tag}_sb_out_w")
            dq3 = sb_bwd(sv["qkv"], sv["o"], do, name=f"{tag}_sb_attn")
            dqkv = jnp.concatenate([t.astype(BF16) for t in dq3], axis=1)
            dhn = mm(dqkv, full["sb_w_qkv"][j], tb=True, name=f"{tag}_sb_qkv_x")
            gw["sb_w_qkv"][j] = mm(sv["hn"], dqkv, ta=True, out_dtype=BF16, name=f"{tag}_sb_qkv_w")
        else:
            dhn, sg = _s5_backward(sv["hn"], dh1, s5[j], sv, tag)
            Cg = w["s5_b_re"].shape[-1]
            nb = s5[j]["nb"]
            gw["s5_w_glu"][j], gw["s5_d"][j] = sg["dw_glu"], sg["dd"][0]
            da = sg["da"].reshape(nb, 2, -1)
            gw["s5_a_re"][j], gw["s5_a_im"][j] = da[:, 0].reshape(-1, S5_STATE), da[:, 1].reshape(-1, S5_STATE)
            gw["s5_b_re"][j], gw["s5_b_im"][j] = _s5_in_weight_grads(sg["dw_in"], Cg)
            gw["s5_c_re"][j], gw["s5_c_im"][j] = _s5_out_weight_grads(sg["dw_out"], Cg)
        dh, dg = rms_bwd(sv["h"], g_mix, dhn, dh1, name=f"{tag}_mix_norm")
        gw["mix_norm_g"][i] = dg[0]
    grad_x = dh

    g4 = _pack_rows([_to_chunks(jnp.stack(gw[n]).astype(BF16), ax) for n, ax in BIG], 32)
    reduced = reduce_weight_grads(g4).reshape(-1)
    grads = dict(zip([n for n, _ in BIG], _unpack(reduced, big_shapes)))

    s5_raw = ("s5_a_re", "s5_a_im", "s5_b_re", "s5_b_im")
    small_names = [n for n in REPLICATED if n not in ("final_norm_g", "s5_log_dt")] + [n for n, _ in SMALL_SHARDED]
    small_full = [jnp.stack(gw[n]).astype(F32) for n in small_names] + [g_final[0]]
    small_full_shapes = [t.shape for t in small_full]
    spk = _pack_rows([t.reshape(-1) for t in small_full], SUBLANE)
    everyone = gather_devices(spk, name="ar_small_gather").reshape(N_DEV, *spk.shape)
    ssum = sum_leading(everyone, name="ar_small_sum", align=SUBLANE).reshape(-1)
    small = dict(zip(small_names + ["final_norm_g"], _unpack(ssum, small_full_shapes)))
    per_layer = [[], [], [], [], []]
    for j in range(n_s5):
        ct = tuple(small[n][j] for n in s5_raw)
        for lst, gpart in zip(per_layer, s5[j]["vjp"]((ct[0], ct[1], ct[2], ct[3]))):
            lst.append(gpart)
    for n, lst in zip(("s5_a_re", "s5_a_im", "s5_log_dt", "s5_b_re", "s5_b_im"), per_layer):
        small[n] = jnp.stack(lst)
    for n, ax in SMALL_SHARDED:
        chunks = _to_chunks(small[n], ax)
        small[n] = lax.dynamic_index_in_dim(chunks, chip, 0, keepdims=False).reshape(w[n].shape)
    for n in REPLICATED:
        grads[n] = small[n].reshape(w[n].shape)
    for n, _ in SMALL_SHARDED:
        grads[n] = small[n]

    delta, new_m, new_v = {}, {}, {}
    for n in WEIGHTS:
        delta[n], new_m[n], new_v[n] = adamw(w[n], grads[n], m[n], v[n], name=f"adamw_{n}")
    total = lax.psum(loss[0, 0], ("x", "y", "c"))
    return (total, grad_x, *[grads[n] for n in WEIGHTS], *[delta[n] for n in WEIGHTS],
            *[new_m[n] for n in WEIGHTS], *[new_v[n] for n in WEIGHTS])


def kernel(x, mem, mix_norm_g, pool_w, pool_scale, sb_w_qkv, sb_w_o, s5_a_re, s5_a_im, s5_log_dt, s5_b_re, s5_b_im,
           s5_c_re, s5_c_im, s5_d, s5_w_glu, xa_norm_g, mem_norm_g, xa_wq, xa_wkv, xa_wo, ffn_norm_g, ffn_w_up,
           ffn_conv_w, ffn_conv_b, ffn_w_down, final_norm_g, loss_target, m_mix_norm_g, m_pool_w, m_pool_scale,
           m_sb_w_qkv, m_sb_w_o, m_s5_a_re, m_s5_a_im, m_s5_log_dt, m_s5_b_re, m_s5_b_im, m_s5_c_re, m_s5_c_im,
           m_s5_d, m_s5_w_glu, m_xa_norm_g, m_mem_norm_g, m_xa_wq, m_xa_wkv, m_xa_wo, m_ffn_norm_g, m_ffn_w_up,
           m_ffn_conv_w, m_ffn_conv_b, m_ffn_w_down, m_final_norm_g, v_mix_norm_g, v_pool_w, v_pool_scale,
           v_sb_w_qkv, v_sb_w_o, v_s5_a_re, v_s5_a_im, v_s5_log_dt, v_s5_b_re, v_s5_b_im, v_s5_c_re, v_s5_c_im,
           v_s5_d, v_s5_w_glu, v_xa_norm_g, v_mem_norm_g, v_xa_wq, v_xa_wkv, v_xa_wo, v_ffn_norm_g, v_ffn_w_up,
           v_ffn_conv_w, v_ffn_conv_b, v_ffn_w_down, v_final_norm_g):
    given = dict(locals())
    w = {n: given[n] for n in WEIGHTS}
    m = {n: given["m_" + n] for n in WEIGHTS}
    v = {n: given["v_" + n] for n in WEIGHTS}
    out = _step(x[0], mem[0], loss_target[0], w, m, v)
    return (out[0], out[1][None], *out[2:])
```
